```python
import math
import jax, jax.numpy as jnp
from jax import lax
import numpy as np

D_MODEL = 1024
BATCH = 16
SEQ = 256
DEPTH = 2
DEC_BATCH = 8
DEC_SEQ = 1024
PAST_LEN = 256

GRID_W = 64
N_MIXERS = 2
N_LAYERS_A = (DEPTH + 1) // 2
N_LAYERS_B = DEPTH // 2
CONV_W = 5
CHUNK = 64
E_A = 2 * D_MODEL
H_A = 4
DH_A = E_A // H_A
QKV_BLOCK = 4
E_B = 2 * D_MODEL
P_B = 64
H_B = E_B // P_B
G_B = 4
HG_B = H_B // G_B
D_STATE = 128
MOE_GROUPS = 4
MOE_EXPERTS = 4
TOP_K_E = 2
D_FF_E = 256
ALPHA = (2.0 * DEPTH) ** 0.25
BETA = (8.0 * DEPTH) ** -0.25
EPS = 1e-5

kernel_name = 'hybrid_mlstm_ssd_hmoe_diffusion_step'


def _layer_norm(x, w, b):
    xf = x.astype(jnp.float32)
    mu = jnp.mean(xf, axis=-1, keepdims=True)
    var = jnp.mean(jnp.square(xf - mu), axis=-1, keepdims=True)
    y = (xf - mu) * lax.rsqrt(var + EPS) * w.astype(jnp.float32) + b.astype(jnp.float32)
    return y.astype(x.dtype)


def _rms_norm(x, w):
    xf = x.astype(jnp.float32)
    return xf * lax.rsqrt(jnp.mean(jnp.square(xf), axis=-1, keepdims=True) + EPS) * w.astype(jnp.float32)


def _dwconv(x, w, b):
    y = lax.conv_general_dilated(x, w[:, None, :].astype(x.dtype), window_strides=(1,),
                                 padding=[(CONV_W // 2, CONV_W // 2)],
                                 dimension_numbers=('NWC', 'WIO', 'NWC'),
                                 feature_group_count=x.shape[-1])
    return y + b


def _flip(t):
    return jnp.flip(t, axis=1)


def _grid_to_cols(x):
    b, l, d = x.shape
    rows = l // GRID_W
    return x.reshape(b, rows, GRID_W, d).transpose(0, 2, 1, 3).reshape(b, l, d)


def _cols_to_grid(x):
    b, l, d = x.shape
    rows = l // GRID_W
    return x.reshape(b, GRID_W, rows, d).transpose(0, 2, 1, 3).reshape(b, l, d)


def _mlstm_chunk_scan(q, k, v, ig, lf, C0, n0, m0):
    bsz, l, h, dh = q.shape
    nc = l // CHUNK
    scale = dh ** -0.5
    mask = jnp.tril(jnp.ones((CHUNK, CHUNK), dtype=bool))

    def chunks(t):
        t = t.astype(jnp.float32).reshape((bsz, nc, CHUNK) + t.shape[2:])
        return jnp.moveaxis(jnp.moveaxis(t, 1, 0), 2, 3)

    def step(carry, inp):
        C, n, m = carry
        qc, kc, vc, ic, fc = inp
        qc = qc * scale
        b = jnp.cumsum(fc, axis=-1)
        logw = jnp.where(mask, b[..., :, None] - b[..., None, :] + ic[..., None, :], -jnp.inf)
        inter = b + m[..., None]
        m_s = jnp.maximum(inter, jnp.max(logw, axis=-1))
        s = jnp.einsum('bhsd,bhrd->bhsr', qc, kc) * jnp.exp(logw - m_s[..., None])
        w_inter = jnp.exp(inter - m_s)
        num = w_inter[..., None] * jnp.einsum('bhsd,bhde->bhse', qc, C) + jnp.einsum('bhsr,bhre->bhse', s, vc)
        den = w_inter * jnp.einsum('bhsd,bhd->bhs', qc, n) + jnp.sum(s, axis=-1)
        h_out = num / jnp.maximum(jnp.abs(den), jnp.exp(-m_s))[..., None]
        b_last = b[..., -1]
        logk = b_last[..., None] - b + ic
        m_new = jnp.maximum(b_last + m, jnp.max(logk, axis=-1))
        decay = jnp.exp(b_last + m - m_new)
        wk = jnp.exp(logk - m_new[..., None])
        C_new = decay[..., None, None] * C + jnp.einsum('bhr,bhrd,bhre->bhde', wk, kc, vc)
        n_new = decay[..., None] * n + jnp.einsum('bhr,bhrd->bhd', wk, kc)
        return (C_new, n_new, m_new), h_out

    carry0 = (C0.astype(jnp.float32), n0.astype(jnp.float32), m0.astype(jnp.float32))
    (C, n, m), hs = lax.scan(step, carry0, (chunks(q), chunks(k), chunks(v), chunks(ig), chunks(lf)))
    hs = hs.transpose(1, 0, 3, 2, 4).reshape(bsz, l, h, dh)
    return hs, C, n, m


def _ssd_chunk_scan(x, dt, A, Bm, Cm, h0):
    bsz, l = x.shape[:2]
    nc = l // CHUNK
    mask = jnp.tril(jnp.ones((CHUNK, CHUNK), dtype=bool))

    def chunks(t):
        return jnp.moveaxis(t.astype(jnp.float32).reshape((bsz, nc, CHUNK) + t.shape[2:]), 1, 0)

    def step(hst, inp):
        xc, dtc, Bc, Cc = inp
        acs = jnp.cumsum(dtc * A, axis=1)
        seg = acs[:, :, None] - acs[:, None, :]
        decay = jnp.exp(jnp.where(mask[None, :, :, None, None], seg, -jnp.inf))
        cb = jnp.einsum('bsgn,brgn->bsrg', Cc, Bc)
        w = cb[..., None] * decay * dtc[:, None]
        y = jnp.einsum('bsrgh,brghp->bsghp', w, xc)
        y = y + jnp.einsum('bsgn,bghpn->bsghp', Cc, hst) * jnp.exp(acs)[..., None]
        a_last = acs[:, -1]
        wr = jnp.exp(a_last[:, None] - acs) * dtc
        h_new = jnp.exp(a_last)[..., None, None] * hst + jnp.einsum('brgh,brghp,brgn->bghpn', wr, xc, Bc)
        return h_new, y

    hst, ys = lax.scan(step, h0.astype(jnp.float32), (chunks(x), chunks(dt), chunks(Bm), chunks(Cm)))
    return jnp.moveaxis(ys, 0, 1).reshape(x.shape), hst


def _mlstm_mixer(h, w_in, conv_w, conv_b, w_q, w_k, w_v, w_gates, b_gates, norm_w, skip, w_down, C0, n0, m0):
    bsz, l, _ = h.shape
    xu, z = jnp.split(h @ w_in, 2, axis=-1)
    xc = jax.nn.silu(_dwconv(xu, conv_w, conv_b))

    def blockdiag(t, w):
        return jnp.einsum('blnc,ncd->blnd', t.reshape(bsz, l, E_A // QKV_BLOCK, QKV_BLOCK), w).reshape(bsz, l, E_A)

    q = blockdiag(xc, w_q)
    k = blockdiag(xc, w_k)
    v = blockdiag(xu, w_v)
    gates = (jnp.concatenate([q, k, v], axis=-1) @ w_gates + b_gates).astype(jnp.float32)
    gates = gates.reshape(bsz, l, 2, 2, H_A)
    ig = gates[:, :, :, 0]
    lf = jax.nn.log_sigmoid(gates[:, :, :, 1])
    qh, kh, vh = (t.reshape(bsz, l, H_A, DH_A) for t in (q, k, v))
    hf, Cf, nf, mf = _mlstm_chunk_scan(qh, kh, vh, ig[:, :, 0], lf[:, :, 0], C0[:, 0], n0[:, 0], m0[:, 0])
    hb, Cb, nb, mb = _mlstm_chunk_scan(_flip(qh), _flip(kh), _flip(vh), _flip(ig[:, :, 1]), _flip(lf[:, :, 1]),
                                       C0[:, 1], n0[:, 1], m0[:, 1])
    hs = hf + _flip(hb)
    mu = jnp.mean(hs, axis=-1, keepdims=True)
    var = jnp.mean(jnp.square(hs - mu), axis=-1, keepdims=True)
    hn = ((hs - mu) * lax.rsqrt(var + EPS)).reshape(bsz, l, E_A) * norm_w
    out = ((hn + skip * xc) * jax.nn.silu(z)).astype(h.dtype) @ w_down
    return out, jnp.stack([Cf, Cb], axis=1), jnp.stack([nf, nb], axis=1), jnp.stack([mf, mb], axis=1)


def _ssd_mixer(h, w_in, conv_w, conv_b, dt_bias, a_log, d_skip, norm_w, w_out, h0):
    bsz, l, _ = h.shape
    gn = G_B * D_STATE
    proj = h @ w_in
    z = proj[..., :E_B]
    xbc = jax.nn.silu(_dwconv(proj[..., E_B:2 * E_B + 2 * gn], conv_w, conv_b))
    dt_raw = proj[..., 2 * E_B + 2 * gn:]
    x = xbc[..., :E_B].reshape(bsz, l, G_B, HG_B, P_B)
    Bm = xbc[..., E_B:E_B + gn].reshape(bsz, l, G_B, D_STATE)
    Cm = xbc[..., E_B + gn:].reshape(bsz, l, G_B, D_STATE)
    dt = jax.nn.softplus(dt_raw.astype(jnp.float32).reshape(bsz, l, 2, G_B, HG_B)
                         + dt_bias.astype(jnp.float32).reshape(2, G_B, HG_B))
    A = -jnp.exp(a_log.astype(jnp.float32)).reshape(2, G_B, HG_B)
    h0g = h0.reshape(bsz, 2, G_B, HG_B, P_B, D_STATE)
    yf, sf = _ssd_chunk_scan(x, dt[:, :, 0], A[0], Bm, Cm, h0g[:, 0])
    yb, sb = _ssd_chunk_scan(_flip(x), _flip(dt[:, :, 1]), A[1], _flip(Bm), _flip(Cm), h0g[:, 1])
    y = yf + _flip(yb) + d_skip.reshape(G_B, HG_B)[..., None] * x
    y = _rms_norm(y.reshape(bsz, l, E_B) * jax.nn.silu(z), norm_w)
    out = y.astype(h.dtype) @ w_out
    return out, jnp.stack([sf, sb], axis=1).reshape(bsz, 2, H_B, P_B, D_STATE)


def _hier_moe(h, w_rg, b_rg, w_re, b_re, w_gate, w_up, w_down):
    glog = (h @ w_rg + b_rg).astype(jnp.float32)
    gp, gi = lax.top_k(jax.nn.softmax(glog, axis=-1), 1)
    oh_g = jax.nn.one_hot(gi[..., 0], MOE_GROUPS, dtype=jnp.float32)
    elog = (jnp.einsum('bld,gde->blge', h, w_re) + b_re).astype(jnp.float32)
    elog_sel = jnp.einsum('blge,blg->ble', elog, oh_g)
    ev, ei = lax.top_k(elog_sel, TOP_K_E)
    ew = jax.nn.softmax(ev, axis=-1) * gp
    we = jnp.einsum('blk,blke->ble', ew, jax.nn.one_hot(ei, MOE_EXPERTS, dtype=jnp.float32))
    combine = (oh_g[..., None] * we[:, :, None, :]).astype(h.dtype)
    a = jax.nn.silu(jnp.einsum('bld,gedf->blgef', h, w_gate)) * jnp.einsum('bld,gedf->blgef', h, w_up)
    return jnp.einsum('blgef,gefd->bld', a * combine[..., None], w_down)


def setup_inputs(seed: int = 0) -> dict:
    key = jax.random.key(seed)
    ks = iter(jax.random.split(key, 48))
    f32 = jnp.float32

    def nrm(shape, scale):
        return scale * jax.random.normal(next(ks), shape, f32)

    gn = G_B * D_STATE
    b_i = nrm((N_LAYERS_A, 2, 1, H_A), 0.1)
    b_f = jnp.linspace(3.0, 6.0, H_A, dtype=f32) + nrm((N_LAYERS_A, 2, 1, H_A), 0.1)
    dt0 = jnp.exp(jax.random.uniform(next(ks), (N_LAYERS_B, 2, H_B), f32, math.log(1e-3), math.log(1e-1)))
    return {
        'x_prompt': nrm((BATCH, SEQ, D_MODEL), 1.0),
        'x_sample': nrm((DEC_BATCH, DEC_SEQ, D_MODEL), 1.0),
        'state_mlstm_C': nrm((DEC_BATCH, N_LAYERS_A, 2, H_A, DH_A, DH_A), 0.02),
        'state_mlstm_n': nrm((DEC_BATCH, N_LAYERS_A, 2, H_A, DH_A), 0.1),
        'state_mlstm_m': nrm((DEC_BATCH, N_LAYERS_A, 2, H_A), 0.5),
        'state_ssd': nrm((DEC_BATCH, N_LAYERS_B, 2, H_B, P_B, D_STATE), 0.1),
        'c': nrm((DEC_BATCH, D_MODEL), 1.0),
        'c_ctx': nrm((D_MODEL,), 1.0),
        'ada_w': nrm((DEPTH, D_MODEL, 6 * D_MODEL), 0.5 * D_MODEL ** -0.5),
        'ada_b': nrm((DEPTH, 6 * D_MODEL), 0.02),
        'ln_w': 1.0 + nrm((DEPTH, 2, D_MODEL), 0.01),
        'ln_b': nrm((DEPTH, 2, D_MODEL), 0.01),
        'a_w_in': nrm((N_LAYERS_A, D_MODEL, 2 * E_A), D_MODEL ** -0.5),
        'a_conv_w': nrm((N_LAYERS_A, CONV_W, E_A), CONV_W ** -0.5),
        'a_conv_b': nrm((N_LAYERS_A, E_A), 0.01),
        'a_w_q': nrm((N_LAYERS_A, E_A // QKV_BLOCK, QKV_BLOCK, QKV_BLOCK), QKV_BLOCK ** -0.5),
        'a_w_k': nrm((N_LAYERS_A, E_A // QKV_BLOCK, QKV_BLOCK, QKV_BLOCK), QKV_BLOCK ** -0.5),
        'a_w_v': nrm((N_LAYERS_A, E_A // QKV_BLOCK, QKV_BLOCK, QKV_BLOCK), QKV_BLOCK ** -0.5),
        'a_w_gates': nrm((N_LAYERS_A, 3 * E_A, 4 * H_A), (3 * E_A) ** -0.5),
        'a_b_gates': jnp.concatenate([b_i, b_f], axis=2).reshape(N_LAYERS_A, 4 * H_A),
        'a_norm_w': 1.0 + nrm((N_LAYERS_A, E_A), 0.01),
        'a_skip': 1.0 + nrm((N_LAYERS_A, E_A), 0.01),
        'a_w_down': nrm((N_LAYERS_A, E_A, D_MODEL), BETA * E_A ** -0.5),
        'b_w_in': nrm((N_LAYERS_B, D_MODEL, 2 * E_B + 2 * gn + 2 * H_B), D_MODEL ** -0.5),
        'b_conv_w': nrm((N_LAYERS_B, CONV_W, E_B + 2 * gn), CONV_W ** -0.5),
        'b_conv_b': nrm((N_LAYERS_B, E_B + 2 * gn), 0.01),
        'b_dt_bias': dt0 + jnp.log(-jnp.expm1(-dt0)),
        'b_a_log': jnp.log(jax.random.uniform(next(ks), (N_LAYERS_B, 2, H_B), f32, 1.0, 16.0)),
        'b_d_skip': 1.0 + nrm((N_LAYERS_B, H_B), 0.01),
        'b_norm_w': 1.0 + nrm((N_LAYERS_B, E_B), 0.01),
        'b_w_out': nrm((N_LAYERS_B, E_B, D_MODEL), BETA * E_B ** -0.5),
        'moe_w_rg': nrm((DEPTH, D_MODEL, MOE_GROUPS), D_MODEL ** -0.5),
        'moe_b_rg': nrm((DEPTH, MOE_GROUPS), 0.01),
        'moe_w_re': nrm((DEPTH, MOE_GROUPS, D_MODEL, MOE_EXPERTS), D_MODEL ** -0.5),
        'moe_b_re': nrm((DEPTH, MOE_GROUPS, MOE_EXPERTS), 0.01),
        'moe_w_gate': nrm((DEPTH, MOE_GROUPS, MOE_EXPERTS, D_MODEL, D_FF_E), D_MODEL ** -0.5),
        'moe_w_up': nrm((DEPTH, MOE_GROUPS, MOE_EXPERTS, D_MODEL, D_FF_E), D_MODEL ** -0.5),
        'moe_w_down': nrm((DEPTH, MOE_GROUPS, MOE_EXPERTS, D_FF_E, D_MODEL), BETA * D_FF_E ** -0.5),
    }


def reference(x_prompt, x_sample, state_mlstm_C, state_mlstm_n, state_mlstm_m, state_ssd, c, c_ctx,
              ada_w, ada_b, ln_w, ln_b,
              a_w_in, a_conv_w, a_conv_b, a_w_q, a_w_k, a_w_v, a_w_gates, a_b_gates, a_norm_w, a_skip, a_w_down,
              b_w_in, b_conv_w, b_conv_b, b_dt_bias, b_a_log, b_d_skip, b_norm_w, b_w_out,
              moe_w_rg, moe_b_rg, moe_w_re, moe_b_re, moe_w_gate, moe_w_up, moe_w_down):
    xp, xs = x_prompt, x_sample
    bp = xp.shape[0]
    new_C, new_n, new_m, new_s = [], [], [], []
    for i in range(DEPTH):
        j = i // N_MIXERS
        mp = (jax.nn.silu(c_ctx) @ ada_w[i] + ada_b[i]).reshape(6, D_MODEL)
        ms = (jax.nn.silu(c) @ ada_w[i] + ada_b[i]).reshape(-1, 6, 1, D_MODEL)
        hp = xp * (1.0 + mp[1]) + mp[0]
        hs = xs * (1.0 + ms[:, 1]) + ms[:, 0]
        if i % N_MIXERS == 0:
            wa = (a_w_in[j], a_conv_w[j], a_conv_b[j], a_w_q[j], a_w_k[j], a_w_v[j], a_w_gates[j], a_b_gates[j],
                  a_norm_w[j], a_skip[j], a_w_down[j])
            yp, Cn, nn_, mn = _mlstm_mixer(hp, *wa,
                                          jnp.zeros((bp, 2, H_A, DH_A, DH_A), jnp.float32),
                                          jnp.zeros((bp, 2, H_A, DH_A), jnp.float32),
                                          jnp.zeros((bp, 2, H_A), jnp.float32))
            ys, _, _, _ = _mlstm_mixer(hs, *wa, state_mlstm_C[:, j], state_mlstm_n[:, j], state_mlstm_m[:, j])
            new_C.append(Cn)
            new_n.append(nn_)
            new_m.append(mn)
        else:
            wb = (b_w_in[j], b_conv_w[j], b_conv_b[j], b_dt_bias[j], b_a_log[j], b_d_skip[j], b_norm_w[j], b_w_out[j])
            yp, sn = _ssd_mixer(hp, *wb, jnp.zeros((bp, 2, H_B, P_B, D_STATE), jnp.float32))
            ys, _ = _ssd_mixer(_grid_to_cols(hs), *wb, state_ssd[:, j])
            ys = _cols_to_grid(ys)
            new_s.append(sn)
        xp = _layer_norm(ALPHA * xp + mp[2] * yp, ln_w[i, 0], ln_b[i, 0])
        xs = _layer_norm(ALPHA * xs + ms[:, 2] * ys, ln_w[i, 0], ln_b[i, 0])
        wm = (moe_w_rg[i], moe_b_rg[i], moe_w_re[i], moe_b_re[i], moe_w_gate[i], moe_w_up[i], moe_w_down[i])
        fp = _hier_moe(xp * (1.0 + mp[4]) + mp[3], *wm)
        fs = _hier_moe(xs * (1.0 + ms[:, 4]) + ms[:, 3], *wm)
        xp = _layer_norm(ALPHA * xp + mp[5] * fp, ln_w[i, 1], ln_b[i, 1])
        xs = _layer_norm(ALPHA * xs + ms[:, 5] * fs, ln_w[i, 1], ln_b[i, 1])
    return (xp, xs, jnp.stack(new_C, axis=1), jnp.stack(new_n, axis=1), jnp.stack(new_m, axis=1), jnp.stack(new_s, axis=1))
```

```python
import functools

import jax
import jax.numpy as jnp
from jax import lax
from jax.experimental import pallas as pl
from jax.experimental.pallas import tpu as pltpu

F32 = jnp.float32
BF16 = jnp.bfloat16

D_MODEL = 1024
DEPTH = 2
GRID_W = 64
CONV_W = 5
E_A = 2 * D_MODEL
H_A = 4
DH_A = E_A // H_A
QKV_BLOCK = 4
E_B = 2 * D_MODEL
P_B = 64
H_B = E_B // P_B
G_B = 4
HG_B = H_B // G_B
D_STATE = 128
GN_B = G_B * D_STATE
MOE_GROUPS = 4
MOE_EXPERTS = 4
N_EXPERTS = MOE_GROUPS * MOE_EXPERTS
D_FF_E = 256
ALPHA = (2.0 * DEPTH) ** 0.25
EPS = 1e-5

LANES = 128
SUBLANES = 8
TM = 256
TM_MOE = 512
SCAN_CHUNK = 256
MOD_ROWS = 8
N_MOD_SEQ = 16
VMEM_LIMIT = 48 * 1024 * 1024
ROUTER_OFF = MOE_GROUPS


def _params(sem):
    return pltpu.CompilerParams(dimension_semantics=sem, vmem_limit_bytes=VMEM_LIMIT)


def _seq_of_tile(m, tm, tp, ls):
    npt = tp // tm
    return jnp.where(m < npt, 0, 1 + (m - npt) // (ls // tm))


def _silu(x):
    return x * jax.nn.sigmoid(x)


def _softplus(x):
    return jnp.maximum(x, 0.0) + jnp.log1p(jnp.exp(-jnp.abs(x)))


def _layer_norm_rows(r, w, b):
    mu = jnp.mean(r, axis=-1, keepdims=True)
    rc = r - mu
    var = jnp.mean(rc * rc, axis=-1, keepdims=True)
    return rc * lax.rsqrt(var + EPS) * w + b


def _ada_kernel(c_ref, w_ref, b_ref, o_ref):
    s = _silu(c_ref[...])
    o_ref[...] = jnp.dot(s.astype(BF16), w_ref[...].astype(BF16), preferred_element_type=F32) + b_ref[...]


def _ada(cvec, ada_w, ada_b):
    nl = ada_w.shape[0]
    n_out = ada_w.shape[2]
    tn = 1536
    return pl.pallas_call(
        _ada_kernel,
        grid=(nl, n_out // tn),
        in_specs=[pl.BlockSpec((N_MOD_SEQ, D_MODEL), lambda l, n: (0, 0)),
                  pl.BlockSpec((None, D_MODEL, tn), lambda l, n: (l, 0, n)),
                  pl.BlockSpec((None, 1, tn), lambda l, n: (l, 0, n))],
        out_specs=pl.BlockSpec((None, N_MOD_SEQ, tn), lambda l, n: (l, 0, n)),
        out_shape=jax.ShapeDtypeStruct((nl, N_MOD_SEQ, n_out), F32),
        compiler_params=_params(("arbitrary", "arbitrary")),
        name="ada_mod",
    )(cvec, ada_w, ada_b.reshape(nl, 1, n_out))


def _modmm_kernel(x_ref, mod_ref, w_ref, o_ref, wbf_ref, *, shift_row, scale_row):
    @pl.when(pl.program_id(1) == 0)
    def _():
        wbf_ref[...] = w_ref[...].astype(BF16)

    mod = mod_ref[...]
    h = x_ref[...] * (1.0 + mod[scale_row:scale_row + 1, :]) + mod[shift_row:shift_row + 1, :]
    o_ref[...] = jnp.dot(h.astype(BF16), wbf_ref[...], preferred_element_type=F32)


def _modmm(x, mod, w, n_out, tn, tp, ls, shift_row, scale_row, name):
    t = x.shape[0]
    tm = 512
    return pl.pallas_call(
        functools.partial(_modmm_kernel, shift_row=shift_row, scale_row=scale_row),
        grid=(n_out // tn, t // tm),
        in_specs=[pl.BlockSpec((tm, D_MODEL), lambda n, m: (m, 0)),
                  pl.BlockSpec((None, MOD_ROWS, D_MODEL), lambda n, m: (_seq_of_tile(m, tm, tp, ls), 0, 0)),
                  pl.BlockSpec((D_MODEL, tn), lambda n, m: (0, n))],
        out_specs=pl.BlockSpec((tm, tn), lambda n, m: (m, n)),
        out_shape=jax.ShapeDtypeStruct((t, n_out), F32),
        scratch_shapes=[pltpu.VMEM((D_MODEL, tn), BF16)],
        compiler_params=_params(("arbitrary", "arbitrary")),
        name=name,
    )(x, mod, w)


def _conv_kernel(cur_ref, prev_ref, next_ref, w_ref, b_ref, o_ref, ext_ref, *, tm, tp, lp, ls):
    m = pl.program_id(0)
    npt = tp // tm
    pos = jnp.where(m < npt, m % (lp // tm), (m - npt) % (ls // tm))
    last_pos = jnp.where(m < npt, lp // tm - 1, ls // tm - 1)
    halo = SUBLANES
    ext_ref[0:halo, :] = jnp.where(pos == 0, 0.0, prev_ref[...])
    ext_ref[halo:halo + tm, :] = cur_ref[...]
    ext_ref[halo + tm:2 * halo + tm, :] = jnp.where(pos == last_pos, 0.0, next_ref[...])
    acc = jnp.zeros(o_ref.shape, F32) + b_ref[...]
    for j in range(CONV_W):
        acc = acc + ext_ref[pl.ds(halo - CONV_W // 2 + j, tm), :] * w_ref[j:j + 1, :]
    o_ref[...] = _silu(acc)


def _conv_silu(a, col0, width, w, b, tp, lp, ls, name):
    t = a.shape[0]
    tm, tc = TM, 512
    cb0 = col0 // tc
    nrow8 = t // SUBLANES
    per = tm // SUBLANES
    return pl.pallas_call(
        functools.partial(_conv_kernel, tm=tm, tp=tp, lp=lp, ls=ls),
        grid=(t // tm, width // tc),
        in_specs=[pl.BlockSpec((tm, tc), lambda m, j: (m, cb0 + j)),
                  pl.BlockSpec((SUBLANES, tc), lambda m, j: (jnp.maximum(m * per - 1, 0), cb0 + j)),
                  pl.BlockSpec((SUBLANES, tc), lambda m, j: (jnp.minimum((m + 1) * per, nrow8 - 1), cb0 + j)),
                  pl.BlockSpec((CONV_W, tc), lambda m, j: (0, j)),
                  pl.BlockSpec((1, tc), lambda m, j: (0, j))],
        out_specs=pl.BlockSpec((tm, tc), lambda m, j: (m, j)),
        out_shape=jax.ShapeDtypeStruct((t, width), F32),
        scratch_shapes=[pltpu.VMEM((tm + 2 * SUBLANES, tc), F32)],
        compiler_params=_params(("arbitrary", "arbitrary")),
        name=name,
    )(a, a, a, w, b.reshape(1, width))


def _qkv_kernel(xc_ref, xu_ref, wqk_ref, wv_ref, wg_ref, bg_ref, q_ref, k_ref, v_ref, g_ref):
    nblk = E_A // LANES
    g = jnp.zeros(g_ref.shape, F32) + bg_ref[...]
    for b in range(nblk):
        sl = slice(b * LANES, (b + 1) * LANES)
        xcb = xc_ref[:, sl].astype(BF16)
        xub = xu_ref[:, sl].astype(BF16)
        qk = jnp.dot(xcb, wqk_ref[b], preferred_element_type=F32)
        vv = jnp.dot(xub, wv_ref[b], preferred_element_type=F32)
        qb = qk[:, :LANES].astype(BF16)
        kb = qk[:, LANES:].astype(BF16)
        vb = vv.astype(BF16)
        q_ref[:, sl] = qb
        k_ref[:, sl] = kb
        v_ref[:, sl] = vb
        g = g + jnp.dot(qb, wg_ref[0, sl, :], preferred_element_type=F32)
        g = g + jnp.dot(kb, wg_ref[1, sl, :], preferred_element_type=F32)
        g = g + jnp.dot(vb, wg_ref[2, sl, :], preferred_element_type=F32)
    lane = lax.broadcasted_iota(jnp.int32, g.shape, 1)
    is_forget = (lane & (2 * H_A - 1)) >= H_A
    log_sig = jnp.minimum(g, 0.0) - jnp.log1p(jnp.exp(-jnp.abs(g)))
    g_ref[...] = jnp.where(is_forget, log_sig, g)


def _qkv_gates(xc, proj, wqk, wv, wg, bg):
    t = xc.shape[0]
    nblk = E_A // LANES
    tile = pl.BlockSpec((TM, E_A), lambda m: (m, 0))
    return pl.pallas_call(
        _qkv_kernel,
        grid=(t // TM,),
        in_specs=[tile, tile,
                  pl.BlockSpec((nblk, LANES, 2 * LANES), lambda m: (0, 0, 0)),
                  pl.BlockSpec((nblk, LANES, LANES), lambda m: (0, 0, 0)),
                  pl.BlockSpec((3, E_A, LANES), lambda m: (0, 0, 0)),
                  pl.BlockSpec((1, LANES), lambda m: (0, 0))],
        out_specs=[tile, tile, tile, pl.BlockSpec((TM, LANES), lambda m: (m, 0))],
        out_shape=[jax.ShapeDtypeStruct((t, E_A), BF16)] * 3 + [jax.ShapeDtypeStruct((t, LANES), F32)],
        compiler_params=_params(("arbitrary",)),
        name="mlstm_qkv_gates",
    )(xc, proj, wqk, wv, wg, bg)


def _order_masks(d, n):
    s_io = lax.broadcasted_iota(jnp.int32, (n, n), 0)
    r_io = lax.broadcasted_iota(jnp.int32, (n, n), 1)
    diff = (s_io - r_io) * (1 - 2 * d)
    return diff >= 0, diff <= 0


def _pick_col(blk, idx):
    lane = lax.broadcasted_iota(jnp.int32, (1, blk.shape[1]), 1)
    return jnp.sum(jnp.where(lane == idx, blk, 0.0), axis=1, keepdims=True)


def _mlstm_kernel(*refs, nc, has_init, emit_state):
    q_ref, k_ref, v_ref, gt_ref, g_ref = refs[:5]
    pos = 5
    if has_init:
        c0_ref, n0_ref, m0_ref = refs[pos:pos + 3]
        pos += 3
    hs_ref = refs[pos]
    pos += 1
    if emit_state:
        cn_ref, nn_ref, mn_ref = refs[pos:pos + 3]
        pos += 3
    c_scr, n_scr, m_scr = refs[pos:pos + 3]

    lc = SCAN_CHUNK
    s_id, h, d, c = (pl.program_id(i) for i in range(4))
    ceff = jnp.where(d == 0, c, nc - 1 - c)
    use_state = has_init or nc > 1

    @pl.when(c == 0)
    def _():
        if has_init:
            c_scr[...] = c0_ref[...]
            n_scr[...] = n0_ref[pl.ds(h, 1), :]
            m_scr[...] = jnp.zeros(m_scr.shape, F32) + m0_ref[(s_id * 2 + d) * H_A + h]
        else:
            c_scr[...] = jnp.zeros(c_scr.shape, F32)
            n_scr[...] = jnp.zeros(n_scr.shape, F32)
            m_scr[...] = jnp.zeros(m_scr.shape, F32)

    scale = DH_A ** -0.5
    row_i = d * (2 * H_A) + h
    row_f = row_i + H_A
    i_row = gt_ref[pl.ds(row_i, 1), :]
    f_row = gt_ref[pl.ds(row_f, 1), :]
    gblk = g_ref[...]
    i_col = _pick_col(gblk, row_i)
    f_col = _pick_col(gblk, row_f)
    before, after = _order_masks(d, lc)

    b_col = jnp.sum(jnp.where(before, f_row, 0.0), axis=1, keepdims=True)
    b_row = jnp.sum(jnp.where(after, f_col, 0.0), axis=0, keepdims=True)
    logw = jnp.where(before, b_col - b_row + i_row, -jnp.inf)
    m_prev = m_scr[:, 0:1]
    inter = b_col + m_prev
    m_s = jnp.maximum(inter, jnp.max(logw, axis=1, keepdims=True))

    q = q_ref[...]
    k = k_ref[...]
    v = v_ref[...]
    s = lax.dot_general(q, k, (((1,), (1,)), ((), ())), preferred_element_type=F32) * scale
    p = s * jnp.exp(logw - m_s)
    num = jnp.dot(p.astype(BF16), v, preferred_element_type=F32)
    den = jnp.sum(p, axis=1, keepdims=True)
    if use_state:
        w_inter = jnp.exp(inter - m_s)
        qc = jnp.dot(q, c_scr[...].astype(BF16), preferred_element_type=F32) * scale
        qn = jnp.sum(q.astype(F32) * n_scr[...], axis=1, keepdims=True) * scale
        num = num + w_inter * qc
        den = den + w_inter * qn
    h_out = num / jnp.maximum(jnp.abs(den), jnp.exp(-m_s))

    rows = pl.ds(pl.multiple_of(ceff * lc, lc), lc)

    @pl.when(d == 0)
    def _():
        hs_ref[rows, :] = h_out

    @pl.when(d == 1)
    def _():
        hs_ref[rows, :] = hs_ref[rows, :] + h_out

    def update_state():
        b_last = jnp.sum(f_row, axis=1, keepdims=True)
        logk = b_last - b_col + i_col
        m_new = jnp.maximum(b_last + m_prev, jnp.max(logk, axis=0, keepdims=True))
        decay = jnp.exp(b_last + m_prev - m_new)
        kw = k.astype(F32) * jnp.exp(logk - m_new)
        c_new = jnp.dot(kw.T.astype(BF16), v, preferred_element_type=F32)
        n_new = jnp.sum(kw, axis=0, keepdims=True)
        if use_state:
            c_new = c_new + decay * c_scr[...]
            n_new = n_new + decay * n_scr[...]
        m_new_row = jnp.zeros(m_scr.shape, F32) + m_new
        if nc > 1:
            c_scr[...] = c_new
            n_scr[...] = n_new
            m_scr[...] = m_new_row
        if emit_state:
            @pl.when(c == nc - 1)
            def _():
                cn_ref[...] = c_new
                nn_ref[...] = n_new
                mn_ref[...] = m_new_row

    if emit_state:
        update_state()
    elif nc > 1:
        pl.when(c < nc - 1)(update_state)


def _mlstm_scan(q, k, v, gt, g, row0, nseq, seqlen, init=None, emit_state=False):
    lc = SCAN_CHUNK
    nc = seqlen // lc
    blk0 = row0 // lc
    has_init = init is not None

    def rowblk(s, h, d, c):
        return blk0 + s * nc + jnp.where(d == 0, c, nc - 1 - c)

    in_specs = [pl.BlockSpec((lc, DH_A), lambda s, h, d, c: (rowblk(s, h, d, c), h))] * 3
    in_specs += [pl.BlockSpec((2 * SUBLANES, lc), lambda s, h, d, c: (0, rowblk(s, h, d, c))),
                 pl.BlockSpec((lc, LANES), lambda s, h, d, c: (rowblk(s, h, d, c), 0))]
    args = [q, k, v, gt, g]
    if has_init:
        c0, n0, m0 = init
        in_specs += [pl.BlockSpec((None, None, None, DH_A, DH_A), lambda s, h, d, c: (s, d, h, 0, 0)),
                     pl.BlockSpec((None, None, H_A, DH_A), lambda s, h, d, c: (s, d, 0, 0)),
                     pl.BlockSpec(memory_space=pltpu.SMEM)]
        args += [c0, n0, m0.reshape(-1)]
    out_specs = [pl.BlockSpec((seqlen, DH_A), lambda s, h, d, c: (s, h))]
    out_shape = [jax.ShapeDtypeStruct((nseq * seqlen, E_A), F32)]
    if emit_state:
        out_specs += [pl.BlockSpec((None, None, None, DH_A, DH_A), lambda s, h, d, c: (s, d, h, 0, 0)),
                      pl.BlockSpec((None, None, None, 1, DH_A), lambda s, h, d, c: (s, d, h, 0, 0)),
                      pl.BlockSpec((None, None, None, 1, LANES), lambda s, h, d, c: (s, d, h, 0, 0))]
        out_shape += [jax.ShapeDtypeStruct((nseq, 2, H_A, DH_A, DH_A), F32),
                      jax.ShapeDtypeStruct((nseq, 2, H_A, 1, DH_A), F32),
                      jax.ShapeDtypeStruct((nseq, 2, H_A, 1, LANES), F32)]
    return pl.pallas_call(
        functools.partial(_mlstm_kernel, nc=nc, has_init=has_init, emit_state=emit_state),
        grid=(nseq, H_A, 2, nc),
        in_specs=in_specs,
        out_specs=out_specs,
        out_shape=out_shape,
        scratch_shapes=[pltpu.VMEM((DH_A, DH_A), F32), pltpu.VMEM((1, DH_A), F32), pltpu.VMEM((1, LANES), F32)],
        compiler_params=_params(("arbitrary",) * 4),
        name="mlstm_scan_state" if emit_state else "mlstm_scan_init",
    )(*args)


def _ssd_kernel(*refs, nc, has_init, emit_state):
    x_ref, b_ref, c_ref, dt_ref, dtt_ref, a_ref, bias_ref = refs[:7]
    pos = 7
    if has_init:
        h0_ref = refs[pos]
        pos += 1
    y_ref = refs[pos]
    pos += 1
    if emit_state:
        sn_ref = refs[pos]
        pos += 1
    h_scr, y_scr, xw_scr = refs[pos:pos + 3]

    lc = SCAN_CHUNK
    grp, d, c = pl.program_id(1), pl.program_id(2), pl.program_id(3)
    ceff = jnp.where(d == 0, c, nc - 1 - c)
    use_state = has_init or nc > 1
    hp = HG_B * P_B

    @pl.when(c == 0)
    def _():
        if has_init:
            h_scr[...] = h0_ref[...].reshape(hp, D_STATE)
        else:
            h_scr[...] = jnp.zeros(h_scr.shape, F32)

    before, after = _order_masks(d, lc)
    bb = b_ref[...].astype(BF16)
    cb16 = c_ref[...].astype(BF16)
    cb = lax.dot_general(cb16, bb, (((1,), (1,)), ((), ())), preferred_element_type=F32)
    if use_state:
        inter = lax.dot_general(cb16, h_scr[...].astype(BF16), (((1,), (1,)), ((), ())),
                                preferred_element_type=F32)
    base = d * H_B + grp * HG_B
    dtblk = dt_ref[...]
    dtt = dtt_ref[pl.ds(pl.multiple_of(base, HG_B), HG_B), :]
    decs = []
    for hg in range(HG_B):
        a_h = a_ref[base + hg]
        bias = bias_ref[base + hg]
        dt_col = _softplus(_pick_col(dtblk, base + hg) + bias)
        dt_row = _softplus(dtt[hg:hg + 1, :] + bias)
        a_col = dt_col * a_h
        a_row = dt_row * a_h
        acs_col = jnp.sum(jnp.where(before, a_row, 0.0), axis=1, keepdims=True)
        acs_row = jnp.sum(jnp.where(after, a_col, 0.0), axis=0, keepdims=True)
        decay = jnp.exp(jnp.where(before, acs_col - acs_row, -jnp.inf))
        w = cb * decay * dt_row
        sl = slice(hg * P_B, (hg + 1) * P_B)
        xh = x_ref[:, sl]
        yh = jnp.dot(w.astype(BF16), xh.astype(BF16), preferred_element_type=F32)
        if use_state:
            yh = yh + inter[:, sl] * jnp.exp(acs_col)
        y_scr[:, sl] = yh
        a_last = jnp.sum(a_row, axis=1, keepdims=True)
        xw_scr[:, sl] = xh * (jnp.exp(a_last - acs_col) * dt_col)
        decs.append(jnp.exp(a_last))

    rows = pl.ds(pl.multiple_of(ceff * lc, lc), lc)

    @pl.when(d == 0)
    def _():
        y_ref[rows, :] = y_scr[...]

    @pl.when(d == 1)
    def _():
        y_ref[rows, :] = y_ref[rows, :] + y_scr[...]

    def update_state():
        upd = jnp.dot(xw_scr[...].T.astype(BF16), bb, preferred_element_type=F32)
        for hg in range(HG_B):
            sl = slice(hg * P_B, (hg + 1) * P_B)
            new = upd[sl, :]
            if use_state:
                new = new + decs[hg] * h_scr[sl, :]
            h_scr[sl, :] = new
        if emit_state:
            @pl.when(c == nc - 1)
            def _():
                sn_ref[...] = h_scr[...].reshape(HG_B, P_B, D_STATE)

    if emit_state:
        update_state()
    elif nc > 1:
        pl.when(c < nc - 1)(update_state)


def _ssd_scan(xbc, dt, dtt, a_neg, dt_bias, row0, nseq, seqlen, init=None, emit_state=False):
    lc = SCAN_CHUNK
    nc = seqlen // lc
    blk0 = row0 // lc
    has_init = init is not None
    hp = HG_B * P_B
    b_off = E_B // D_STATE
    c_off = (E_B + GN_B) // D_STATE

    def rowblk(s, g, d, c):
        return blk0 + s * nc + jnp.where(d == 0, c, nc - 1 - c)

    in_specs = [pl.BlockSpec((lc, hp), lambda s, g, d, c: (rowblk(s, g, d, c), g)),
                pl.BlockSpec((lc, D_STATE), lambda s, g, d, c: (rowblk(s, g, d, c), b_off + g)),
                pl.BlockSpec((lc, D_STATE), lambda s, g, d, c: (rowblk(s, g, d, c), c_off + g)),
                pl.BlockSpec((lc, 2 * H_B), lambda s, g, d, c: (rowblk(s, g, d, c), 0)),
                pl.BlockSpec((2 * H_B, lc), lambda s, g, d, c: (0, rowblk(s, g, d, c))),
                pl.BlockSpec(memory_space=pltpu.SMEM),
                pl.BlockSpec(memory_space=pltpu.SMEM)]
    args = [xbc, xbc, xbc, dt, dtt, a_neg, dt_bias]
    state_spec = pl.BlockSpec((None, None, HG_B, P_B, D_STATE), lambda s, g, d, c: (s, d, g, 0, 0))
    if has_init:
        in_specs.append(state_spec)
        args.append(init)
    out_specs = [pl.BlockSpec((seqlen, hp), lambda s, g, d, c: (s, g))]
    out_shape = [jax.ShapeDtypeStruct((nseq * seqlen, E_B), F32)]
    if emit_state:
        out_specs.append(state_spec)
        out_shape.append(jax.ShapeDtypeStruct((nseq, 2, H_B, P_B, D_STATE), F32))
    return pl.pallas_call(
        functools.partial(_ssd_kernel, nc=nc, has_init=has_init, emit_state=emit_state),
        grid=(nseq, G_B, 2, nc),
        in_specs=in_specs,
        out_specs=out_specs,
        out_shape=out_shape,
        scratch_shapes=[pltpu.VMEM((hp, D_STATE), F32), pltpu.VMEM((lc, hp), F32), pltpu.VMEM((lc, hp), F32)],
        compiler_params=_params(("arbitrary",) * 4),
        name="ssd_scan_state" if emit_state else "ssd_scan_init",
    )(*args)


def _mixer_out_kernel(yp_ref, ys_ref, u_ref, z_ref, x_ref, mod_ref, nw_ref, sk_ref, w_ref, lnw_ref, lnb_ref,
                      o_ref, *, npt, grouped):
    m = pl.program_id(0)
    y = jnp.where(m < npt, yp_ref[...], ys_ref[...])
    u = u_ref[...]
    gate = _silu(z_ref[...])
    if grouped:
        parts = []
        for hd in range(H_A):
            seg = y[:, hd * DH_A:(hd + 1) * DH_A]
            mu = jnp.mean(seg, axis=-1, keepdims=True)
            sc = seg - mu
            var = jnp.mean(sc * sc, axis=-1, keepdims=True)
            parts.append(sc * lax.rsqrt(var + EPS))
        hn = jnp.concatenate(parts, axis=1) * nw_ref[...]
        t = (hn + sk_ref[...] * u) * gate
    else:
        t0 = (y + sk_ref[...] * u) * gate
        t = t0 * lax.rsqrt(jnp.mean(t0 * t0, axis=-1, keepdims=True) + EPS) * nw_ref[...]
    out = jnp.dot(t.astype(BF16), w_ref[...], preferred_element_type=F32)
    mod = mod_ref[...]
    r = ALPHA * x_ref[...] + mod[2:3, :] * out
    o_ref[...] = _layer_norm_rows(r, lnw_ref[...], lnb_ref[...])


def _mixer_out(yp, ys, u, proj, zblk, x, mod, nw, sk, w_bf, lnw, lnb, tp, ls, grouped, name):
    t = x.shape[0]
    e = w_bf.shape[0]
    npt = tp // TM
    nst = ys.shape[0] // TM
    vec_e = pl.BlockSpec((1, e), lambda m: (0, 0))
    vec_d = pl.BlockSpec((1, D_MODEL), lambda m: (0, 0))
    return pl.pallas_call(
        functools.partial(_mixer_out_kernel, npt=npt, grouped=grouped),
        grid=(t // TM,),
        in_specs=[pl.BlockSpec((TM, e), lambda m: (jnp.minimum(m, npt - 1), 0)),
                  pl.BlockSpec((TM, e), lambda m: (jnp.clip(m - npt, 0, nst - 1), 0)),
                  pl.BlockSpec((TM, e), lambda m: (m, 0)),
                  pl.BlockSpec((TM, e), lambda m: (m, zblk)),
                  pl.BlockSpec((TM, D_MODEL), lambda m: (m, 0)),
                  pl.BlockSpec((None, MOD_ROWS, D_MODEL), lambda m: (_seq_of_tile(m, TM, tp, ls), 0, 0)),
                  vec_e, vec_e,
                  pl.BlockSpec((e, D_MODEL), lambda m: (0, 0)),
                  vec_d, vec_d],
        out_specs=pl.BlockSpec((TM, D_MODEL), lambda m: (m, 0)),
        out_shape=jax.ShapeDtypeStruct((t, D_MODEL), F32),
        compiler_params=_params(("arbitrary",)),
        name=name,
    )(yp, ys, u, proj, x, mod, nw.reshape(1, e), sk.reshape(1, e), w_bf, lnw.reshape(1, D_MODEL),
      lnb.reshape(1, D_MODEL))


def _router_kernel(x_ref, mod_ref, w_ref, b_ref, hm_ref, comb_ref):
    mod = mod_ref[...]
    hm = (x_ref[...] * (1.0 + mod[4:5, :]) + mod[3:4, :]).astype(BF16)
    hm_ref[...] = hm
    logits = jnp.dot(hm, w_ref[...], preferred_element_type=F32) + b_ref[...]
    lane = lax.broadcasted_iota(jnp.int32, logits.shape, 1).astype(F32)
    big = float(2 * LANES)
    glog = jnp.where(lane < MOE_GROUPS, logits, -jnp.inf)
    ge = jnp.exp(glog - jnp.max(glog, axis=1, keepdims=True))
    prob = ge / jnp.sum(ge, axis=1, keepdims=True)
    gp = jnp.max(prob, axis=1, keepdims=True)
    gi = jnp.min(jnp.where(prob == gp, lane, big), axis=1, keepdims=True)
    lo = ROUTER_OFF + gi * MOE_EXPERTS
    ev = jnp.where((lane >= lo) & (lane < lo + MOE_EXPERTS), logits, -jnp.inf)
    e1 = jnp.max(ev, axis=1, keepdims=True)
    i1 = jnp.min(jnp.where(ev == e1, lane, big), axis=1, keepdims=True)
    ev2 = jnp.where(lane == i1, -jnp.inf, ev)
    e2 = jnp.max(ev2, axis=1, keepdims=True)
    i2 = jnp.min(jnp.where(ev2 == e2, lane, big), axis=1, keepdims=True)
    t2 = jnp.exp(e2 - e1)
    w1 = 1.0 / (1.0 + t2)
    w2 = t2 / (1.0 + t2)
    comb_ref[...] = jnp.where(lane == i1, w1 * gp, jnp.where(lane == i2, w2 * gp, 0.0))


def _router(x, mod, w_r, b_r, tp, ls):
    t = x.shape[0]
    tm = TM_MOE
    return pl.pallas_call(
        _router_kernel,
        grid=(t // tm,),
        in_specs=[pl.BlockSpec((tm, D_MODEL), lambda m: (m, 0)),
                  pl.BlockSpec((None, MOD_ROWS, D_MODEL), lambda m: (_seq_of_tile(m, tm, tp, ls), 0, 0)),
                  pl.BlockSpec((D_MODEL, LANES), lambda m: (0, 0)),
                  pl.BlockSpec((1, LANES), lambda m: (0, 0))],
        out_specs=[pl.BlockSpec((tm, D_MODEL), lambda m: (m, 0)),
                   pl.BlockSpec((tm, LANES), lambda m: (m, 0))],
        out_shape=[jax.ShapeDtypeStruct((t, D_MODEL), BF16), jax.ShapeDtypeStruct((t, LANES), F32)],
        compiler_params=_params(("arbitrary",)),
        name="moe_router",
    )(x, mod, w_r, b_r)


def _moe_kernel(hm_ref, comb_ref, wg_ref, wu_ref, wd_ref, x_ref, mod_ref, lnw_ref, lnb_ref, o_ref, acc_ref):
    e = pl.program_id(1)

    @pl.when(e == 0)
    def _():
        acc_ref[...] = jnp.zeros(acc_ref.shape, F32)

    hm = hm_ref[...]
    a = _silu(jnp.dot(hm, wg_ref[...], preferred_element_type=F32)) * jnp.dot(
        hm, wu_ref[...], preferred_element_type=F32)
    ce = _pick_col(comb_ref[...], ROUTER_OFF + e)
    acc_ref[...] += jnp.dot((a * ce).astype(BF16), wd_ref[...], preferred_element_type=F32)

    @pl.when(e == N_EXPERTS - 1)
    def _():
        mod = mod_ref[...]
        r = ALPHA * x_ref[...] + mod[5:6, :] * acc_ref[...]
        o_ref[...] = _layer_norm_rows(r, lnw_ref[...], lnb_ref[...])


def _moe(hm, comb, wg, wu, wd, x, mod, lnw, lnb, tp, ls):
    t = x.shape[0]
    tm = TM_MOE
    vec_d = pl.BlockSpec((1, D_MODEL), lambda m, e: (0, 0))
    return pl.pallas_call(
        _moe_kernel,
        grid=(t // tm, N_EXPERTS),
        in_specs=[pl.BlockSpec((tm, D_MODEL), lambda m, e: (m, 0)),
                  pl.BlockSpec((tm, LANES), lambda m, e: (m, 0)),
                  pl.BlockSpec((None, None, D_MODEL, D_FF_E), lambda m, e: (e // MOE_EXPERTS, e % MOE_EXPERTS, 0, 0)),
                  pl.BlockSpec((None, None, D_MODEL, D_FF_E), lambda m, e: (e // MOE_EXPERTS, e % MOE_EXPERTS, 0, 0)),
                  pl.BlockSpec((None, None, D_FF_E, D_MODEL), lambda m, e: (e // MOE_EXPERTS, e % MOE_EXPERTS, 0, 0)),
                  pl.BlockSpec((tm, D_MODEL), lambda m, e: (m, 0)),
                  pl.BlockSpec((None, MOD_ROWS, D_MODEL), lambda m, e: (_seq_of_tile(m, tm, tp, ls), 0, 0)),
                  vec_d, vec_d],
        out_specs=pl.BlockSpec((tm, D_MODEL), lambda m, e: (m, 0)),
        out_shape=jax.ShapeDtypeStruct((t, D_MODEL), F32),
        scratch_shapes=[pltpu.VMEM((tm, D_MODEL), F32)],
        compiler_params=_params(("arbitrary", "arbitrary")),
        name="moe_experts",
    )(hm, comb, wg, wu, wd, x, mod, lnw.reshape(1, D_MODEL), lnb.reshape(1, D_MODEL))


def _moe_layer(x, mod, w_rg, b_rg, w_re, b_re, w_gate, w_up, w_down, lnw, lnb, tp, ls):
    w_r = jnp.concatenate([w_rg, jnp.moveaxis(w_re, 0, 1).reshape(D_MODEL, N_EXPERTS)], axis=1)
    w_r = jnp.pad(w_r, ((0, 0), (0, LANES - w_r.shape[1]))).astype(BF16)
    b_r = jnp.pad(jnp.concatenate([b_rg, b_re.reshape(-1)]), (0, LANES - MOE_GROUPS - N_EXPERTS)).reshape(1, LANES)
    hm, comb = _router(x, mod, w_r, b_r, tp, ls)
    return _moe(hm, comb, w_gate.astype(BF16), w_up.astype(BF16), w_down.astype(BF16), x, mod, lnw, lnb, tp, ls)


def _blockdiag_lanes(w):
    per = LANES // QKV_BLOCK
    wb = w.reshape(-1, per, QKV_BLOCK, QKV_BLOCK)
    eye = jnp.eye(per, dtype=w.dtype)
    return jnp.einsum('bncd,nm->bncmd', wb, eye).reshape(-1, LANES, LANES)


def _grid_to_cols(x):
    b, l, dd = x.shape
    return x.reshape(b, l // GRID_W, GRID_W, dd).transpose(0, 2, 1, 3).reshape(b, l, dd)


def _cols_to_grid(x):
    b, l, dd = x.shape
    return x.reshape(b, GRID_W, l // GRID_W, dd).transpose(0, 2, 1, 3).reshape(b, l, dd)


def kernel(x_prompt, x_sample, state_mlstm_C, state_mlstm_n, state_mlstm_m, state_ssd, c, c_ctx, ada_w, ada_b, ln_w, ln_b, a_w_in, a_conv_w, a_conv_b, a_w_q, a_w_k, a_w_v, a_w_gates, a_b_gates, a_norm_w, a_skip, a_w_down, b_w_in, b_conv_w, b_conv_b, b_dt_bias, b_a_log, b_d_skip, b_norm_w, b_w_out, moe_w_rg, moe_b_rg, moe_w_re, moe_b_re, moe_w_gate, moe_w_up, moe_w_down):
    bp, lp, _ = x_prompt.shape
    bs, ls, _ = x_sample.shape
    tp, ts = bp * lp, bs * ls
    assert lp % SCAN_CHUNK == 0 and ls % SCAN_CHUNK == 0 and tp % 512 == 0 and ls % 512 == 0
    assert bs + 1 <= N_MOD_SEQ

    cvec = jnp.concatenate([c_ctx[None, :], c, jnp.zeros((N_MOD_SEQ - 1 - bs, D_MODEL), F32)], axis=0)
    mods = _ada(cvec, ada_w, ada_b).reshape(DEPTH, N_MOD_SEQ, 6, D_MODEL)
    mods = jnp.pad(mods, ((0, 0), (0, 0), (0, MOD_ROWS - 6), (0, 0)))

    x = jnp.concatenate([x_prompt.reshape(tp, D_MODEL), x_sample.reshape(ts, D_MODEL)], axis=0)

    mod = mods[0]
    proj = _modmm(x, mod, a_w_in[0], 2 * E_A, 1024, tp, ls, 0, 1, "mlstm_in_proj")
    xc = _conv_silu(proj, 0, E_A, a_conv_w[0], a_conv_b[0], tp, lp, ls, "mlstm_conv")
    wqk = jnp.concatenate([_blockdiag_lanes(a_w_q[0]), _blockdiag_lanes(a_w_k[0])], axis=-1).astype(BF16)
    wv = _blockdiag_lanes(a_w_v[0]).astype(BF16)
    ngate = 4 * H_A
    wg = jnp.pad(a_w_gates[0].reshape(3, E_A, ngate), ((0, 0), (0, 0), (0, LANES - ngate))).astype(BF16)
    bg = jnp.pad(a_b_gates[0], (0, LANES - ngate)).reshape(1, LANES)
    q, k, v, g = _qkv_gates(xc, proj, wqk, wv, wg, bg)
    gt = g[:, :ngate].T
    hs_p, c_new, n_new, m_new = _mlstm_scan(q, k, v, gt, g, 0, bp, lp, emit_state=True)
    (hs_s,) = _mlstm_scan(q, k, v, gt, g, tp, bs, ls,
                          init=(state_mlstm_C[:, 0], state_mlstm_n[:, 0], state_mlstm_m[:, 0]))
    x = _mixer_out(hs_p, hs_s, xc, proj, 1, x, mod, a_norm_w[0], a_skip[0], a_w_down[0].astype(BF16),
                   ln_w[0, 0], ln_b[0, 0], tp, ls, True, "mlstm_out")
    x = _moe_layer(x, mod, moe_w_rg[0], moe_b_rg[0], moe_w_re[0], moe_b_re[0], moe_w_gate[0], moe_w_up[0],
                   moe_w_down[0], ln_w[0, 1], ln_b[0, 1], tp, ls)

    mod = mods[1]
    x = jnp.concatenate([x[:tp], _grid_to_cols(x[tp:].reshape(bs, ls, D_MODEL)).reshape(ts, D_MODEL)], axis=0)
    n_main = 2 * E_B + 2 * GN_B
    proj = _modmm(x, mod, b_w_in[0], n_main, 1024, tp, ls, 0, 1, "ssd_in_proj")
    dt = _modmm(x, mod, b_w_in[0][:, n_main:], 2 * H_B, 2 * H_B, tp, ls, 0, 1, "ssd_dt_proj")
    xbc = _conv_silu(proj, E_B, E_B + 2 * GN_B, b_conv_w[0], b_conv_b[0], tp, lp, ls, "ssd_conv")
    a_neg = -jnp.exp(b_a_log[0].reshape(-1))
    dt_bias = b_dt_bias[0].reshape(-1)
    dtt = dt.T
    y_p, s_new = _ssd_scan(xbc, dt, dtt, a_neg, dt_bias, 0, bp, lp, emit_state=True)
    (y_s,) = _ssd_scan(xbc, dt, dtt, a_neg, dt_bias, tp, bs, ls, init=state_ssd[:, 0])
    x = _mixer_out(y_p, y_s, xbc, proj, 0, x, mod, b_norm_w[0], jnp.repeat(b_d_skip[0], P_B),
                   b_w_out[0].astype(BF16), ln_w[1, 0], ln_b[1, 0], tp, ls, False, "ssd_out")
    x = _moe_layer(x, mod, moe_w_rg[1], moe_b_rg[1], moe_w_re[1], moe_b_re[1], moe_w_gate[1], moe_w_up[1],
                   moe_w_down[1], ln_w[1, 1], ln_b[1, 1], tp, ls)

    y_prompt = x[:tp].reshape(bp, lp, D_MODEL)
    y_sample = _cols_to_grid(x[tp:].reshape(bs, ls, D_MODEL))
    return (y_prompt, y_sample,
            c_new[:, None],
            n_new.reshape(bp, 1, 2, H_A, DH_A),
            m_new[:, :, :, 0, 0][:, None],
            s_new[:, None])
```

```python
import functools
import math

import jax
import jax.numpy as jnp
from jax import lax
from jax.experimental import pallas as pl
from jax.experimental.pallas import tpu as pltpu

F32 = jnp.float32
BF16 = jnp.bfloat16

D_MODEL = 1024
DEPTH = 2
GRID_W = 64
CONV_W = 5
E_A = 2 * D_MODEL
H_A = 4
DH_A = E_A // H_A
QKV_BLOCK = 4
E_B = 2 * D_MODEL
P_B = 64
H_B = E_B // P_B
G_B = 4
HG_B = H_B // G_B
D_STATE = 128
GN_B = G_B * D_STATE
MOE_GROUPS = 4
MOE_EXPERTS = 4
N_EXPERTS = MOE_GROUPS * MOE_EXPERTS
D_FF_E = 256
ALPHA = (2.0 * DEPTH) ** 0.25
EPS = 1e-5

LANES = 128
SUBLANES = 8
TM = 256
TM_MOE = 512
SCAN_CHUNK = 256
MOD_ROWS = 8
N_MOD_SEQ = 16
VMEM_LIMIT = 48 * 1024 * 1024
ROUTER_OFF = MOE_GROUPS


def _params(sem):
    return pltpu.CompilerParams(dimension_semantics=sem, vmem_limit_bytes=VMEM_LIMIT)


def _seq_of_tile(m, tm, tp, ls):
    npt = tp // tm
    return jnp.where(m < npt, 0, 1 + (m - npt) // (ls // tm))


def _silu(x):
    return x * jax.nn.sigmoid(x)


def _softplus(x):
    return jnp.maximum(x, 0.0) + jnp.log1p(jnp.exp(-jnp.abs(x)))


def _layer_norm_rows(r, w, b):
    mu = jnp.mean(r, axis=-1, keepdims=True)
    rc = r - mu
    var = jnp.mean(rc * rc, axis=-1, keepdims=True)
    return rc * lax.rsqrt(var + EPS) * w + b


def _ada_kernel(c_ref, w_ref, b_ref, o_ref):
    s = _silu(c_ref[...])
    o_ref[...] = jnp.dot(s.astype(BF16), w_ref[...].astype(BF16), preferred_element_type=F32) + b_ref[...]


def _ada(cvec, ada_w, ada_b):
    nl = ada_w.shape[0]
    n_out = ada_w.shape[2]
    tn = 1536
    return pl.pallas_call(
        _ada_kernel,
        grid=(nl, n_out // tn),
        in_specs=[pl.BlockSpec((N_MOD_SEQ, D_MODEL), lambda l, n: (0, 0)),
                  pl.BlockSpec((None, D_MODEL, tn), lambda l, n: (l, 0, n)),
                  pl.BlockSpec((None, 1, tn), lambda l, n: (l, 0, n))],
        out_specs=pl.BlockSpec((None, N_MOD_SEQ, tn), lambda l, n: (l, 0, n)),
        out_shape=jax.ShapeDtypeStruct((nl, N_MOD_SEQ, n_out), F32),
        compiler_params=_params(("arbitrary", "arbitrary")),
        name="ada_mod",
    )(cvec, ada_w, ada_b.reshape(nl, 1, n_out))


def _modmm_kernel(x_ref, mod_ref, w_ref, o_ref, wbf_ref, *, shift_row, scale_row):
    @pl.when(pl.program_id(1) == 0)
    def _():
        wbf_ref[...] = w_ref[...].astype(BF16)

    mod = mod_ref[...]
    h = x_ref[...] * (1.0 + mod[scale_row:scale_row + 1, :]) + mod[shift_row:shift_row + 1, :]
    o_ref[...] = jnp.dot(h.astype(BF16), wbf_ref[...], preferred_element_type=F32)


def _modmm(x, mod, w, n_out, tn, tp, ls, shift_row, scale_row, name):
    t = x.shape[0]
    tm = 512
    return pl.pallas_call(
        functools.partial(_modmm_kernel, shift_row=shift_row, scale_row=scale_row),
        grid=(n_out // tn, t // tm),
        in_specs=[pl.BlockSpec((tm, D_MODEL), lambda n, m: (m, 0)),
                  pl.BlockSpec((None, MOD_ROWS, D_MODEL), lambda n, m: (_seq_of_tile(m, tm, tp, ls), 0, 0)),
                  pl.BlockSpec((D_MODEL, tn), lambda n, m: (0, n))],
        out_specs=pl.BlockSpec((tm, tn), lambda n, m: (m, n)),
        out_shape=jax.ShapeDtypeStruct((t, n_out), F32),
        scratch_shapes=[pltpu.VMEM((D_MODEL, tn), BF16)],
        compiler_params=_params(("arbitrary", "arbitrary")),
        name=name,
    )(x, mod, w)


def _conv_kernel(cur_ref, prev_ref, next_ref, w_ref, b_ref, o_ref, ext_ref, *, tm, tp, lp, ls):
    m = pl.program_id(0)
    npt = tp // tm
    pos = jnp.where(m < npt, m % (lp // tm), (m - npt) % (ls // tm))
    last_pos = jnp.where(m < npt, lp // tm - 1, ls // tm - 1)
    halo = SUBLANES
    ext_ref[0:halo, :] = jnp.where(pos == 0, 0.0, prev_ref[...])
    ext_ref[halo:halo + tm, :] = cur_ref[...]
    ext_ref[halo + tm:2 * halo + tm, :] = jnp.where(pos == last_pos, 0.0, next_ref[...])
    acc = jnp.zeros(o_ref.shape, F32) + b_ref[...]
    for j in range(CONV_W):
        acc = acc + ext_ref[pl.ds(halo - CONV_W // 2 + j, tm), :] * w_ref[j:j + 1, :]
    o_ref[...] = _silu(acc)


def _conv_silu(a, col0, width, w, b, tp, lp, ls, name):
    t = a.shape[0]
    tm, tc = TM, 1024
    cb0 = col0 // tc
    nrow8 = t // SUBLANES
    per = tm // SUBLANES
    return pl.pallas_call(
        functools.partial(_conv_kernel, tm=tm, tp=tp, lp=lp, ls=ls),
        grid=(t // tm, width // tc),
        in_specs=[pl.BlockSpec((tm, tc), lambda m, j: (m, cb0 + j)),
                  pl.BlockSpec((SUBLANES, tc), lambda m, j: (jnp.maximum(m * per - 1, 0), cb0 + j)),
                  pl.BlockSpec((SUBLANES, tc), lambda m, j: (jnp.minimum((m + 1) * per, nrow8 - 1), cb0 + j)),
                  pl.BlockSpec((CONV_W, tc), lambda m, j: (0, j)),
                  pl.BlockSpec((1, tc), lambda m, j: (0, j))],
        out_specs=pl.BlockSpec((tm, tc), lambda m, j: (m, j)),
        out_shape=jax.ShapeDtypeStruct((t, width), F32),
        scratch_shapes=[pltpu.VMEM((tm + 2 * SUBLANES, tc), F32)],
        compiler_params=_params(("arbitrary", "arbitrary")),
        name=name,
    )(a, a, a, w, b.reshape(1, width))


def _qkv_kernel(xc_ref, xu_ref, wqk_ref, wv_ref, wg_ref, bg_ref, q_ref, k_ref, v_ref, g_ref):
    nblk = E_A // LANES
    g = jnp.zeros(g_ref.shape, F32) + bg_ref[...]
    for b in range(nblk):
        sl = slice(b * LANES, (b + 1) * LANES)
        xcb = xc_ref[:, sl].astype(BF16)
        xub = xu_ref[:, sl].astype(BF16)
        qk = jnp.dot(xcb, wqk_ref[b], preferred_element_type=F32)
        vv = jnp.dot(xub, wv_ref[b], preferred_element_type=F32)
        qb = qk[:, :LANES].astype(BF16)
        kb = qk[:, LANES:].astype(BF16)
        vb = vv.astype(BF16)
        q_ref[:, sl] = qb
        k_ref[:, sl] = kb
        v_ref[:, sl] = vb
        g = g + jnp.dot(qb, wg_ref[0, sl, :], preferred_element_type=F32)
        g = g + jnp.dot(kb, wg_ref[1, sl, :], preferred_element_type=F32)
        g = g + jnp.dot(vb, wg_ref[2, sl, :], preferred_element_type=F32)
    lane = lax.broadcasted_iota(jnp.int32, g.shape, 1)
    is_forget = (lane & (2 * H_A - 1)) >= H_A
    log_sig = jnp.minimum(g, 0.0) - jnp.log1p(jnp.exp(-jnp.abs(g)))
    g_ref[...] = jnp.where(is_forget, log_sig, g)


def _qkv_gates(xc, proj, wqk, wv, wg, bg):
    t = xc.shape[0]
    nblk = E_A // LANES
    tile = pl.BlockSpec((TM, E_A), lambda m: (m, 0))
    return pl.pallas_call(
        _qkv_kernel,
        grid=(t // TM,),
        in_specs=[tile, tile,
                  pl.BlockSpec((nblk, LANES, 2 * LANES), lambda m: (0, 0, 0)),
                  pl.BlockSpec((nblk, LANES, LANES), lambda m: (0, 0, 0)),
                  pl.BlockSpec((3, E_A, LANES), lambda m: (0, 0, 0)),
                  pl.BlockSpec((1, LANES), lambda m: (0, 0))],
        out_specs=[tile, tile, tile, pl.BlockSpec((TM, LANES), lambda m: (m, 0))],
        out_shape=[jax.ShapeDtypeStruct((t, E_A), BF16)] * 3 + [jax.ShapeDtypeStruct((t, LANES), F32)],
        compiler_params=_params(("arbitrary",)),
        name="mlstm_qkv_gates",
    )(xc, proj, wqk, wv, wg, bg)


def _order_mask(dirn, n, rows_are_later):
    a_io = lax.broadcasted_iota(jnp.int32, (n, n), 0)
    b_io = lax.broadcasted_iota(jnp.int32, (n, n), 1)
    if (dirn == 0) == rows_are_later:
        return b_io <= a_io
    return b_io >= a_io


def _exact_cumsum_rows(rows, tri):
    nrow = rows.shape[0]
    p1 = rows.astype(BF16).astype(F32)
    r1 = rows - p1
    p2 = r1.astype(BF16).astype(F32)
    p3 = r1 - p2
    pieces = jnp.concatenate([p1, p2, p3, jnp.zeros_like(p1)], axis=0).astype(BF16)
    cs = jnp.dot(pieces, tri, preferred_element_type=F32)
    return cs[0:nrow] + cs[nrow:2 * nrow] + cs[2 * nrow:3 * nrow]


def _pick_row(blk, idx):
    sub = lax.broadcasted_iota(jnp.int32, (blk.shape[0], 1), 0)
    return jnp.sum(jnp.where(sub == idx, blk, 0.0), axis=0, keepdims=True)


def _rows_to_cols(rows):
    pad = jnp.zeros((LANES - rows.shape[0], rows.shape[1]), F32)
    return jnp.concatenate([rows, pad], axis=0).T


def _pick_col(blk, idx):
    lane = lax.broadcasted_iota(jnp.int32, (1, blk.shape[1]), 1)
    return jnp.sum(jnp.where(lane == idx, blk, 0.0), axis=1, keepdims=True)


def _mlstm_direction(dirn, h, q, k, v, s_qk, kt, gt_ref, c_ref, n_ref, m_ref, use_state, need_update):
    lc = SCAN_CHUNK
    scale = DH_A ** -0.5
    log_scale = -0.5 * math.log(DH_A)
    gates = gt_ref[dirn * 2 * H_A:(dirn + 1) * 2 * H_A, :]
    tri = jnp.where(_order_mask(dirn, lc, False), 1.0, 0.0).astype(BF16)
    csum = _exact_cumsum_rows(gates, tri)
    i_row = _pick_row(gates, h)
    f_row = _pick_row(gates, H_A + h)
    b_row = _pick_row(csum, H_A + h)
    v_row = i_row - b_row
    valid = _order_mask(dirn, lc, True)
    m_prev = m_ref[:, 0:1]
    run_max = jnp.max(jnp.where(valid, v_row, -jnp.inf), axis=1, keepdims=True)
    u_col = -jnp.maximum(m_prev, run_max)
    p = s_qk * jnp.exp(jnp.where(valid, u_col + (v_row + log_scale), -jnp.inf))
    num = jnp.dot(p.astype(BF16), v, preferred_element_type=F32)
    den = jnp.sum(p, axis=1, keepdims=True)
    if use_state:
        w_inter = jnp.exp(m_prev + u_col) * scale
        qc = jnp.dot(q, c_ref[...].astype(BF16), preferred_element_type=F32)
        qn = jnp.sum(q.astype(F32) * n_ref[...], axis=1, keepdims=True)
        num = num + w_inter * qc
        den = den + w_inter * qn
    b_col = _rows_to_cols(b_row)[:, 0:1]
    h_out = num / jnp.maximum(jnp.abs(den), jnp.exp(u_col - b_col))
    if not need_update:
        return h_out, None
    b_last = jnp.sum(f_row, axis=1, keepdims=True)
    logk = b_last + v_row
    m_new = jnp.maximum(b_last + m_prev, jnp.max(logk, axis=1, keepdims=True))
    wk = jnp.exp(logk - m_new)
    c_new = jnp.dot((kt * wk).astype(BF16), v, preferred_element_type=F32)
    wk8 = (jnp.zeros((SUBLANES, lc), F32) + wk).astype(BF16)
    n_new = jnp.dot(wk8, k, preferred_element_type=F32)[0:1, :]
    return h_out, (c_new, n_new, m_new, jnp.exp(b_last + m_prev - m_new))


def _mlstm_kernel(*refs, nc, has_init, emit_state):
    shared = nc == 1
    pos = 4 if shared else 8
    chunk_refs = [refs[0:4], refs[0:4] if shared else refs[4:8]]
    if has_init:
        c0_ref, n0_ref, m0_ref = refs[pos:pos + 3]
        pos += 3
    hs_ref = refs[pos]
    pos += 1
    if emit_state:
        cn_ref, nn_ref, mn_ref = refs[pos:pos + 3]
        pos += 3
    c_scr, n_scr, m_scr = refs[pos:pos + 3]

    lc = SCAN_CHUNK
    s_id, h, c = (pl.program_id(i) for i in range(3))
    use_state = has_init or nc > 1

    @pl.when(c == 0)
    def _():
        for dirn in range(2):
            if has_init:
                c_scr[dirn] = c0_ref[dirn]
                n_scr[dirn] = _pick_row(n0_ref[dirn], h)
                m_scr[dirn] = jnp.zeros(m_scr.shape[1:], F32) + m0_ref[(s_id * 2 + dirn) * H_A + h]
            else:
                c_scr[dirn] = jnp.zeros(c_scr.shape[1:], F32)
                n_scr[dirn] = jnp.zeros(n_scr.shape[1:], F32)
                m_scr[dirn] = jnp.zeros(m_scr.shape[1:], F32)
        if not shared:
            hs_ref[...] = jnp.zeros(hs_ref.shape, F32)

    def chunk_operands(q_ref, k_ref, v_ref, gt_ref):
        q = q_ref[...]
        k = k_ref[...]
        s_qk = lax.dot_general(q, k, (((1,), (1,)), ((), ())), preferred_element_type=F32)
        return q, k, v_ref[...], s_qk, k.astype(F32).T, gt_ref

    ops = [chunk_operands(*chunk_refs[0])]
    ops.append(ops[0] if shared else chunk_operands(*chunk_refs[1]))

    def run(need_update):
        return [_mlstm_direction(dirn, h, *ops[dirn], c_scr.at[dirn], n_scr.at[dirn], m_scr.at[dirn],
                                 use_state, need_update) for dirn in range(2)]

    def write_out(res):
        if shared:
            hs_ref[...] = res[0][0] + res[1][0]
        else:
            rows_f = pl.ds(pl.multiple_of(c * lc, lc), lc)
            rows_b = pl.ds(pl.multiple_of((nc - 1 - c) * lc, lc), lc)
            hs_ref[rows_f, :] = hs_ref[rows_f, :] + res[0][0]
            hs_ref[rows_b, :] = hs_ref[rows_b, :] + res[1][0]

    def with_update():
        res = run(True)
        write_out(res)
        for dirn in range(2):
            c_new, n_new, m_new, decay = res[dirn][1]
            if use_state:
                c_new = c_new + decay * c_scr[dirn]
                n_new = n_new + decay * n_scr[dirn]
            m_new_row = jnp.zeros(m_scr.shape[1:], F32) + m_new
            if nc > 1:
                c_scr[dirn] = c_new
                n_scr[dirn] = n_new
                m_scr[dirn] = m_new_row
            if emit_state:
                @pl.when(c == nc - 1)
                def _():
                    cn_ref[dirn] = c_new
                    nn_ref[dirn] = n_new
                    mn_ref[dirn] = m_new_row

    def without_update():
        write_out(run(False))

    if emit_state:
        with_update()
    else:
        pl.when(c < nc - 1)(with_update)
        pl.when(c == nc - 1)(without_update)


def _mlstm_scan(q, k, v, gt, row0, nseq, seqlen, init=None, emit_state=False):
    lc = SCAN_CHUNK
    nc = seqlen // lc
    blk0 = row0 // lc
    has_init = init is not None

    def blk_f(s, c):
        return blk0 + s * nc + c

    def blk_b(s, c):
        return blk0 + s * nc + nc - 1 - c

    def chunk_specs(blk):
        return [pl.BlockSpec((lc, DH_A), lambda s, h, c: (blk(s, c), h))] * 3 + [
            pl.BlockSpec((4 * H_A, lc), lambda s, h, c: (0, blk(s, c)))]

    in_specs = chunk_specs(blk_f)
    args = [q, k, v, gt]
    if nc > 1:
        in_specs += chunk_specs(blk_b)
        args += [q, k, v, gt]
    if has_init:
        c0, n0, m0 = init
        in_specs += [pl.BlockSpec((None, 2, None, DH_A, DH_A), lambda s, h, c: (s, 0, h, 0, 0)),
                     pl.BlockSpec((None, 2, H_A, DH_A), lambda s, h, c: (s, 0, 0, 0)),
                     pl.BlockSpec(memory_space=pltpu.SMEM)]
        args += [c0, n0, m0.reshape(-1)]
    out_specs = [pl.BlockSpec((seqlen, DH_A), lambda s, h, c: (s, h))]
    out_shape = [jax.ShapeDtypeStruct((nseq * seqlen, E_A), F32)]
    if emit_state:
        out_specs += [pl.BlockSpec((None, 2, None, DH_A, DH_A), lambda s, h, c: (s, 0, h, 0, 0)),
                      pl.BlockSpec((None, 2, None, 1, DH_A), lambda s, h, c: (s, 0, h, 0, 0)),
                      pl.BlockSpec((None, 2, None, 1, LANES), lambda s, h, c: (s, 0, h, 0, 0))]
        out_shape += [jax.ShapeDtypeStruct((nseq, 2, H_A, DH_A, DH_A), F32),
                      jax.ShapeDtypeStruct((nseq, 2, H_A, 1, DH_A), F32),
                      jax.ShapeDtypeStruct((nseq, 2, H_A, 1, LANES), F32)]
    return pl.pallas_call(
        functools.partial(_mlstm_kernel, nc=nc, has_init=has_init, emit_state=emit_state),
        grid=(nseq, H_A, nc),
        in_specs=in_specs,
        out_specs=out_specs,
        out_shape=out_shape,
        scratch_shapes=[pltpu.VMEM((2, DH_A, DH_A), F32), pltpu.VMEM((2, 1, DH_A), F32),
                        pltpu.VMEM((2, 1, LANES), F32)],
        compiler_params=_params(("arbitrary",) * 3),
        name="mlstm_scan_state" if emit_state else "mlstm_scan_init",
    )(*args)


def _ssd_direction(dirn, xt, bb, ct, cbt_raw, dtr_ref, a_ref, bias_ref, h_ref, yt_ref, xw_ref, use_state):
    lc = SCAN_CHUNK
    r_io = lax.broadcasted_iota(jnp.int32, (lc, lc), 0)
    s_io = lax.broadcasted_iota(jnp.int32, (lc, lc), 1)
    valid = (r_io <= s_io) if dirn == 0 else (r_io >= s_io)
    tri = jnp.where(valid, 1.0, 0.0).astype(BF16)

    dt_rows = _softplus(dtr_ref[...] + bias_ref[:, 0:1])
    a_rows = dt_rows * a_ref[:, 0:1]
    a1 = a_rows.astype(BF16).astype(F32)
    r1 = a_rows - a1
    a2 = r1.astype(BF16).astype(F32)
    a3 = r1 - a2
    pieces = jnp.concatenate([a1, a2, a3, jnp.zeros_like(a1)], axis=0).astype(BF16)
    cs = jnp.dot(pieces, tri, preferred_element_type=F32)
    acs_rows = cs[0:HG_B] + cs[HG_B:2 * HG_B] + cs[2 * HG_B:3 * HG_B]
    a_last = jnp.sum(a_rows, axis=1, keepdims=True)
    wr_rows = jnp.exp(a_last - acs_rows) * dt_rows
    acs_cols = jnp.concatenate([acs_rows, jnp.zeros((LANES - HG_B, lc), F32)], axis=0).T

    cbt = jnp.where(valid, cbt_raw, 0.0)
    if use_state:
        inter_t = jnp.dot(h_ref[...].astype(BF16), ct, preferred_element_type=F32)
        e_rows = jnp.exp(acs_rows)
    for hg in range(HG_B):
        sl = slice(hg * P_B, (hg + 1) * P_B)
        row = slice(hg, hg + 1)
        seg_t = acs_rows[row, :] - acs_cols[:, hg:hg + 1]
        wt = cbt * jnp.exp(jnp.minimum(seg_t, 0.0))
        xth = xt[sl, :]
        yth = jnp.dot((xth * dt_rows[row, :]).astype(BF16), wt.astype(BF16), preferred_element_type=F32)
        if use_state:
            yth = yth + inter_t[sl, :] * e_rows[row, :]
        yt_ref[sl, :] = yth
        xw_ref[sl, :] = xth * wr_rows[row, :]
    return a_last


def _ssd_kernel(*refs, nc, has_init, emit_state):
    shared = nc == 1
    pos = 3 if shared else 6
    chunk_refs = [refs[0:3], refs[0:3] if shared else refs[3:6]]
    dtr_refs = refs[pos:pos + 2]
    a_ref, bias_ref = refs[pos + 2:pos + 4]
    pos += 4
    if has_init:
        h0_ref = refs[pos]
        pos += 1
    y_ref = refs[pos]
    pos += 1
    if emit_state:
        sn_ref = refs[pos]
        pos += 1
    h_scr, yt_scr, xw_scr = refs[pos:pos + 3]

    lc = SCAN_CHUNK
    c = pl.program_id(2)
    use_state = has_init or nc > 1
    hp = HG_B * P_B

    @pl.when(c == 0)
    def _():
        if has_init:
            h_scr[...] = h0_ref[...].reshape(2, hp, D_STATE)
        else:
            h_scr[...] = jnp.zeros(h_scr.shape, F32)
        if not shared:
            y_ref[...] = jnp.zeros(y_ref.shape, F32)

    def chunk_operands(x_ref, b_ref, c_ref):
        bb = b_ref[...].astype(BF16)
        ct = c_ref[...].T.astype(BF16)
        return x_ref[...].T, bb, ct, jnp.dot(bb, ct, preferred_element_type=F32)

    ops = [chunk_operands(*chunk_refs[0])]
    ops.append(ops[0] if shared else chunk_operands(*chunk_refs[1]))
    a_last = [_ssd_direction(dirn, *ops[dirn], dtr_refs[dirn], a_ref.at[dirn], bias_ref.at[dirn], h_scr.at[dirn],
                             yt_scr.at[dirn], xw_scr.at[dirn], use_state) for dirn in range(2)]

    if shared:
        y_ref[...] = (yt_scr[0] + yt_scr[1]).T
    else:
        rows_f = pl.ds(pl.multiple_of(c * lc, lc), lc)
        rows_b = pl.ds(pl.multiple_of((nc - 1 - c) * lc, lc), lc)
        y_ref[rows_f, :] = y_ref[rows_f, :] + yt_scr[0].T
        y_ref[rows_b, :] = y_ref[rows_b, :] + yt_scr[1].T

    def update_state():
        for dirn in range(2):
            upd = jnp.dot(xw_scr[dirn].astype(BF16), ops[dirn][1], preferred_element_type=F32)
            dec = jnp.exp(a_last[dirn])
            for hg in range(HG_B):
                sl = slice(hg * P_B, (hg + 1) * P_B)
                new = upd[sl, :]
                if use_state:
                    new = new + dec[hg:hg + 1, :] * h_scr[dirn, sl, :]
                h_scr[dirn, sl, :] = new
        if emit_state:
            @pl.when(c == nc - 1)
            def _():
                sn_ref[...] = h_scr[...].reshape(2, HG_B, P_B, D_STATE)

    if emit_state:
        update_state()
    elif nc > 1:
        pl.when(c < nc - 1)(update_state)


def _ssd_scan(xbc, dtr, a_neg, dt_bias, row0, nseq, seqlen, init=None, emit_state=False):
    lc = SCAN_CHUNK
    nc = seqlen // lc
    blk0 = row0 // lc
    has_init = init is not None
    hp = HG_B * P_B
    b_off = E_B // D_STATE
    c_off = (E_B + GN_B) // D_STATE

    def blk_f(s, c):
        return blk0 + s * nc + c

    def blk_b(s, c):
        return blk0 + s * nc + nc - 1 - c

    def chunk_specs(blk):
        return [pl.BlockSpec((lc, hp), lambda s, g, c: (blk(s, c), g)),
                pl.BlockSpec((lc, D_STATE), lambda s, g, c: (blk(s, c), b_off + g)),
                pl.BlockSpec((lc, D_STATE), lambda s, g, c: (blk(s, c), c_off + g))]

    in_specs = chunk_specs(blk_f)
    args = [xbc, xbc, xbc]
    if nc > 1:
        in_specs += chunk_specs(blk_b)
        args += [xbc, xbc, xbc]
    table_spec = pl.BlockSpec((2, None, HG_B, LANES), lambda s, g, c: (0, g, 0, 0))
    in_specs += [pl.BlockSpec((None, None, HG_B, lc), lambda s, g, c: (0, g, 0, blk_f(s, c))),
                 pl.BlockSpec((None, None, HG_B, lc), lambda s, g, c: (1, g, 0, blk_b(s, c))),
                 table_spec, table_spec]
    args += [dtr, dtr, a_neg, dt_bias]
    state_spec = pl.BlockSpec((None, 2, HG_B, P_B, D_STATE), lambda s, g, c: (s, 0, g, 0, 0))
    if has_init:
        in_specs.append(state_spec)
        args.append(init)
    out_specs = [pl.BlockSpec((seqlen, hp), lambda s, g, c: (s, g))]
    out_shape = [jax.ShapeDtypeStruct((nseq * seqlen, E_B), F32)]
    if emit_state:
        out_specs.append(state_spec)
        out_shape.append(jax.ShapeDtypeStruct((nseq, 2, H_B, P_B, D_STATE), F32))
    return pl.pallas_call(
        functools.partial(_ssd_kernel, nc=nc, has_init=has_init, emit_state=emit_state),
        grid=(nseq, G_B, nc),
        in_specs=in_specs,
        out_specs=out_specs,
        out_shape=out_shape,
        scratch_shapes=[pltpu.VMEM((2, hp, D_STATE), F32), pltpu.VMEM((2, hp, lc), F32),
                        pltpu.VMEM((2, hp, lc), F32)],
        compiler_params=_params(("arbitrary",) * 3),
        name="ssd_scan_state" if emit_state else "ssd_scan_init",
    )(*args)


def _mixer_out_kernel(yp_ref, ys_ref, u_ref, z_ref, x_ref, mod_ref, nw_ref, sk_ref, w_ref, lnw_ref, lnb_ref,
                      o_ref, *, npt, grouped):
    m = pl.program_id(0)
    y = jnp.where(m < npt, yp_ref[...], ys_ref[...])
    u = u_ref[...]
    gate = _silu(z_ref[...])
    if grouped:
        parts = []
        for hd in range(H_A):
            seg = y[:, hd * DH_A:(hd + 1) * DH_A]
            mu = jnp.mean(seg, axis=-1, keepdims=True)
            sc = seg - mu
            var = jnp.mean(sc * sc, axis=-1, keepdims=True)
            parts.append(sc * lax.rsqrt(var + EPS))
        hn = jnp.concatenate(parts, axis=1) * nw_ref[...]
        t = (hn + sk_ref[...] * u) * gate
    else:
        t0 = (y + sk_ref[...] * u) * gate
        t = t0 * lax.rsqrt(jnp.mean(t0 * t0, axis=-1, keepdims=True) + EPS) * nw_ref[...]
    out = jnp.dot(t.astype(BF16), w_ref[...], preferred_element_type=F32)
    mod = mod_ref[...]
    r = ALPHA * x_ref[...] + mod[2:3, :] * out
    o_ref[...] = _layer_norm_rows(r, lnw_ref[...], lnb_ref[...])


def _mixer_out(yp, ys, u, proj, zblk, x, mod, nw, sk, w_bf, lnw, lnb, tp, ls, grouped, name):
    t = x.shape[0]
    e = w_bf.shape[0]
    npt = tp // TM
    nst = ys.shape[0] // TM
    vec_e = pl.BlockSpec((1, e), lambda m: (0, 0))
    vec_d = pl.BlockSpec((1, D_MODEL), lambda m: (0, 0))
    return pl.pallas_call(
        functools.partial(_mixer_out_kernel, npt=npt, grouped=grouped),
        grid=(t // TM,),
        in_specs=[pl.BlockSpec((TM, e), lambda m: (jnp.minimum(m, npt - 1), 0)),
                  pl.BlockSpec((TM, e), lambda m: (jnp.clip(m - npt, 0, nst - 1), 0)),
                  pl.BlockSpec((TM, e), lambda m: (m, 0)),
                  pl.BlockSpec((TM, e), lambda m: (m, zblk)),
                  pl.BlockSpec((TM, D_MODEL), lambda m: (m, 0)),
                  pl.BlockSpec((None, MOD_ROWS, D_MODEL), lambda m: (_seq_of_tile(m, TM, tp, ls), 0, 0)),
                  vec_e, vec_e,
                  pl.BlockSpec((e, D_MODEL), lambda m: (0, 0)),
                  vec_d, vec_d],
        out_specs=pl.BlockSpec((TM, D_MODEL), lambda m: (m, 0)),
        out_shape=jax.ShapeDtypeStruct((t, D_MODEL), F32),
        compiler_params=_params(("arbitrary",)),
        name=name,
    )(yp, ys, u, proj, x, mod, nw.reshape(1, e), sk.reshape(1, e), w_bf, lnw.reshape(1, D_MODEL),
      lnb.reshape(1, D_MODEL))


def _router_kernel(x_ref, mod_ref, w_ref, b_ref, hm_ref, comb_ref):
    mod = mod_ref[...]
    hm = (x_ref[...] * (1.0 + mod[4:5, :]) + mod[3:4, :]).astype(BF16)
    hm_ref[...] = hm
    logits = jnp.dot(hm, w_ref[...], preferred_element_type=F32) + b_ref[...]
    lane = lax.broadcasted_iota(jnp.int32, logits.shape, 1).astype(F32)
    big = float(2 * LANES)
    glog = jnp.where(lane < MOE_GROUPS, logits, -jnp.inf)
    ge = jnp.exp(glog - jnp.max(glog, axis=1, keepdims=True))
    prob = ge / jnp.sum(ge, axis=1, keepdims=True)
    gp = jnp.max(prob, axis=1, keepdims=True)
    gi = jnp.min(jnp.where(prob == gp, lane, big), axis=1, keepdims=True)
    lo = ROUTER_OFF + gi * MOE_EXPERTS
    ev = jnp.where((lane >= lo) & (lane < lo + MOE_EXPERTS), logits, -jnp.inf)
    e1 = jnp.max(ev, axis=1, keepdims=True)
    i1 = jnp.min(jnp.where(ev == e1, lane, big), axis=1, keepdims=True)
    ev2 = jnp.where(lane == i1, -jnp.inf, ev)
    e2 = jnp.max(ev2, axis=1, keepdims=True)
    i2 = jnp.min(jnp.where(ev2 == e2, lane, big), axis=1, keepdims=True)
    t2 = jnp.exp(e2 - e1)
    w1 = 1.0 / (1.0 + t2)
    w2 = t2 / (1.0 + t2)
    comb_ref[...] = jnp.where(lane == i1, w1 * gp, jnp.where(lane == i2, w2 * gp, 0.0))


def _router(x, mod, w_r, b_r, tp, ls):
    t = x.shape[0]
    tm = TM_MOE
    return pl.pallas_call(
        _router_kernel,
        grid=(t // tm,),
        in_specs=[pl.BlockSpec((tm, D_MODEL), lambda m: (m, 0)),
                  pl.BlockSpec((None, MOD_ROWS, D_MODEL), lambda m: (_seq_of_tile(m, tm, tp, ls), 0, 0)),
                  pl.BlockSpec((D_MODEL, LANES), lambda m: (0, 0)),
                  pl.BlockSpec((1, LANES), lambda m: (0, 0))],
        out_specs=[pl.BlockSpec((tm, D_MODEL), lambda m: (m, 0)),
                   pl.BlockSpec((tm, LANES), lambda m: (m, 0))],
        out_shape=[jax.ShapeDtypeStruct((t, D_MODEL), BF16), jax.ShapeDtypeStruct((t, LANES), F32)],
        compiler_params=_params(("arbitrary",)),
        name="moe_router",
    )(x, mod, w_r, b_r)


def _moe_kernel(hm_ref, comb_ref, wg_ref, wu_ref, wd_ref, x_ref, mod_ref, lnw_ref, lnb_ref, o_ref, acc_ref):
    e = pl.program_id(1)

    @pl.when(e == 0)
    def _():
        acc_ref[...] = jnp.zeros(acc_ref.shape, F32)

    hm = hm_ref[...]
    a = _silu(jnp.dot(hm, wg_ref[...], preferred_element_type=F32)) * jnp.dot(
        hm, wu_ref[...], preferred_element_type=F32)
    ce = _pick_col(comb_ref[...], ROUTER_OFF + e)
    acc_ref[...] += jnp.dot((a * ce).astype(BF16), wd_ref[...], preferred_element_type=F32)

    @pl.when(e == N_EXPERTS - 1)
    def _():
        mod = mod_ref[...]
        r = ALPHA * x_ref[...] + mod[5:6, :] * acc_ref[...]
        o_ref[...] = _layer_norm_rows(r, lnw_ref[...], lnb_ref[...])


def _moe(hm, comb, wg, wu, wd, x, mod, lnw, lnb, tp, ls):
    t = x.shape[0]
    tm = TM_MOE
    vec_d = pl.BlockSpec((1, D_MODEL), lambda m, e: (0, 0))
    return pl.pallas_call(
        _moe_kernel,
        grid=(t // tm, N_EXPERTS),
        in_specs=[pl.BlockSpec((tm, D_MODEL), lambda m, e: (m, 0)),
                  pl.BlockSpec((tm, LANES), lambda m, e: (m, 0)),
                  pl.BlockSpec((None, None, D_MODEL, D_FF_E), lambda m, e: (e // MOE_EXPERTS, e % MOE_EXPERTS, 0, 0)),
                  pl.BlockSpec((None, None, D_MODEL, D_FF_E), lambda m, e: (e // MOE_EXPERTS, e % MOE_EXPERTS, 0, 0)),
                  pl.BlockSpec((None, None, D_FF_E, D_MODEL), lambda m, e: (e // MOE_EXPERTS, e % MOE_EXPERTS, 0, 0)),
                  pl.BlockSpec((tm, D_MODEL), lambda m, e: (m, 0)),
                  pl.BlockSpec((None, MOD_ROWS, D_MODEL), lambda m, e: (_seq_of_tile(m, tm, tp, ls), 0, 0)),
                  vec_d, vec_d],
        out_specs=pl.BlockSpec((tm, D_MODEL), lambda m, e: (m, 0)),
        out_shape=jax.ShapeDtypeStruct((t, D_MODEL), F32),
        scratch_shapes=[pltpu.VMEM((tm, D_MODEL), F32)],
        compiler_params=_params(("arbitrary", "arbitrary")),
        name="moe_experts",
    )(hm, comb, wg, wu, wd, x, mod, lnw.reshape(1, D_MODEL), lnb.reshape(1, D_MODEL))


def _moe_layer(x, mod, w_rg, b_rg, w_re, b_re, w_gate, w_up, w_down, lnw, lnb, tp, ls):
    w_r = jnp.concatenate([w_rg, jnp.moveaxis(w_re, 0, 1).reshape(D_MODEL, N_EXPERTS)], axis=1)
    w_r = jnp.pad(w_r, ((0, 0), (0, LANES - w_r.shape[1]))).astype(BF16)
    b_r = jnp.pad(jnp.concatenate([b_rg, b_re.reshape(-1)]), (0, LANES - MOE_GROUPS - N_EXPERTS)).reshape(1, LANES)
    hm, comb = _router(x, mod, w_r, b_r, tp, ls)
    return _moe(hm, comb, w_gate.astype(BF16), w_up.astype(BF16), w_down.astype(BF16), x, mod, lnw, lnb, tp, ls)


def _blockdiag_lanes(w):
    per = LANES // QKV_BLOCK
    wb = w.reshape(-1, per, QKV_BLOCK, QKV_BLOCK)
    eye = jnp.eye(per, dtype=w.dtype)
    return jnp.einsum('bncd,nm->bncmd', wb, eye).reshape(-1, LANES, LANES)


def _grid_to_cols(x):
    b, l, dd = x.shape
    return x.reshape(b, l // GRID_W, GRID_W, dd).transpose(0, 2, 1, 3).reshape(b, l, dd)


def _cols_to_grid(x):
    b, l, dd = x.shape
    return x.reshape(b, GRID_W, l // GRID_W, dd).transpose(0, 2, 1, 3).reshape(b, l, dd)


def kernel(x_prompt, x_sample, state_mlstm_C, state_mlstm_n, state_mlstm_m, state_ssd, c, c_ctx, ada_w, ada_b, ln_w, ln_b, a_w_in, a_conv_w, a_conv_b, a_w_q, a_w_k, a_w_v, a_w_gates, a_b_gates, a_norm_w, a_skip, a_w_down, b_w_in, b_conv_w, b_conv_b, b_dt_bias, b_a_log, b_d_skip, b_norm_w, b_w_out, moe_w_rg, moe_b_rg, moe_w_re, moe_b_re, moe_w_gate, moe_w_up, moe_w_down):
    bp, lp, _ = x_prompt.shape
    bs, ls, _ = x_sample.shape
    tp, ts = bp * lp, bs * ls
    assert lp % SCAN_CHUNK == 0 and ls % SCAN_CHUNK == 0 and tp % 512 == 0 and ls % 512 == 0
    assert bs + 1 <= N_MOD_SEQ

    cvec = jnp.concatenate([c_ctx[None, :], c, jnp.zeros((N_MOD_SEQ - 1 - bs, D_MODEL), F32)], axis=0)
    mods = _ada(cvec, ada_w, ada_b).reshape(DEPTH, N_MOD_SEQ, 6, D_MODEL)
    mods = jnp.pad(mods, ((0, 0), (0, 0), (0, MOD_ROWS - 6), (0, 0)))

    x = jnp.concatenate([x_prompt.reshape(tp, D_MODEL), x_sample.reshape(ts, D_MODEL)], axis=0)

    mod = mods[0]
    proj = _modmm(x, mod, a_w_in[0], 2 * E_A, 1024, tp, ls, 0, 1, "mlstm_in_proj")
    xc = _conv_silu(proj, 0, E_A, a_conv_w[0], a_conv_b[0], tp, lp, ls, "mlstm_conv")
    wqk = jnp.concatenate([_blockdiag_lanes(a_w_q[0]), _blockdiag_lanes(a_w_k[0])], axis=-1).astype(BF16)
    wv = _blockdiag_lanes(a_w_v[0]).astype(BF16)
    ngate = 4 * H_A
    wg = jnp.pad(a_w_gates[0].reshape(3, E_A, ngate), ((0, 0), (0, 0), (0, LANES - ngate))).astype(BF16)
    bg = jnp.pad(a_b_gates[0], (0, LANES - ngate)).reshape(1, LANES)
    q, k, v, g = _qkv_gates(xc, proj, wqk, wv, wg, bg)
    gt = g[:, :ngate].T
    hs_p, c_new, n_new, m_new = _mlstm_scan(q, k, v, gt, 0, bp, lp, emit_state=True)
    (hs_s,) = _mlstm_scan(q, k, v, gt, tp, bs, ls,
                          init=(state_mlstm_C[:, 0], state_mlstm_n[:, 0], state_mlstm_m[:, 0]))
    x = _mixer_out(hs_p, hs_s, xc, proj, 1, x, mod, a_norm_w[0], a_skip[0], a_w_down[0].astype(BF16),
                   ln_w[0, 0], ln_b[0, 0], tp, ls, True, "mlstm_out")
    x = _moe_layer(x, mod, moe_w_rg[0], moe_b_rg[0], moe_w_re[0], moe_b_re[0], moe_w_gate[0], moe_w_up[0],
                   moe_w_down[0], ln_w[0, 1], ln_b[0, 1], tp, ls)

    mod = mods[1]
    x = jnp.concatenate([x[:tp], _grid_to_cols(x[tp:].reshape(bs, ls, D_MODEL)).reshape(ts, D_MODEL)], axis=0)
    n_main = 2 * E_B + 2 * GN_B
    proj = _modmm(x, mod, b_w_in[0], n_main, 1024, tp, ls, 0, 1, "ssd_in_proj")
    dt = _modmm(x, mod, b_w_in[0][:, n_main:], 2 * H_B, 2 * H_B, tp, ls, 0, 1, "ssd_dt_proj")
    xbc = _conv_silu(proj, E_B, E_B + 2 * GN_B, b_conv_w[0], b_conv_b[0], tp, lp, ls, "ssd_conv")
    lane_bcast = (2, G_B, HG_B, LANES)
    a_neg = jnp.broadcast_to(-jnp.exp(b_a_log[0]).reshape(2, G_B, HG_B, 1), lane_bcast)
    dt_bias = jnp.broadcast_to(b_dt_bias[0].reshape(2, G_B, HG_B, 1), lane_bcast)
    dtr = dt.T.reshape(2, G_B, HG_B, tp + ts)
    y_p, s_new = _ssd_scan(xbc, dtr, a_neg, dt_bias, 0, bp, lp, emit_state=True)
    (y_s,) = _ssd_scan(xbc, dtr, a_neg, dt_bias, tp, bs, ls, init=state_ssd[:, 0])
    x = _mixer_out(y_p, y_s, xbc, proj, 0, x, mod, b_norm_w[0], jnp.repeat(b_d_skip[0], P_B),
                   b_w_out[0].astype(BF16), ln_w[1, 0], ln_b[1, 0], tp, ls, False, "ssd_out")
    x = _moe_layer(x, mod, moe_w_rg[1], moe_b_rg[1], moe_w_re[1], moe_b_re[1], moe_w_gate[1], moe_w_up[1],
                   moe_w_down[1], ln_w[1, 1], ln_b[1, 1], tp, ls)

    y_prompt = x[:tp].reshape(bp, lp, D_MODEL)
    y_sample = _cols_to_grid(x[tp:].reshape(bs, ls, D_MODEL))
    return (y_prompt, y_sample,
            c_new[:, None],
            n_new.reshape(bp, 1, 2, H_A, DH_A),
            m_new[:, :, :, 0, 0][:, None],
            s_new[:, None])
```

```python
import functools
import math

import jax
import jax.numpy as jnp
from jax import lax
from jax.experimental import pallas as pl
from jax.experimental.pallas import tpu as pltpu

F32 = jnp.float32
BF16 = jnp.bfloat16

D_MODEL = 1024
DEPTH = 2
GRID_W = 64
CONV_W = 5
E_A = 2 * D_MODEL
H_A = 4
DH_A = E_A // H_A
QKV_BLOCK = 4
E_B = 2 * D_MODEL
P_B = 64
H_B = E_B // P_B
G_B = 4
HG_B = H_B // G_B
D_STATE = 128
GN_B = G_B * D_STATE
MOE_GROUPS = 4
MOE_EXPERTS = 4
N_EXPERTS = MOE_GROUPS * MOE_EXPERTS
D_FF_E = 256
ALPHA = (2.0 * DEPTH) ** 0.25
EPS = 1e-5

LANES = 128
SUBLANES = 8
TM = 256
TM_MOE = 512
SCAN_CHUNK = 256
MOD_ROWS = 8
N_MOD_SEQ = 16
VMEM_LIMIT = 56 * 1024 * 1024
ROUTER_OFF = MOE_GROUPS


def _params(sem):
    return pltpu.CompilerParams(dimension_semantics=sem, vmem_limit_bytes=VMEM_LIMIT)


def _seq_of_tile(m, tm, tp, ls):
    npt = tp // tm
    return jnp.where(m < npt, 0, 1 + (m - npt) // (ls // tm))


def _silu(x):
    return x * jax.nn.sigmoid(x)


def _softplus(x):
    return jnp.maximum(x, 0.0) + jnp.log1p(jnp.exp(-jnp.abs(x)))


def _layer_norm_rows(r, w, b):
    mu = jnp.mean(r, axis=-1, keepdims=True)
    rc = r - mu
    var = jnp.mean(rc * rc, axis=-1, keepdims=True)
    return rc * lax.rsqrt(var + EPS) * w + b


def _ada_kernel(c_ref, w_ref, b_ref, o_ref):
    s = _silu(c_ref[...])
    o_ref[...] = jnp.dot(s.astype(BF16), w_ref[...].astype(BF16), preferred_element_type=F32) + b_ref[...]


def _ada(cvec, ada_w, ada_b):
    nl = ada_w.shape[0]
    n_out = ada_w.shape[2]
    tn = 1536
    return pl.pallas_call(
        _ada_kernel,
        grid=(nl, n_out // tn),
        in_specs=[pl.BlockSpec((N_MOD_SEQ, D_MODEL), lambda l, n: (0, 0)),
                  pl.BlockSpec((None, D_MODEL, tn), lambda l, n: (l, 0, n)),
                  pl.BlockSpec((None, 1, tn), lambda l, n: (l, 0, n))],
        out_specs=pl.BlockSpec((None, N_MOD_SEQ, tn), lambda l, n: (l, 0, n)),
        out_shape=jax.ShapeDtypeStruct((nl, N_MOD_SEQ, n_out), F32),
        compiler_params=_params(("arbitrary", "arbitrary")),
        name="ada_mod",
    )(cvec, ada_w, ada_b.reshape(nl, 1, n_out))


def _modmm_kernel(x_ref, mod_ref, w_ref, o_ref, *, shift_row, scale_row):
    mod = mod_ref[...]
    h = x_ref[...] * (1.0 + mod[scale_row:scale_row + 1, :]) + mod[shift_row:shift_row + 1, :]
    o_ref[...] = jnp.dot(h.astype(BF16), w_ref[...], preferred_element_type=F32)


def _resident(shape):
    return pl.BlockSpec(shape, lambda *_: (0,) * len(shape), pipeline_mode=pl.Buffered(1))


def _modmm(x, mod, w_bf, tp, ls, shift_row, scale_row, name):
    t = x.shape[0]
    n_out = w_bf.shape[1]
    tm = TM
    return pl.pallas_call(
        functools.partial(_modmm_kernel, shift_row=shift_row, scale_row=scale_row),
        grid=(t // tm,),
        in_specs=[pl.BlockSpec((tm, D_MODEL), lambda m: (m, 0)),
                  pl.BlockSpec((None, MOD_ROWS, D_MODEL), lambda m: (_seq_of_tile(m, tm, tp, ls), 0, 0)),
                  _resident((D_MODEL, n_out))],
        out_specs=pl.BlockSpec((tm, n_out), lambda m: (m, 0)),
        out_shape=jax.ShapeDtypeStruct((t, n_out), F32),
        compiler_params=_params(("arbitrary",)),
        name=name,
    )(x, mod, w_bf)


def _conv_kernel(cur_ref, prev_ref, next_ref, w_ref, b_ref, o_ref, ext_ref, *, tm, tp, lp, ls):
    m = pl.program_id(0)
    npt = tp // tm
    pos = jnp.where(m < npt, m % (lp // tm), (m - npt) % (ls // tm))
    last_pos = jnp.where(m < npt, lp // tm - 1, ls // tm - 1)
    halo = SUBLANES
    ext_ref[0:halo, :] = jnp.where(pos == 0, 0.0, prev_ref[...])
    ext_ref[halo:halo + tm, :] = cur_ref[...]
    ext_ref[halo + tm:2 * halo + tm, :] = jnp.where(pos == last_pos, 0.0, next_ref[...])
    acc = jnp.zeros(o_ref.shape, F32) + b_ref[...]
    for j in range(CONV_W):
        acc = acc + ext_ref[pl.ds(halo - CONV_W // 2 + j, tm), :] * w_ref[j:j + 1, :]
    o_ref[...] = _silu(acc)


def _conv_silu(a, col0, width, w, b, tp, lp, ls, name):
    t = a.shape[0]
    tm, tc = TM, 1024
    cb0 = col0 // tc
    nrow8 = t // SUBLANES
    per = tm // SUBLANES
    return pl.pallas_call(
        functools.partial(_conv_kernel, tm=tm, tp=tp, lp=lp, ls=ls),
        grid=(t // tm, width // tc),
        in_specs=[pl.BlockSpec((tm, tc), lambda m, j: (m, cb0 + j)),
                  pl.BlockSpec((SUBLANES, tc), lambda m, j: (jnp.maximum(m * per - 1, 0), cb0 + j)),
                  pl.BlockSpec((SUBLANES, tc), lambda m, j: (jnp.minimum((m + 1) * per, nrow8 - 1), cb0 + j)),
                  pl.BlockSpec((CONV_W, tc), lambda m, j: (0, j)),
                  pl.BlockSpec((1, tc), lambda m, j: (0, j))],
        out_specs=pl.BlockSpec((tm, tc), lambda m, j: (m, j)),
        out_shape=jax.ShapeDtypeStruct((t, width), F32),
        scratch_shapes=[pltpu.VMEM((tm + 2 * SUBLANES, tc), F32)],
        compiler_params=_params(("arbitrary", "arbitrary")),
        name=name,
    )(a, a, a, w, b.reshape(1, width))


def _qkv_kernel(xc_ref, xu_ref, wqk_ref, wv_ref, wg_ref, bg_ref, q_ref, k_ref, v_ref, g_ref):
    nblk = E_A // LANES
    g = jnp.zeros(g_ref.shape, F32) + bg_ref[...]
    for b in range(nblk):
        sl = slice(b * LANES, (b + 1) * LANES)
        xcb = xc_ref[:, sl].astype(BF16)
        xub = xu_ref[:, sl].astype(BF16)
        qk = jnp.dot(xcb, wqk_ref[b], preferred_element_type=F32)
        vv = jnp.dot(xub, wv_ref[b], preferred_element_type=F32)
        qb = qk[:, :LANES].astype(BF16)
        kb = qk[:, LANES:].astype(BF16)
        vb = vv.astype(BF16)
        q_ref[:, sl] = qb
        k_ref[:, sl] = kb
        v_ref[:, sl] = vb
        g = g + jnp.dot(qb, wg_ref[0, sl, :], preferred_element_type=F32)
        g = g + jnp.dot(kb, wg_ref[1, sl, :], preferred_element_type=F32)
        g = g + jnp.dot(vb, wg_ref[2, sl, :], preferred_element_type=F32)
    lane = lax.broadcasted_iota(jnp.int32, g.shape, 1)
    is_forget = (lane & (2 * H_A - 1)) >= H_A
    log_sig = jnp.minimum(g, 0.0) - jnp.log1p(jnp.exp(-jnp.abs(g)))
    g_ref[...] = jnp.where(is_forget, log_sig, g)


def _qkv_gates(xc, proj, wqk, wv, wg, bg):
    t = xc.shape[0]
    nblk = E_A // LANES
    tile = pl.BlockSpec((TM, E_A), lambda m: (m, 0))
    return pl.pallas_call(
        _qkv_kernel,
        grid=(t // TM,),
        in_specs=[tile, tile,
                  pl.BlockSpec((nblk, LANES, 2 * LANES), lambda m: (0, 0, 0)),
                  pl.BlockSpec((nblk, LANES, LANES), lambda m: (0, 0, 0)),
                  pl.BlockSpec((3, E_A, LANES), lambda m: (0, 0, 0)),
                  pl.BlockSpec((1, LANES), lambda m: (0, 0))],
        out_specs=[tile, tile, tile, pl.BlockSpec((TM, LANES), lambda m: (m, 0))],
        out_shape=[jax.ShapeDtypeStruct((t, E_A), BF16)] * 3 + [jax.ShapeDtypeStruct((t, LANES), F32)],
        compiler_params=_params(("arbitrary",)),
        name="mlstm_qkv_gates",
    )(xc, proj, wqk, wv, wg, bg)


def _order_mask(dirn, n, rows_are_later):
    a_io = lax.broadcasted_iota(jnp.int32, (n, n), 0)
    b_io = lax.broadcasted_iota(jnp.int32, (n, n), 1)
    if (dirn == 0) == rows_are_later:
        return b_io <= a_io
    return b_io >= a_io


def _exact_cumsum_rows(rows, tri):
    nrow = rows.shape[0]
    p1 = rows.astype(BF16).astype(F32)
    r1 = rows - p1
    p2 = r1.astype(BF16).astype(F32)
    p3 = r1 - p2
    pieces = jnp.concatenate([p1, p2, p3, jnp.zeros_like(p1)], axis=0).astype(BF16)
    cs = jnp.dot(pieces, tri, preferred_element_type=F32)
    return cs[0:nrow] + cs[nrow:2 * nrow] + cs[2 * nrow:3 * nrow]


def _pick_row(blk, idx):
    sub = lax.broadcasted_iota(jnp.int32, (blk.shape[0], 1), 0)
    return jnp.sum(jnp.where(sub == idx, blk, 0.0), axis=0, keepdims=True)


def _rows_to_cols(rows):
    pad = jnp.zeros((LANES - rows.shape[0], rows.shape[1]), F32)
    return jnp.concatenate([rows, pad], axis=0).T


def _pick_col(blk, idx):
    lane = lax.broadcasted_iota(jnp.int32, (1, blk.shape[1]), 1)
    return jnp.sum(jnp.where(lane == idx, blk, 0.0), axis=1, keepdims=True)


def _mlstm_direction(dirn, h, q, k, v, s_qk, kt, gt_ref, c_ref, n_ref, m_ref, use_state, need_update):
    lc = SCAN_CHUNK
    scale = DH_A ** -0.5
    log_scale = -0.5 * math.log(DH_A)
    gates = gt_ref[dirn * 2 * H_A:(dirn + 1) * 2 * H_A, :]
    tri = jnp.where(_order_mask(dirn, lc, False), 1.0, 0.0).astype(BF16)
    csum = _exact_cumsum_rows(gates, tri)
    i_row = _pick_row(gates, h)
    f_row = _pick_row(gates, H_A + h)
    b_row = _pick_row(csum, H_A + h)
    v_row = i_row - b_row
    valid = _order_mask(dirn, lc, True)
    m_prev = m_ref[:, 0:1]
    run_max = jnp.max(jnp.where(valid, v_row, -jnp.inf), axis=1, keepdims=True)
    u_col = -jnp.maximum(m_prev, run_max)
    p = s_qk * jnp.exp(jnp.where(valid, u_col + (v_row + log_scale), -jnp.inf))
    num = jnp.dot(p.astype(BF16), v, preferred_element_type=F32)
    den = jnp.sum(p, axis=1, keepdims=True)
    if use_state:
        w_inter = jnp.exp(m_prev + u_col) * scale
        qc = jnp.dot(q, c_ref[...].astype(BF16), preferred_element_type=F32)
        qn = jnp.sum(q.astype(F32) * n_ref[...], axis=1, keepdims=True)
        num = num + w_inter * qc
        den = den + w_inter * qn
    b_col = _rows_to_cols(b_row)[:, 0:1]
    h_out = num / jnp.maximum(jnp.abs(den), jnp.exp(u_col - b_col))
    if not need_update:
        return h_out, None
    b_last = jnp.sum(f_row, axis=1, keepdims=True)
    logk = b_last + v_row
    m_new = jnp.maximum(b_last + m_prev, jnp.max(logk, axis=1, keepdims=True))
    wk = jnp.exp(logk - m_new)
    c_new = jnp.dot((kt * wk).astype(BF16), v, preferred_element_type=F32)
    wk8 = (jnp.zeros((SUBLANES, lc), F32) + wk).astype(BF16)
    n_new = jnp.dot(wk8, k, preferred_element_type=F32)[0:1, :]
    return h_out, (c_new, n_new, m_new, jnp.exp(b_last + m_prev - m_new))


def _mlstm_kernel(*refs, nc, has_init, emit_state):
    shared = nc == 1
    pos = 4 if shared else 8
    chunk_refs = [refs[0:4], refs[0:4] if shared else refs[4:8]]
    if has_init:
        c0_ref, n0_ref, m0_ref = refs[pos:pos + 3]
        pos += 3
    hs_ref = refs[pos]
    pos += 1
    if emit_state:
        cn_ref, nn_ref, mn_ref = refs[pos:pos + 3]
        pos += 3
    c_scr, n_scr, m_scr = refs[pos:pos + 3]

    lc = SCAN_CHUNK
    s_id, h, c = (pl.program_id(i) for i in range(3))
    use_state = has_init or nc > 1

    @pl.when(c == 0)
    def _():
        for dirn in range(2):
            if has_init:
                c_scr[dirn] = c0_ref[dirn]
                n_scr[dirn] = _pick_row(n0_ref[dirn], h)
                m_scr[dirn] = jnp.zeros(m_scr.shape[1:], F32) + m0_ref[(s_id * 2 + dirn) * H_A + h]
            else:
                c_scr[dirn] = jnp.zeros(c_scr.shape[1:], F32)
                n_scr[dirn] = jnp.zeros(n_scr.shape[1:], F32)
                m_scr[dirn] = jnp.zeros(m_scr.shape[1:], F32)
        if not shared:
            hs_ref[...] = jnp.zeros(hs_ref.shape, F32)

    def chunk_operands(q_ref, k_ref, v_ref, gt_ref):
        q = q_ref[...]
        k = k_ref[...]
        s_qk = lax.dot_general(q, k, (((1,), (1,)), ((), ())), preferred_element_type=F32)
        return q, k, v_ref[...], s_qk, k.astype(F32).T, gt_ref

    ops = [chunk_operands(*chunk_refs[0])]
    ops.append(ops[0] if shared else chunk_operands(*chunk_refs[1]))

    def run(need_update):
        return [_mlstm_direction(dirn, h, *ops[dirn], c_scr.at[dirn], n_scr.at[dirn], m_scr.at[dirn],
                                 use_state, need_update) for dirn in range(2)]

    def write_out(res):
        if shared:
            hs_ref[...] = res[0][0] + res[1][0]
        else:
            rows_f = pl.ds(pl.multiple_of(c * lc, lc), lc)
            rows_b = pl.ds(pl.multiple_of((nc - 1 - c) * lc, lc), lc)
            hs_ref[rows_f, :] = hs_ref[rows_f, :] + res[0][0]
            hs_ref[rows_b, :] = hs_ref[rows_b, :] + res[1][0]

    def with_update():
        res = run(True)
        write_out(res)
        for dirn in range(2):
            c_new, n_new, m_new, decay = res[dirn][1]
            if use_state:
                c_new = c_new + decay * c_scr[dirn]
                n_new = n_new + decay * n_scr[dirn]
            m_new_row = jnp.zeros(m_scr.shape[1:], F32) + m_new
            if nc > 1:
                c_scr[dirn] = c_new
                n_scr[dirn] = n_new
                m_scr[dirn] = m_new_row
            if emit_state:
                @pl.when(c == nc - 1)
                def _():
                    cn_ref[dirn] = c_new
                    nn_ref[dirn] = n_new
                    mn_ref[dirn] = m_new_row

    def without_update():
        write_out(run(False))

    if emit_state:
        with_update()
    else:
        pl.when(c < nc - 1)(with_update)
        pl.when(c == nc - 1)(without_update)


def _mlstm_scan(q, k, v, gt, row0, nseq, seqlen, init=None, emit_state=False):
    lc = SCAN_CHUNK
    nc = seqlen // lc
    blk0 = row0 // lc
    has_init = init is not None

    def blk_f(s, c):
        return blk0 + s * nc + c

    def blk_b(s, c):
        return blk0 + s * nc + nc - 1 - c

    def chunk_specs(blk):
        return [pl.BlockSpec((lc, DH_A), lambda s, h, c: (blk(s, c), h))] * 3 + [
            pl.BlockSpec((4 * H_A, lc), lambda s, h, c: (0, blk(s, c)))]

    in_specs = chunk_specs(blk_f)
    args = [q, k, v, gt]
    if nc > 1:
        in_specs += chunk_specs(blk_b)
        args += [q, k, v, gt]
    if has_init:
        c0, n0, m0 = init
        in_specs += [pl.BlockSpec((None, 2, None, DH_A, DH_A), lambda s, h, c: (s, 0, h, 0, 0)),
                     pl.BlockSpec((None, 2, H_A, DH_A), lambda s, h, c: (s, 0, 0, 0)),
                     pl.BlockSpec(memory_space=pltpu.SMEM)]
        args += [c0, n0, m0.reshape(-1)]
    out_specs = [pl.BlockSpec((seqlen, DH_A), lambda s, h, c: (s, h))]
    out_shape = [jax.ShapeDtypeStruct((nseq * seqlen, E_A), F32)]
    if emit_state:
        out_specs += [pl.BlockSpec((None, 2, None, DH_A, DH_A), lambda s, h, c: (s, 0, h, 0, 0)),
                      pl.BlockSpec((None, 2, None, 1, DH_A), lambda s, h, c: (s, 0, h, 0, 0)),
                      pl.BlockSpec((None, 2, None, 1, LANES), lambda s, h, c: (s, 0, h, 0, 0))]
        out_shape += [jax.ShapeDtypeStruct((nseq, 2, H_A, DH_A, DH_A), F32),
                      jax.ShapeDtypeStruct((nseq, 2, H_A, 1, DH_A), F32),
                      jax.ShapeDtypeStruct((nseq, 2, H_A, 1, LANES), F32)]
    return pl.pallas_call(
        functools.partial(_mlstm_kernel, nc=nc, has_init=has_init, emit_state=emit_state),
        grid=(nseq, H_A, nc),
        in_specs=in_specs,
        out_specs=out_specs,
        out_shape=out_shape,
        scratch_shapes=[pltpu.VMEM((2, DH_A, DH_A), F32), pltpu.VMEM((2, 1, DH_A), F32),
                        pltpu.VMEM((2, 1, LANES), F32)],
        compiler_params=_params(("arbitrary",) * 3),
        name="mlstm_scan_state" if emit_state else "mlstm_scan_init",
    )(*args)


def _ssd_direction(dirn, xt, bb, ct, cbt_raw, dtr_ref, a_ref, bias_ref, h_ref, yt_ref, xw_ref, use_state):
    lc = SCAN_CHUNK
    r_io = lax.broadcasted_iota(jnp.int32, (lc, lc), 0)
    s_io = lax.broadcasted_iota(jnp.int32, (lc, lc), 1)
    valid = (r_io <= s_io) if dirn == 0 else (r_io >= s_io)
    tri = jnp.where(valid, 1.0, 0.0).astype(BF16)

    dt_rows = _softplus(dtr_ref[...] + bias_ref[:, 0:1])
    a_rows = dt_rows * a_ref[:, 0:1]
    a1 = a_rows.astype(BF16).astype(F32)
    r1 = a_rows - a1
    a2 = r1.astype(BF16).astype(F32)
    a3 = r1 - a2
    pieces = jnp.concatenate([a1, a2, a3, jnp.zeros_like(a1)], axis=0).astype(BF16)
    cs = jnp.dot(pieces, tri, preferred_element_type=F32)
    acs_rows = cs[0:HG_B] + cs[HG_B:2 * HG_B] + cs[2 * HG_B:3 * HG_B]
    a_last = jnp.sum(a_rows, axis=1, keepdims=True)
    wr_rows = jnp.exp(a_last - acs_rows) * dt_rows
    acs_cols = jnp.concatenate([acs_rows, jnp.zeros((LANES - HG_B, lc), F32)], axis=0).T

    cbt = jnp.where(valid, cbt_raw, 0.0)
    if use_state:
        inter_t = jnp.dot(h_ref[...].astype(BF16), ct, preferred_element_type=F32)
        e_rows = jnp.exp(acs_rows)
    for hg in range(HG_B):
        sl = slice(hg * P_B, (hg + 1) * P_B)
        row = slice(hg, hg + 1)
        seg_t = acs_rows[row, :] - acs_cols[:, hg:hg + 1]
        wt = cbt * jnp.exp(jnp.minimum(seg_t, 0.0))
        xth = xt[sl, :]
        yth = jnp.dot((xth * dt_rows[row, :]).astype(BF16), wt.astype(BF16), preferred_element_type=F32)
        if use_state:
            yth = yth + inter_t[sl, :] * e_rows[row, :]
        yt_ref[sl, :] = yth
        xw_ref[sl, :] = xth * wr_rows[row, :]
    return a_last


def _ssd_kernel(*refs, nc, has_init, emit_state):
    shared = nc == 1
    pos = 3 if shared else 6
    chunk_refs = [refs[0:3], refs[0:3] if shared else refs[3:6]]
    dtr_refs = refs[pos:pos + 2]
    a_ref, bias_ref = refs[pos + 2:pos + 4]
    pos += 4
    if has_init:
        h0_ref = refs[pos]
        pos += 1
    y_ref = refs[pos]
    pos += 1
    if emit_state:
        sn_ref = refs[pos]
        pos += 1
    h_scr, yt_scr, xw_scr = refs[pos:pos + 3]

    lc = SCAN_CHUNK
    c = pl.program_id(2)
    use_state = has_init or nc > 1
    hp = HG_B * P_B

    @pl.when(c == 0)
    def _():
        if has_init:
            h_scr[...] = h0_ref[...].reshape(2, hp, D_STATE)
        else:
            h_scr[...] = jnp.zeros(h_scr.shape, F32)
        if not shared:
            y_ref[...] = jnp.zeros(y_ref.shape, F32)

    def chunk_operands(x_ref, b_ref, c_ref):
        bb = b_ref[...].astype(BF16)
        ct = c_ref[...].T.astype(BF16)
        return x_ref[...].T, bb, ct, jnp.dot(bb, ct, preferred_element_type=F32)

    ops = [chunk_operands(*chunk_refs[0])]
    ops.append(ops[0] if shared else chunk_operands(*chunk_refs[1]))
    a_last = [_ssd_direction(dirn, *ops[dirn], dtr_refs[dirn], a_ref.at[dirn], bias_ref.at[dirn], h_scr.at[dirn],
                             yt_scr.at[dirn], xw_scr.at[dirn], use_state) for dirn in range(2)]

    if shared:
        y_ref[...] = (yt_scr[0] + yt_scr[1]).T
    else:
        rows_f = pl.ds(pl.multiple_of(c * lc, lc), lc)
        rows_b = pl.ds(pl.multiple_of((nc - 1 - c) * lc, lc), lc)
        y_ref[rows_f, :] = y_ref[rows_f, :] + yt_scr[0].T
        y_ref[rows_b, :] = y_ref[rows_b, :] + yt_scr[1].T

    def update_state():
        for dirn in range(2):
            upd = jnp.dot(xw_scr[dirn].astype(BF16), ops[dirn][1], preferred_element_type=F32)
            dec = jnp.exp(a_last[dirn])
            for hg in range(HG_B):
                sl = slice(hg * P_B, (hg + 1) * P_B)
                new = upd[sl, :]
                if use_state:
                    new = new + dec[hg:hg + 1, :] * h_scr[dirn, sl, :]
                h_scr[dirn, sl, :] = new
        if emit_state:
            @pl.when(c == nc - 1)
            def _():
                sn_ref[...] = h_scr[...].reshape(2, HG_B, P_B, D_STATE)

    if emit_state:
        update_state()
    elif nc > 1:
        pl.when(c < nc - 1)(update_state)


def _ssd_scan(xbc, dtr, a_neg, dt_bias, row0, nseq, seqlen, init=None, emit_state=False):
    lc = SCAN_CHUNK
    nc = seqlen // lc
    blk0 = row0 // lc
    has_init = init is not None
    hp = HG_B * P_B
    b_off = E_B // D_STATE
    c_off = (E_B + GN_B) // D_STATE

    def blk_f(s, c):
        return blk0 + s * nc + c

    def blk_b(s, c):
        return blk0 + s * nc + nc - 1 - c

    def chunk_specs(blk):
        return [pl.BlockSpec((lc, hp), lambda s, g, c: (blk(s, c), g)),
                pl.BlockSpec((lc, D_STATE), lambda s, g, c: (blk(s, c), b_off + g)),
                pl.BlockSpec((lc, D_STATE), lambda s, g, c: (blk(s, c), c_off + g))]

    in_specs = chunk_specs(blk_f)
    args = [xbc, xbc, xbc]
    if nc > 1:
        in_specs += chunk_specs(blk_b)
        args += [xbc, xbc, xbc]
    table_spec = pl.BlockSpec((2, None, HG_B, LANES), lambda s, g, c: (0, g, 0, 0))
    in_specs += [pl.BlockSpec((None, None, HG_B, lc), lambda s, g, c: (0, g, 0, blk_f(s, c))),
                 pl.BlockSpec((None, None, HG_B, lc), lambda s, g, c: (1, g, 0, blk_b(s, c))),
                 table_spec, table_spec]
    args += [dtr, dtr, a_neg, dt_bias]
    state_spec = pl.BlockSpec((None, 2, HG_B, P_B, D_STATE), lambda s, g, c: (s, 0, g, 0, 0))
    if has_init:
        in_specs.append(state_spec)
        args.append(init)
    out_specs = [pl.BlockSpec((seqlen, hp), lambda s, g, c: (s, g))]
    out_shape = [jax.ShapeDtypeStruct((nseq * seqlen, E_B), F32)]
    if emit_state:
        out_specs.append(state_spec)
        out_shape.append(jax.ShapeDtypeStruct((nseq, 2, H_B, P_B, D_STATE), F32))
    return pl.pallas_call(
        functools.partial(_ssd_kernel, nc=nc, has_init=has_init, emit_state=emit_state),
        grid=(nseq, G_B, nc),
        in_specs=in_specs,
        out_specs=out_specs,
        out_shape=out_shape,
        scratch_shapes=[pltpu.VMEM((2, hp, D_STATE), F32), pltpu.VMEM((2, hp, lc), F32),
                        pltpu.VMEM((2, hp, lc), F32)],
        compiler_params=_params(("arbitrary",) * 3),
        name="ssd_scan_state" if emit_state else "ssd_scan_init",
    )(*args)


def _mixer_out_kernel(yp_ref, ys_ref, u_ref, z_ref, x_ref, mod_ref, nw_ref, sk_ref, w_ref, lnw_ref, lnb_ref,
                      o_ref, *, npt, grouped):
    m = pl.program_id(0)
    y = jnp.where(m < npt, yp_ref[...], ys_ref[...])
    u = u_ref[...]
    gate = _silu(z_ref[...])
    if grouped:
        parts = []
        for hd in range(H_A):
            seg = y[:, hd * DH_A:(hd + 1) * DH_A]
            mu = jnp.mean(seg, axis=-1, keepdims=True)
            sc = seg - mu
            var = jnp.mean(sc * sc, axis=-1, keepdims=True)
            parts.append(sc * lax.rsqrt(var + EPS))
        hn = jnp.concatenate(parts, axis=1) * nw_ref[...]
        t = (hn + sk_ref[...] * u) * gate
    else:
        t0 = (y + sk_ref[...] * u) * gate
        t = t0 * lax.rsqrt(jnp.mean(t0 * t0, axis=-1, keepdims=True) + EPS) * nw_ref[...]
    out = jnp.dot(t.astype(BF16), w_ref[...], preferred_element_type=F32)
    mod = mod_ref[...]
    r = ALPHA * x_ref[...] + mod[2:3, :] * out
    o_ref[...] = _layer_norm_rows(r, lnw_ref[...], lnb_ref[...])


def _mixer_out(yp, ys, u, proj, zblk, x, mod, nw, sk, w_bf, lnw, lnb, tp, ls, grouped, name):
    t = x.shape[0]
    e = w_bf.shape[0]
    npt = tp // TM
    nst = ys.shape[0] // TM
    vec_e = pl.BlockSpec((1, e), lambda m: (0, 0))
    vec_d = pl.BlockSpec((1, D_MODEL), lambda m: (0, 0))
    return pl.pallas_call(
        functools.partial(_mixer_out_kernel, npt=npt, grouped=grouped),
        grid=(t // TM,),
        in_specs=[pl.BlockSpec((TM, e), lambda m: (jnp.minimum(m, npt - 1), 0)),
                  pl.BlockSpec((TM, e), lambda m: (jnp.clip(m - npt, 0, nst - 1), 0)),
                  pl.BlockSpec((TM, e), lambda m: (m, 0)),
                  pl.BlockSpec((TM, e), lambda m: (m, zblk)),
                  pl.BlockSpec((TM, D_MODEL), lambda m: (m, 0)),
                  pl.BlockSpec((None, MOD_ROWS, D_MODEL), lambda m: (_seq_of_tile(m, TM, tp, ls), 0, 0)),
                  vec_e, vec_e,
                  pl.BlockSpec((e, D_MODEL), lambda m: (0, 0)),
                  vec_d, vec_d],
        out_specs=pl.BlockSpec((TM, D_MODEL), lambda m: (m, 0)),
        out_shape=jax.ShapeDtypeStruct((t, D_MODEL), F32),
        compiler_params=_params(("arbitrary",)),
        name=name,
    )(yp, ys, u, proj, x, mod, nw.reshape(1, e), sk.reshape(1, e), w_bf, lnw.reshape(1, D_MODEL),
      lnb.reshape(1, D_MODEL))


def _router_kernel(x_ref, mod_ref, w_ref, b_ref, hm_ref, comb_ref):
    mod = mod_ref[...]
    hm = (x_ref[...] * (1.0 + mod[4:5, :]) + mod[3:4, :]).astype(BF16)
    hm_ref[...] = hm
    logits = jnp.dot(hm, w_ref[...], preferred_element_type=F32) + b_ref[...]
    lane = lax.broadcasted_iota(jnp.int32, logits.shape, 1).astype(F32)
    big = float(2 * LANES)
    glog = jnp.where(lane < MOE_GROUPS, logits, -jnp.inf)
    ge = jnp.exp(glog - jnp.max(glog, axis=1, keepdims=True))
    prob = ge / jnp.sum(ge, axis=1, keepdims=True)
    gp = jnp.max(prob, axis=1, keepdims=True)
    gi = jnp.min(jnp.where(prob == gp, lane, big), axis=1, keepdims=True)
    lo = ROUTER_OFF + gi * MOE_EXPERTS
    ev = jnp.where((lane >= lo) & (lane < lo + MOE_EXPERTS), logits, -jnp.inf)
    e1 = jnp.max(ev, axis=1, keepdims=True)
    i1 = jnp.min(jnp.where(ev == e1, lane, big), axis=1, keepdims=True)
    ev2 = jnp.where(lane == i1, -jnp.inf, ev)
    e2 = jnp.max(ev2, axis=1, keepdims=True)
    i2 = jnp.min(jnp.where(ev2 == e2, lane, big), axis=1, keepdims=True)
    t2 = jnp.exp(e2 - e1)
    w1 = 1.0 / (1.0 + t2)
    w2 = t2 / (1.0 + t2)
    comb_ref[...] = jnp.where(lane == i1, w1 * gp, jnp.where(lane == i2, w2 * gp, 0.0))


def _router(x, mod, w_r, b_r, tp, ls):
    t = x.shape[0]
    tm = TM_MOE
    return pl.pallas_call(
        _router_kernel,
        grid=(t // tm,),
        in_specs=[pl.BlockSpec((tm, D_MODEL), lambda m: (m, 0)),
                  pl.BlockSpec((None, MOD_ROWS, D_MODEL), lambda m: (_seq_of_tile(m, tm, tp, ls), 0, 0)),
                  pl.BlockSpec((D_MODEL, LANES), lambda m: (0, 0)),
                  pl.BlockSpec((1, LANES), lambda m: (0, 0))],
        out_specs=[pl.BlockSpec((tm, D_MODEL), lambda m: (m, 0)),
                   pl.BlockSpec((tm, LANES), lambda m: (m, 0))],
        out_shape=[jax.ShapeDtypeStruct((t, D_MODEL), BF16), jax.ShapeDtypeStruct((t, LANES), F32)],
        compiler_params=_params(("arbitrary",)),
        name="moe_router",
    )(x, mod, w_r, b_r)


def _moe_kernel(hm_ref, comb_ref, wg_ref, wu_ref, wd_ref, x_ref, mod_ref, lnw_ref, lnb_ref, o_ref):
    hm = hm_ref[...]
    comb = comb_ref[...]
    acc = None
    for g in range(MOE_GROUPS):
        a = _silu(jnp.dot(hm, wg_ref[g], preferred_element_type=F32)) * jnp.dot(
            hm, wu_ref[g], preferred_element_type=F32)
        parts = []
        for e in range(MOE_EXPERTS):
            lane = ROUTER_OFF + g * MOE_EXPERTS + e
            parts.append(a[:, e * D_FF_E:(e + 1) * D_FF_E] * comb[:, lane:lane + 1])
        part = jnp.dot(jnp.concatenate(parts, axis=1).astype(BF16), wd_ref[g], preferred_element_type=F32)
        acc = part if acc is None else acc + part
    mod = mod_ref[...]
    r = ALPHA * x_ref[...] + mod[5:6, :] * acc
    o_ref[...] = _layer_norm_rows(r, lnw_ref[...], lnb_ref[...])


def _moe(hm, comb, wg, wu, wd, x, mod, lnw, lnb, tp, ls):
    t = x.shape[0]
    tm = TM_MOE
    gf = MOE_EXPERTS * D_FF_E
    vec_d = pl.BlockSpec((1, D_MODEL), lambda m: (0, 0))
    return pl.pallas_call(
        _moe_kernel,
        grid=(t // tm,),
        in_specs=[pl.BlockSpec((tm, D_MODEL), lambda m: (m, 0)),
                  pl.BlockSpec((tm, LANES), lambda m: (m, 0)),
                  _resident((MOE_GROUPS, D_MODEL, gf)),
                  _resident((MOE_GROUPS, D_MODEL, gf)),
                  _resident((MOE_GROUPS, gf, D_MODEL)),
                  pl.BlockSpec((tm, D_MODEL), lambda m: (m, 0)),
                  pl.BlockSpec((None, MOD_ROWS, D_MODEL), lambda m: (_seq_of_tile(m, tm, tp, ls), 0, 0)),
                  vec_d, vec_d],
        out_specs=pl.BlockSpec((tm, D_MODEL), lambda m: (m, 0)),
        out_shape=jax.ShapeDtypeStruct((t, D_MODEL), F32),
        compiler_params=_params(("arbitrary",)),
        name="moe_experts",
    )(hm, comb, wg, wu, wd, x, mod, lnw.reshape(1, D_MODEL), lnb.reshape(1, D_MODEL))


def _moe_layer(x, mod, w_rg, b_rg, w_re, b_re, w_gate, w_up, w_down, lnw, lnb, tp, ls):
    w_r = jnp.concatenate([w_rg, jnp.moveaxis(w_re, 0, 1).reshape(D_MODEL, N_EXPERTS)], axis=1)
    w_r = jnp.pad(w_r, ((0, 0), (0, LANES - w_r.shape[1]))).astype(BF16)
    b_r = jnp.pad(jnp.concatenate([b_rg, b_re.reshape(-1)]), (0, LANES - MOE_GROUPS - N_EXPERTS)).reshape(1, LANES)
    hm, comb = _router(x, mod, w_r, b_r, tp, ls)
    gf = MOE_EXPERTS * D_FF_E
    wg = w_gate.transpose(0, 2, 1, 3).reshape(MOE_GROUPS, D_MODEL, gf).astype(BF16)
    wu = w_up.transpose(0, 2, 1, 3).reshape(MOE_GROUPS, D_MODEL, gf).astype(BF16)
    wd = w_down.reshape(MOE_GROUPS, gf, D_MODEL).astype(BF16)
    return _moe(hm, comb, wg, wu, wd, x, mod, lnw, lnb, tp, ls)


def _blockdiag_lanes(w):
    per = LANES // QKV_BLOCK
    wb = w.reshape(-1, per, QKV_BLOCK, QKV_BLOCK)
    eye = jnp.eye(per, dtype=w.dtype)
    return jnp.einsum('bncd,nm->bncmd', wb, eye).reshape(-1, LANES, LANES)


def _grid_to_cols(x):
    b, l, dd = x.shape
    return x.reshape(b, l // GRID_W, GRID_W, dd).transpose(0, 2, 1, 3).reshape(b, l, dd)


def _cols_to_grid(x):
    b, l, dd = x.shape
    return x.reshape(b, GRID_W, l // GRID_W, dd).transpose(0, 2, 1, 3).reshape(b, l, dd)


def kernel(x_prompt, x_sample, state_mlstm_C, state_mlstm_n, state_mlstm_m, state_ssd, c, c_ctx, ada_w, ada_b, ln_w, ln_b, a_w_in, a_conv_w, a_conv_b, a_w_q, a_w_k, a_w_v, a_w_gates, a_b_gates, a_norm_w, a_skip, a_w_down, b_w_in, b_conv_w, b_conv_b, b_dt_bias, b_a_log, b_d_skip, b_norm_w, b_w_out, moe_w_rg, moe_b_rg, moe_w_re, moe_b_re, moe_w_gate, moe_w_up, moe_w_down):
    bp, lp, _ = x_prompt.shape
    bs, ls, _ = x_sample.shape
    tp, ts = bp * lp, bs * ls
    assert lp % SCAN_CHUNK == 0 and ls % SCAN_CHUNK == 0 and tp % 512 == 0 and ls % 512 == 0
    assert bs + 1 <= N_MOD_SEQ

    cvec = jnp.concatenate([c_ctx[None, :], c, jnp.zeros((N_MOD_SEQ - 1 - bs, D_MODEL), F32)], axis=0)
    mods = _ada(cvec, ada_w, ada_b).reshape(DEPTH, N_MOD_SEQ, 6, D_MODEL)
    mods = jnp.pad(mods, ((0, 0), (0, 0), (0, MOD_ROWS - 6), (0, 0)))

    x = jnp.concatenate([x_prompt.reshape(tp, D_MODEL), x_sample.reshape(ts, D_MODEL)], axis=0)

    mod = mods[0]
    proj = _modmm(x, mod, a_w_in[0].astype(BF16), tp, ls, 0, 1, "mlstm_in_proj")
    xc = _conv_silu(proj, 0, E_A, a_conv_w[0], a_conv_b[0], tp, lp, ls, "mlstm_conv")
    wqk = jnp.concatenate([_blockdiag_lanes(a_w_q[0]), _blockdiag_lanes(a_w_k[0])], axis=-1).astype(BF16)
    wv = _blockdiag_lanes(a_w_v[0]).astype(BF16)
    ngate = 4 * H_A
    wg = jnp.pad(a_w_gates[0].reshape(3, E_A, ngate), ((0, 0), (0, 0), (0, LANES - ngate))).astype(BF16)
    bg = jnp.pad(a_b_gates[0], (0, LANES - ngate)).reshape(1, LANES)
    q, k, v, g = _qkv_gates(xc, proj, wqk, wv, wg, bg)
    gt = g[:, :ngate].T
    hs_p, c_new, n_new, m_new = _mlstm_scan(q, k, v, gt, 0, bp, lp, emit_state=True)
    (hs_s,) = _mlstm_scan(q, k, v, gt, tp, bs, ls,
                          init=(state_mlstm_C[:, 0], state_mlstm_n[:, 0], state_mlstm_m[:, 0]))
    x = _mixer_out(hs_p, hs_s, xc, proj, 1, x, mod, a_norm_w[0], a_skip[0], a_w_down[0].astype(BF16),
                   ln_w[0, 0], ln_b[0, 0], tp, ls, True, "mlstm_out")
    x = _moe_layer(x, mod, moe_w_rg[0], moe_b_rg[0], moe_w_re[0], moe_b_re[0], moe_w_gate[0], moe_w_up[0],
                   moe_w_down[0], ln_w[0, 1], ln_b[0, 1], tp, ls)

    mod = mods[1]
    x = jnp.concatenate([x[:tp], _grid_to_cols(x[tp:].reshape(bs, ls, D_MODEL)).reshape(ts, D_MODEL)], axis=0)
    n_main = 2 * E_B + 2 * GN_B
    w_in = jnp.pad(b_w_in[0], ((0, 0), (0, -b_w_in.shape[2] % LANES))).astype(BF16)
    proj = _modmm(x, mod, w_in, tp, ls, 0, 1, "ssd_in_proj")
    dt = proj[:, n_main:n_main + 2 * H_B]
    xbc = _conv_silu(proj, E_B, E_B + 2 * GN_B, b_conv_w[0], b_conv_b[0], tp, lp, ls, "ssd_conv")
    lane_bcast = (2, G_B, HG_B, LANES)
    a_neg = jnp.broadcast_to(-jnp.exp(b_a_log[0]).reshape(2, G_B, HG_B, 1), lane_bcast)
    dt_bias = jnp.broadcast_to(b_dt_bias[0].reshape(2, G_B, HG_B, 1), lane_bcast)
    dtr = dt.T.reshape(2, G_B, HG_B, tp + ts)
    y_p, s_new = _ssd_scan(xbc, dtr, a_neg, dt_bias, 0, bp, lp, emit_state=True)
    (y_s,) = _ssd_scan(xbc, dtr, a_neg, dt_bias, tp, bs, ls, init=state_ssd[:, 0])
    x = _mixer_out(y_p, y_s, xbc, proj, 0, x, mod, b_norm_w[0], jnp.repeat(b_d_skip[0], P_B),
                   b_w_out[0].astype(BF16), ln_w[1, 0], ln_b[1, 0], tp, ls, False, "ssd_out")
    x = _moe_layer(x, mod, moe_w_rg[1], moe_b_rg[1], moe_w_re[1], moe_b_re[1], moe_w_gate[1], moe_w_up[1],
                   moe_w_down[1], ln_w[1, 1], ln_b[1, 1], tp, ls)

    y_prompt = x[:tp].reshape(bp, lp, D_MODEL)
    y_sample = _cols_to_grid(x[tp:].reshape(bs, ls, D_MODEL))
    return (y_prompt, y_sample,
            c_new[:, None],
            n_new.reshape(bp, 1, 2, H_A, DH_A),
            m_new[:, :, :, 0, 0][:, None],
            s_new[:, None])
```

```python
import functools
import math

import jax
import jax.numpy as jnp
from jax import lax
from jax.experimental import pallas as pl
from jax.experimental.pallas import tpu as pltpu

F32 = jnp.float32
BF16 = jnp.bfloat16

D_MODEL = 1024
DEPTH = 2
GRID_W = 64
CONV_W = 5
E_A = 2 * D_MODEL
H_A = 4
DH_A = E_A // H_A
QKV_BLOCK = 4
E_B = 2 * D_MODEL
P_B = 64
H_B = E_B // P_B
G_B = 4
HG_B = H_B // G_B
D_STATE = 128
GN_B = G_B * D_STATE
MOE_GROUPS = 4
MOE_EXPERTS = 4
N_EXPERTS = MOE_GROUPS * MOE_EXPERTS
D_FF_E = 256
ALPHA = (2.0 * DEPTH) ** 0.25
EPS = 1e-5

LANES = 128
SUBLANES = 8
BF16_ROWS = 16
TM = 256
TM_MOE = 512
SCAN_CHUNK = 256
MOD_ROWS = 8
N_MOD_SEQ = 16
VMEM_LIMIT = 56 * 1024 * 1024
ROUTER_OFF = MOE_GROUPS


def _params(sem):
    return pltpu.CompilerParams(dimension_semantics=sem, vmem_limit_bytes=VMEM_LIMIT)


def _seq_of_tile(m, tm, tp, ls):
    npt = tp // tm
    return jnp.where(m < npt, 0, 1 + (m - npt) // (ls // tm))


def _silu(x):
    return x * jax.nn.sigmoid(x)


def _softplus(x):
    return jnp.maximum(x, 0.0) + jnp.log1p(jnp.exp(-jnp.abs(x)))


def _layer_norm_rows(r, w, b):
    mu = jnp.mean(r, axis=-1, keepdims=True)
    rc = r - mu
    var = jnp.mean(rc * rc, axis=-1, keepdims=True)
    return rc * lax.rsqrt(var + EPS) * w + b


def _ada_kernel(c_ref, w_ref, b_ref, o_ref):
    s = _silu(c_ref[...])
    o_ref[...] = jnp.dot(s.astype(BF16), w_ref[...].astype(BF16), preferred_element_type=F32) + b_ref[...]


def _ada(cvec, ada_w, ada_b):
    nl = ada_w.shape[0]
    n_out = ada_w.shape[2]
    tn = 1536
    return pl.pallas_call(
        _ada_kernel,
        grid=(nl, n_out // tn),
        in_specs=[pl.BlockSpec((N_MOD_SEQ, D_MODEL), lambda l, n: (0, 0)),
                  pl.BlockSpec((None, D_MODEL, tn), lambda l, n: (l, 0, n)),
                  pl.BlockSpec((None, 1, tn), lambda l, n: (l, 0, n))],
        out_specs=pl.BlockSpec((None, N_MOD_SEQ, tn), lambda l, n: (l, 0, n)),
        out_shape=jax.ShapeDtypeStruct((nl, N_MOD_SEQ, n_out), F32),
        compiler_params=_params(("arbitrary", "arbitrary")),
        name="ada_mod",
    )(cvec, ada_w, ada_b.reshape(nl, 1, n_out))


def _token_tile_specs(src, tm, tp, width=D_MODEL):
    pa, p0, sa, s0 = src
    npt = tp // tm
    nst = (sa.shape[0] - s0) // tm
    return ([pl.BlockSpec((tm, width), lambda m, *_: (p0 // tm + jnp.minimum(m, npt - 1), 0)),
             pl.BlockSpec((tm, width), lambda m, *_: (s0 // tm + jnp.clip(m - npt, 0, nst - 1), 0))], [pa, sa])


def _select_token_tile(p_ref, s_ref, tm, tp):
    return jnp.where(pl.program_id(0) < tp // tm, p_ref[...], s_ref[...])


def _modmm_kernel(xp_ref, xs_ref, mod_ref, w_ref, o_ref, *tail_ref, tm, tp, shift_row, scale_row):
    mod = mod_ref[...]
    x = _select_token_tile(xp_ref, xs_ref, tm, tp)
    h = x * (1.0 + mod[scale_row:scale_row + 1, :]) + mod[shift_row:shift_row + 1, :]
    res = jnp.dot(h.astype(BF16), w_ref[...], preferred_element_type=F32)
    n_main = o_ref.shape[1]
    o_ref[...] = res[:, :n_main].astype(o_ref.dtype)
    if tail_ref:
        tail_ref[0][...] = res[:, n_main:]


def _resident(shape):
    return pl.BlockSpec(shape, lambda *_: (0,) * len(shape), pipeline_mode=pl.Buffered(1))


def _modmm(src, t, mod, w_bf, n_tail, tp, ls, shift_row, scale_row, name):
    n_main = w_bf.shape[1] - n_tail
    tm = TM
    x_specs, x_args = _token_tile_specs(src, tm, tp)
    out_specs = [pl.BlockSpec((tm, n_main), lambda m: (m, 0))]
    out_shape = [jax.ShapeDtypeStruct((t, n_main), BF16)]
    if n_tail:
        out_specs.append(pl.BlockSpec((tm, n_tail), lambda m: (m, 0)))
        out_shape.append(jax.ShapeDtypeStruct((t, n_tail), F32))
    return pl.pallas_call(
        functools.partial(_modmm_kernel, tm=tm, tp=tp, shift_row=shift_row, scale_row=scale_row),
        grid=(t // tm,),
        in_specs=x_specs + [
            pl.BlockSpec((None, MOD_ROWS, D_MODEL), lambda m: (_seq_of_tile(m, tm, tp, ls), 0, 0)),
            _resident((D_MODEL, w_bf.shape[1]))],
        out_specs=out_specs,
        out_shape=out_shape,
        compiler_params=_params(("arbitrary",)),
        name=name,
    )(*x_args, mod, w_bf)


def _conv_kernel(cur_ref, prev_ref, next_ref, w_ref, b_ref, o_ref, ext_ref, *, tm, tp, lp, ls):
    m = pl.program_id(0)
    npt = tp // tm
    pos = jnp.where(m < npt, m % (lp // tm), (m - npt) % (ls // tm))
    last_pos = jnp.where(m < npt, lp // tm - 1, ls // tm - 1)
    halo = SUBLANES
    ext_ref[0:halo, :] = jnp.where(pos == 0, 0.0, prev_ref[...].astype(F32)[BF16_ROWS - halo:, :])
    ext_ref[halo:halo + tm, :] = cur_ref[...].astype(F32)
    ext_ref[halo + tm:2 * halo + tm, :] = jnp.where(pos == last_pos, 0.0, next_ref[...].astype(F32)[:halo, :])
    acc = jnp.zeros(o_ref.shape, F32) + b_ref[...]
    for j in range(CONV_W):
        acc = acc + ext_ref[pl.ds(halo - CONV_W // 2 + j, tm), :] * w_ref[j:j + 1, :]
    o_ref[...] = _silu(acc).astype(o_ref.dtype)


def _conv_silu(a, col0, width, w, b, tp, lp, ls, name):
    t = a.shape[0]
    tm, tc = TM, 1024
    cb0 = col0 // tc
    nhalo = t // BF16_ROWS
    per = tm // BF16_ROWS
    return pl.pallas_call(
        functools.partial(_conv_kernel, tm=tm, tp=tp, lp=lp, ls=ls),
        grid=(t // tm, width // tc),
        in_specs=[pl.BlockSpec((tm, tc), lambda m, j: (m, cb0 + j)),
                  pl.BlockSpec((BF16_ROWS, tc), lambda m, j: (jnp.maximum(m * per - 1, 0), cb0 + j)),
                  pl.BlockSpec((BF16_ROWS, tc), lambda m, j: (jnp.minimum((m + 1) * per, nhalo - 1), cb0 + j)),
                  pl.BlockSpec((CONV_W, tc), lambda m, j: (0, j)),
                  pl.BlockSpec((1, tc), lambda m, j: (0, j))],
        out_specs=pl.BlockSpec((tm, tc), lambda m, j: (m, j)),
        out_shape=jax.ShapeDtypeStruct((t, width), BF16),
        scratch_shapes=[pltpu.VMEM((tm + 2 * SUBLANES, tc), F32)],
        compiler_params=_params(("arbitrary", "arbitrary")),
        name=name,
    )(a, a, a, w, b.reshape(1, width))


def _qkv_kernel(xc_ref, xu_ref, wqk_ref, wv_ref, wg_ref, bg_ref, q_ref, k_ref, v_ref, g_ref):
    nblk = E_A // LANES
    g = jnp.zeros(g_ref.shape, F32) + bg_ref[...]
    for b in range(nblk):
        sl = slice(b * LANES, (b + 1) * LANES)
        xcb = xc_ref[:, sl]
        xub = xu_ref[:, sl]
        qk = jnp.dot(xcb, wqk_ref[b], preferred_element_type=F32)
        vv = jnp.dot(xub, wv_ref[b], preferred_element_type=F32)
        qb = qk[:, :LANES].astype(BF16)
        kb = qk[:, LANES:].astype(BF16)
        vb = vv.astype(BF16)
        q_ref[:, sl] = qb
        k_ref[:, sl] = kb
        v_ref[:, sl] = vb
        g = g + jnp.dot(qb, wg_ref[0, sl, :], preferred_element_type=F32)
        g = g + jnp.dot(kb, wg_ref[1, sl, :], preferred_element_type=F32)
        g = g + jnp.dot(vb, wg_ref[2, sl, :], preferred_element_type=F32)
    lane = lax.broadcasted_iota(jnp.int32, g.shape, 1)
    is_forget = (lane & (2 * H_A - 1)) >= H_A
    log_sig = jnp.minimum(g, 0.0) - jnp.log1p(jnp.exp(-jnp.abs(g)))
    g_ref[...] = jnp.where(is_forget, log_sig, g)


def _qkv_gates(xc, proj, wqk, wv, wg, bg):
    t = xc.shape[0]
    nblk = E_A // LANES
    tile = pl.BlockSpec((TM, E_A), lambda m: (m, 0))
    return pl.pallas_call(
        _qkv_kernel,
        grid=(t // TM,),
        in_specs=[tile, tile,
                  pl.BlockSpec((nblk, LANES, 2 * LANES), lambda m: (0, 0, 0)),
                  pl.BlockSpec((nblk, LANES, LANES), lambda m: (0, 0, 0)),
                  pl.BlockSpec((3, E_A, LANES), lambda m: (0, 0, 0)),
                  pl.BlockSpec((1, LANES), lambda m: (0, 0))],
        out_specs=[tile, tile, tile, pl.BlockSpec((TM, LANES), lambda m: (m, 0))],
        out_shape=[jax.ShapeDtypeStruct((t, E_A), BF16)] * 3 + [jax.ShapeDtypeStruct((t, LANES), F32)],
        compiler_params=_params(("arbitrary",)),
        name="mlstm_qkv_gates",
    )(xc, proj, wqk, wv, wg, bg)


def _order_mask(dirn, n, rows_are_later):
    a_io = lax.broadcasted_iota(jnp.int32, (n, n), 0)
    b_io = lax.broadcasted_iota(jnp.int32, (n, n), 1)
    if (dirn == 0) == rows_are_later:
        return b_io <= a_io
    return b_io >= a_io


def _exact_cumsum_rows(rows, tri):
    nrow = rows.shape[0]
    p1 = rows.astype(BF16).astype(F32)
    r1 = rows - p1
    p2 = r1.astype(BF16).astype(F32)
    p3 = r1 - p2
    pieces = jnp.concatenate([p1, p2, p3, jnp.zeros_like(p1)], axis=0).astype(BF16)
    cs = jnp.dot(pieces, tri, preferred_element_type=F32)
    return cs[0:nrow] + cs[nrow:2 * nrow] + cs[2 * nrow:3 * nrow]


def _pick_row(blk, idx):
    sub = lax.broadcasted_iota(jnp.int32, (blk.shape[0], 1), 0)
    return jnp.sum(jnp.where(sub == idx, blk, 0.0), axis=0, keepdims=True)


def _rows_to_cols(rows):
    pad = jnp.zeros((LANES - rows.shape[0], rows.shape[1]), F32)
    return jnp.concatenate([rows, pad], axis=0).T


def _pick_col(blk, idx):
    lane = lax.broadcasted_iota(jnp.int32, (1, blk.shape[1]), 1)
    return jnp.sum(jnp.where(lane == idx, blk, 0.0), axis=1, keepdims=True)


def _mlstm_direction(dirn, h, q, k, v, s_qk, kt, gt_ref, c_ref, n_ref, m_ref, use_state, need_update):
    lc = SCAN_CHUNK
    scale = DH_A ** -0.5
    log_scale = -0.5 * math.log(DH_A)
    gates = gt_ref[dirn * 2 * H_A:(dirn + 1) * 2 * H_A, :]
    tri = jnp.where(_order_mask(dirn, lc, False), 1.0, 0.0).astype(BF16)
    csum = _exact_cumsum_rows(gates, tri)
    i_row = _pick_row(gates, h)
    f_row = _pick_row(gates, H_A + h)
    b_row = _pick_row(csum, H_A + h)
    v_row = i_row - b_row
    valid = _order_mask(dirn, lc, True)
    m_prev = m_ref[:, 0:1]
    run_max = jnp.max(jnp.where(valid, v_row, -jnp.inf), axis=1, keepdims=True)
    u_col = -jnp.maximum(m_prev, run_max)
    p = s_qk * jnp.exp(jnp.where(valid, u_col + (v_row + log_scale), -jnp.inf))
    num = jnp.dot(p.astype(BF16), v, preferred_element_type=F32)
    den = jnp.sum(p, axis=1, keepdims=True)
    if use_state:
        w_inter = jnp.exp(m_prev + u_col) * scale
        qc = jnp.dot(q, c_ref[...].astype(BF16), preferred_element_type=F32)
        qn = jnp.sum(q.astype(F32) * n_ref[...], axis=1, keepdims=True)
        num = num + w_inter * qc
        den = den + w_inter * qn
    b_col = _rows_to_cols(b_row)[:, 0:1]
    h_out = num / jnp.maximum(jnp.abs(den), jnp.exp(u_col - b_col))
    if not need_update:
        return h_out, None
    b_last = jnp.sum(f_row, axis=1, keepdims=True)
    logk = b_last + v_row
    m_new = jnp.maximum(b_last + m_prev, jnp.max(logk, axis=1, keepdims=True))
    wk = jnp.exp(logk - m_new)
    c_new = jnp.dot((kt * wk).astype(BF16), v, preferred_element_type=F32)
    wk8 = (jnp.zeros((SUBLANES, lc), F32) + wk).astype(BF16)
    n_new = jnp.dot(wk8, k, preferred_element_type=F32)[0:1, :]
    return h_out, (c_new, n_new, m_new, jnp.exp(b_last + m_prev - m_new))


def _mlstm_kernel(*refs, nc, has_init, emit_state):
    shared = nc == 1
    pos = 4 if shared else 8
    chunk_refs = [refs[0:4], refs[0:4] if shared else refs[4:8]]
    if has_init:
        c0_ref, n0_ref, m0_ref = refs[pos:pos + 3]
        pos += 3
    hs_ref = refs[pos]
    pos += 1
    if emit_state:
        cn_ref, nn_ref, mn_ref = refs[pos:pos + 3]
        pos += 3
    c_scr, n_scr, m_scr = refs[pos:pos + 3]

    lc = SCAN_CHUNK
    s_id, h, c = (pl.program_id(i) for i in range(3))
    use_state = has_init or nc > 1

    @pl.when(c == 0)
    def _():
        for dirn in range(2):
            if has_init:
                c_scr[dirn] = c0_ref[dirn]
                n_scr[dirn] = _pick_row(n0_ref[dirn], h)
                m_scr[dirn] = jnp.zeros(m_scr.shape[1:], F32) + m0_ref[(s_id * 2 + dirn) * H_A + h]
            else:
                c_scr[dirn] = jnp.zeros(c_scr.shape[1:], F32)
                n_scr[dirn] = jnp.zeros(n_scr.shape[1:], F32)
                m_scr[dirn] = jnp.zeros(m_scr.shape[1:], F32)
        if not shared:
            hs_ref[...] = jnp.zeros(hs_ref.shape, F32)

    def chunk_operands(q_ref, k_ref, v_ref, gt_ref):
        q = q_ref[...]
        k = k_ref[...]
        s_qk = lax.dot_general(q, k, (((1,), (1,)), ((), ())), preferred_element_type=F32)
        return q, k, v_ref[...], s_qk, k.astype(F32).T, gt_ref

    ops = [chunk_operands(*chunk_refs[0])]
    ops.append(ops[0] if shared else chunk_operands(*chunk_refs[1]))

    def run(need_update):
        return [_mlstm_direction(dirn, h, *ops[dirn], c_scr.at[dirn], n_scr.at[dirn], m_scr.at[dirn],
                                 use_state, need_update) for dirn in range(2)]

    def write_out(res):
        if shared:
            hs_ref[...] = res[0][0] + res[1][0]
        else:
            rows_f = pl.ds(pl.multiple_of(c * lc, lc), lc)
            rows_b = pl.ds(pl.multiple_of((nc - 1 - c) * lc, lc), lc)
            hs_ref[rows_f, :] = hs_ref[rows_f, :] + res[0][0]
            hs_ref[rows_b, :] = hs_ref[rows_b, :] + res[1][0]

    def with_update():
        res = run(True)
        write_out(res)
        for dirn in range(2):
            c_new, n_new, m_new, decay = res[dirn][1]
            if use_state:
                c_new = c_new + decay * c_scr[dirn]
                n_new = n_new + decay * n_scr[dirn]
            m_new_row = jnp.zeros(m_scr.shape[1:], F32) + m_new
            if nc > 1:
                c_scr[dirn] = c_new
                n_scr[dirn] = n_new
                m_scr[dirn] = m_new_row
            if emit_state:
                @pl.when(c == nc - 1)
                def _():
                    cn_ref[dirn] = c_new
                    nn_ref[dirn] = n_new
                    mn_ref[dirn] = m_new_row

    def without_update():
        write_out(run(False))

    if emit_state:
        with_update()
    else:
        pl.when(c < nc - 1)(with_update)
        pl.when(c == nc - 1)(without_update)


def _mlstm_scan(q, k, v, gt, row0, nseq, seqlen, init=None, emit_state=False):
    lc = SCAN_CHUNK
    nc = seqlen // lc
    blk0 = row0 // lc
    has_init = init is not None

    def blk_f(s, c):
        return blk0 + s * nc + c

    def blk_b(s, c):
        return blk0 + s * nc + nc - 1 - c

    def chunk_specs(blk):
        return [pl.BlockSpec((lc, DH_A), lambda s, h, c: (blk(s, c), h))] * 3 + [
            pl.BlockSpec((4 * H_A, lc), lambda s, h, c: (0, blk(s, c)))]

    in_specs = chunk_specs(blk_f)
    args = [q, k, v, gt]
    if nc > 1:
        in_specs += chunk_specs(blk_b)
        args += [q, k, v, gt]
    if has_init:
        c0, n0, m0 = init
        in_specs += [pl.BlockSpec((None, 2, None, DH_A, DH_A), lambda s, h, c: (s, 0, h, 0, 0)),
                     pl.BlockSpec((None, 2, H_A, DH_A), lambda s, h, c: (s, 0, 0, 0)),
                     pl.BlockSpec(memory_space=pltpu.SMEM)]
        args += [c0, n0, m0.reshape(-1)]
    out_specs = [pl.BlockSpec((seqlen, DH_A), lambda s, h, c: (s, h))]
    out_shape = [jax.ShapeDtypeStruct((nseq * seqlen, E_A), F32)]
    if emit_state:
        out_specs += [pl.BlockSpec((None, 2, None, DH_A, DH_A), lambda s, h, c: (s, 0, h, 0, 0)),
                      pl.BlockSpec((None, 2, None, 1, DH_A), lambda s, h, c: (s, 0, h, 0, 0)),
                      pl.BlockSpec((None, 2, None, 1, LANES), lambda s, h, c: (s, 0, h, 0, 0))]
        out_shape += [jax.ShapeDtypeStruct((nseq, 2, H_A, DH_A, DH_A), F32),
                      jax.ShapeDtypeStruct((nseq, 2, H_A, 1, DH_A), F32),
                      jax.ShapeDtypeStruct((nseq, 2, H_A, 1, LANES), F32)]
    return pl.pallas_call(
        functools.partial(_mlstm_kernel, nc=nc, has_init=has_init, emit_state=emit_state),
        grid=(nseq, H_A, nc),
        in_specs=in_specs,
        out_specs=out_specs,
        out_shape=out_shape,
        scratch_shapes=[pltpu.VMEM((2, DH_A, DH_A), F32), pltpu.VMEM((2, 1, DH_A), F32),
                        pltpu.VMEM((2, 1, LANES), F32)],
        compiler_params=_params(("arbitrary",) * 3),
        name="mlstm_scan_state" if emit_state else "mlstm_scan_init",
    )(*args)


def _ssd_direction(dirn, xt, bb, ct, cbt_raw, dtr_ref, a_ref, bias_ref, h_ref, yt_ref, xw_ref, use_state):
    lc = SCAN_CHUNK
    r_io = lax.broadcasted_iota(jnp.int32, (lc, lc), 0)
    s_io = lax.broadcasted_iota(jnp.int32, (lc, lc), 1)
    valid = (r_io <= s_io) if dirn == 0 else (r_io >= s_io)
    tri = jnp.where(valid, 1.0, 0.0).astype(BF16)

    dt_rows = _softplus(dtr_ref[...] + bias_ref[:, 0:1])
    a_rows = dt_rows * a_ref[:, 0:1]
    a1 = a_rows.astype(BF16).astype(F32)
    r1 = a_rows - a1
    a2 = r1.astype(BF16).astype(F32)
    a3 = r1 - a2
    pieces = jnp.concatenate([a1, a2, a3, jnp.zeros_like(a1)], axis=0).astype(BF16)
    cs = jnp.dot(pieces, tri, preferred_element_type=F32)
    acs_rows = cs[0:HG_B] + cs[HG_B:2 * HG_B] + cs[2 * HG_B:3 * HG_B]
    a_last = jnp.sum(a_rows, axis=1, keepdims=True)
    wr_rows = jnp.exp(a_last - acs_rows) * dt_rows
    acs_cols = jnp.concatenate([acs_rows, jnp.zeros((LANES - HG_B, lc), F32)], axis=0).T

    cbt = jnp.where(valid, cbt_raw, 0.0)
    if use_state:
        inter_t = jnp.dot(h_ref[...].astype(BF16), ct, preferred_element_type=F32)
        e_rows = jnp.exp(acs_rows)
    for hg in range(HG_B):
        sl = slice(hg * P_B, (hg + 1) * P_B)
        row = slice(hg, hg + 1)
        seg_t = acs_rows[row, :] - acs_cols[:, hg:hg + 1]
        wt = cbt * jnp.exp(jnp.minimum(seg_t, 0.0))
        xth = xt[sl, :]
        yth = jnp.dot((xth * dt_rows[row, :]).astype(BF16), wt.astype(BF16), preferred_element_type=F32)
        if use_state:
            yth = yth + inter_t[sl, :] * e_rows[row, :]
        yt_ref[sl, :] = yth
        xw_ref[sl, :] = xth * wr_rows[row, :]
    return a_last


def _ssd_kernel(*refs, nc, has_init, emit_state):
    shared = nc == 1
    pos = 3 if shared else 6
    chunk_refs = [refs[0:3], refs[0:3] if shared else refs[3:6]]
    dtr_refs = refs[pos:pos + 2]
    a_ref, bias_ref = refs[pos + 2:pos + 4]
    pos += 4
    if has_init:
        h0_ref = refs[pos]
        pos += 1
    y_ref = refs[pos]
    pos += 1
    if emit_state:
        sn_ref = refs[pos]
        pos += 1
    h_scr, yt_scr, xw_scr = refs[pos:pos + 3]

    lc = SCAN_CHUNK
    c = pl.program_id(2)
    use_state = has_init or nc > 1
    hp = HG_B * P_B

    @pl.when(c == 0)
    def _():
        if has_init:
            h_scr[...] = h0_ref[...].reshape(2, hp, D_STATE)
        else:
            h_scr[...] = jnp.zeros(h_scr.shape, F32)
        if not shared:
            y_ref[...] = jnp.zeros(y_ref.shape, F32)

    def chunk_operands(x_ref, b_ref, c_ref):
        bb = b_ref[...]
        ct = c_ref[...].astype(F32).T.astype(BF16)
        return x_ref[...].astype(F32).T, bb, ct, jnp.dot(bb, ct, preferred_element_type=F32)

    ops = [chunk_operands(*chunk_refs[0])]
    ops.append(ops[0] if shared else chunk_operands(*chunk_refs[1]))
    a_last = [_ssd_direction(dirn, *ops[dirn], dtr_refs[dirn], a_ref.at[dirn], bias_ref.at[dirn], h_scr.at[dirn],
                             yt_scr.at[dirn], xw_scr.at[dirn], use_state) for dirn in range(2)]

    if shared:
        y_ref[...] = (yt_scr[0] + yt_scr[1]).T
    else:
        rows_f = pl.ds(pl.multiple_of(c * lc, lc), lc)
        rows_b = pl.ds(pl.multiple_of((nc - 1 - c) * lc, lc), lc)
        y_ref[rows_f, :] = y_ref[rows_f, :] + yt_scr[0].T
        y_ref[rows_b, :] = y_ref[rows_b, :] + yt_scr[1].T

    def update_state():
        for dirn in range(2):
            upd = jnp.dot(xw_scr[dirn].astype(BF16), ops[dirn][1], preferred_element_type=F32)
            dec = jnp.exp(a_last[dirn])
            for hg in range(HG_B):
                sl = slice(hg * P_B, (hg + 1) * P_B)
                new = upd[sl, :]
                if use_state:
                    new = new + dec[hg:hg + 1, :] * h_scr[dirn, sl, :]
                h_scr[dirn, sl, :] = new
        if emit_state:
            @pl.when(c == nc - 1)
            def _():
                sn_ref[...] = h_scr[...].reshape(2, HG_B, P_B, D_STATE)

    if emit_state:
        update_state()
    elif nc > 1:
        pl.when(c < nc - 1)(update_state)


def _ssd_scan(xbc, dtr, a_neg, dt_bias, row0, nseq, seqlen, init=None, emit_state=False):
    lc = SCAN_CHUNK
    nc = seqlen // lc
    blk0 = row0 // lc
    has_init = init is not None
    hp = HG_B * P_B
    b_off = E_B // D_STATE
    c_off = (E_B + GN_B) // D_STATE

    def blk_f(s, c):
        return blk0 + s * nc + c

    def blk_b(s, c):
        return blk0 + s * nc + nc - 1 - c

    def chunk_specs(blk):
        return [pl.BlockSpec((lc, hp), lambda s, g, c: (blk(s, c), g)),
                pl.BlockSpec((lc, D_STATE), lambda s, g, c: (blk(s, c), b_off + g)),
                pl.BlockSpec((lc, D_STATE), lambda s, g, c: (blk(s, c), c_off + g))]

    in_specs = chunk_specs(blk_f)
    args = [xbc, xbc, xbc]
    if nc > 1:
        in_specs += chunk_specs(blk_b)
        args += [xbc, xbc, xbc]
    table_spec = pl.BlockSpec((2, None, HG_B, LANES), lambda s, g, c: (0, g, 0, 0))
    in_specs += [pl.BlockSpec((None, None, HG_B, lc), lambda s, g, c: (0, g, 0, blk_f(s, c))),
                 pl.BlockSpec((None, None, HG_B, lc), lambda s, g, c: (1, g, 0, blk_b(s, c))),
                 table_spec, table_spec]
    args += [dtr, dtr, a_neg, dt_bias]
    state_spec = pl.BlockSpec((None, 2, HG_B, P_B, D_STATE), lambda s, g, c: (s, 0, g, 0, 0))
    if has_init:
        in_specs.append(state_spec)
        args.append(init)
    out_specs = [pl.BlockSpec((seqlen, hp), lambda s, g, c: (s, g))]
    out_shape = [jax.ShapeDtypeStruct((nseq * seqlen, E_B), F32)]
    if emit_state:
        out_specs.append(state_spec)
        out_shape.append(jax.ShapeDtypeStruct((nseq, 2, H_B, P_B, D_STATE), F32))
    return pl.pallas_call(
        functools.partial(_ssd_kernel, nc=nc, has_init=has_init, emit_state=emit_state),
        grid=(nseq, G_B, nc),
        in_specs=in_specs,
        out_specs=out_specs,
        out_shape=out_shape,
        scratch_shapes=[pltpu.VMEM((2, hp, D_STATE), F32), pltpu.VMEM((2, hp, lc), F32),
                        pltpu.VMEM((2, hp, lc), F32)],
        compiler_params=_params(("arbitrary",) * 3),
        name="ssd_scan_state" if emit_state else "ssd_scan_init",
    )(*args)


def _mixer_out_kernel(yp_ref, ys_ref, u_ref, z_ref, xp_ref, xs_ref, mod_ref, nw_ref, sk_ref, w_ref, lnw_ref, lnb_ref,
                      o_ref, *, tp, grouped):
    y = _select_token_tile(yp_ref, ys_ref, TM, tp)
    u = u_ref[...].astype(F32)
    gate = _silu(z_ref[...].astype(F32))
    if grouped:
        parts = []
        for hd in range(H_A):
            seg = y[:, hd * DH_A:(hd + 1) * DH_A]
            mu = jnp.mean(seg, axis=-1, keepdims=True)
            sc = seg - mu
            var = jnp.mean(sc * sc, axis=-1, keepdims=True)
            parts.append(sc * lax.rsqrt(var + EPS))
        hn = jnp.concatenate(parts, axis=1) * nw_ref[...]
        t = (hn + sk_ref[...] * u) * gate
    else:
        t0 = (y + sk_ref[...] * u) * gate
        t = t0 * lax.rsqrt(jnp.mean(t0 * t0, axis=-1, keepdims=True) + EPS) * nw_ref[...]
    out = jnp.dot(t.astype(BF16), w_ref[...], preferred_element_type=F32)
    mod = mod_ref[...]
    r = ALPHA * _select_token_tile(xp_ref, xs_ref, TM, tp) + mod[2:3, :] * out
    o_ref[...] = _layer_norm_rows(r, lnw_ref[...], lnb_ref[...])


def _mixer_out(yp, ys, u, proj, zblk, x_src, t, mod, nw, sk, w_bf, lnw, lnb, tp, ls, grouped, name):
    e = w_bf.shape[0]
    y_specs, y_args = _token_tile_specs((yp, 0, ys, 0), TM, tp, e)
    x_specs, x_args = _token_tile_specs(x_src, TM, tp)
    vec_e = pl.BlockSpec((1, e), lambda m: (0, 0))
    vec_d = pl.BlockSpec((1, D_MODEL), lambda m: (0, 0))
    return pl.pallas_call(
        functools.partial(_mixer_out_kernel, tp=tp, grouped=grouped),
        grid=(t // TM,),
        in_specs=y_specs + [
                  pl.BlockSpec((TM, e), lambda m: (m, 0)),
                  pl.BlockSpec((TM, e), lambda m: (m, zblk))] + x_specs + [
                  pl.BlockSpec((None, MOD_ROWS, D_MODEL), lambda m: (_seq_of_tile(m, TM, tp, ls), 0, 0)),
                  vec_e, vec_e,
                  pl.BlockSpec((e, D_MODEL), lambda m: (0, 0)),
                  vec_d, vec_d],
        out_specs=pl.BlockSpec((TM, D_MODEL), lambda m: (m, 0)),
        out_shape=jax.ShapeDtypeStruct((t, D_MODEL), F32),
        compiler_params=_params(("arbitrary",)),
        name=name,
    )(*y_args, u, proj, *x_args, mod, nw.reshape(1, e), sk.reshape(1, e), w_bf, lnw.reshape(1, D_MODEL),
      lnb.reshape(1, D_MODEL))


def _router_kernel(x_ref, mod_ref, w_ref, b_ref, hm_ref, comb_ref):
    mod = mod_ref[...]
    hm = (x_ref[...] * (1.0 + mod[4:5, :]) + mod[3:4, :]).astype(BF16)
    hm_ref[...] = hm
    logits = jnp.dot(hm, w_ref[...], preferred_element_type=F32) + b_ref[...]
    lane = lax.broadcasted_iota(jnp.int32, logits.shape, 1).astype(F32)
    big = float(2 * LANES)
    glog = jnp.where(lane < MOE_GROUPS, logits, -jnp.inf)
    ge = jnp.exp(glog - jnp.max(glog, axis=1, keepdims=True))
    prob = ge / jnp.sum(ge, axis=1, keepdims=True)
    gp = jnp.max(prob, axis=1, keepdims=True)
    gi = jnp.min(jnp.where(prob == gp, lane, big), axis=1, keepdims=True)
    lo = ROUTER_OFF + gi * MOE_EXPERTS
    ev = jnp.where((lane >= lo) & (lane < lo + MOE_EXPERTS), logits, -jnp.inf)
    e1 = jnp.max(ev, axis=1, keepdims=True)
    i1 = jnp.min(jnp.where(ev == e1, lane, big), axis=1, keepdims=True)
    ev2 = jnp.where(lane == i1, -jnp.inf, ev)
    e2 = jnp.max(ev2, axis=1, keepdims=True)
    i2 = jnp.min(jnp.where(ev2 == e2, lane, big), axis=1, keepdims=True)
    t2 = jnp.exp(e2 - e1)
    w1 = 1.0 / (1.0 + t2)
    w2 = t2 / (1.0 + t2)
    comb_ref[...] = jnp.where(lane == i1, w1 * gp, jnp.where(lane == i2, w2 * gp, 0.0))


def _router(x, mod, w_r, b_r, tp, ls):
    t = x.shape[0]
    tm = TM_MOE
    return pl.pallas_call(
        _router_kernel,
        grid=(t // tm,),
        in_specs=[pl.BlockSpec((tm, D_MODEL), lambda m: (m, 0)),
                  pl.BlockSpec((None, MOD_ROWS, D_MODEL), lambda m: (_seq_of_tile(m, tm, tp, ls), 0, 0)),
                  pl.BlockSpec((D_MODEL, LANES), lambda m: (0, 0)),
                  pl.BlockSpec((1, LANES), lambda m: (0, 0))],
        out_specs=[pl.BlockSpec((tm, D_MODEL), lambda m: (m, 0)),
                   pl.BlockSpec((tm, LANES), lambda m: (m, 0))],
        out_shape=[jax.ShapeDtypeStruct((t, D_MODEL), BF16), jax.ShapeDtypeStruct((t, LANES), F32)],
        compiler_params=_params(("arbitrary",)),
        name="moe_router",
    )(x, mod, w_r, b_r)


def _moe_kernel(hm_ref, comb_ref, wg_ref, wu_ref, wd_ref, x_ref, mod_ref, lnw_ref, lnb_ref, o_ref):
    hm = hm_ref[...]
    comb = comb_ref[...]
    acc = None
    for g in range(MOE_GROUPS):
        a = _silu(jnp.dot(hm, wg_ref[g], preferred_element_type=F32)) * jnp.dot(
            hm, wu_ref[g], preferred_element_type=F32)
        parts = []
        for e in range(MOE_EXPERTS):
            lane = ROUTER_OFF + g * MOE_EXPERTS + e
            parts.append(a[:, e * D_FF_E:(e + 1) * D_FF_E] * comb[:, lane:lane + 1])
        part = jnp.dot(jnp.concatenate(parts, axis=1).astype(BF16), wd_ref[g], preferred_element_type=F32)
        acc = part if acc is None else acc + part
    mod = mod_ref[...]
    r = ALPHA * x_ref[...] + mod[5:6, :] * acc
    o_ref[...] = _layer_norm_rows(r, lnw_ref[...], lnb_ref[...])


def _moe(hm, comb, wg, wu, wd, x, mod, lnw, lnb, tp, ls):
    t = x.shape[0]
    tm = TM_MOE
    gf = MOE_EXPERTS * D_FF_E
    vec_d = pl.BlockSpec((1, D_MODEL), lambda m: (0, 0))
    return pl.pallas_call(
        _moe_kernel,
        grid=(t // tm,),
        in_specs=[pl.BlockSpec((tm, D_MODEL), lambda m: (m, 0)),
                  pl.BlockSpec((tm, LANES), lambda m: (m, 0)),
                  _resident((MOE_GROUPS, D_MODEL, gf)),
                  _resident((MOE_GROUPS, D_MODEL, gf)),
                  _resident((MOE_GROUPS, gf, D_MODEL)),
                  pl.BlockSpec((tm, D_MODEL), lambda m: (m, 0)),
                  pl.BlockSpec((None, MOD_ROWS, D_MODEL), lambda m: (_seq_of_tile(m, tm, tp, ls), 0, 0)),
                  vec_d, vec_d],
        out_specs=pl.BlockSpec((tm, D_MODEL), lambda m: (m, 0)),
        out_shape=jax.ShapeDtypeStruct((t, D_MODEL), F32),
        compiler_params=_params(("arbitrary",)),
        name="moe_experts",
    )(hm, comb, wg, wu, wd, x, mod, lnw.reshape(1, D_MODEL), lnb.reshape(1, D_MODEL))


def _moe_layer(x, mod, w_rg, b_rg, w_re, b_re, w_gate, w_up, w_down, lnw, lnb, tp, ls):
    w_r = jnp.concatenate([w_rg, jnp.moveaxis(w_re, 0, 1).reshape(D_MODEL, N_EXPERTS)], axis=1)
    w_r = jnp.pad(w_r, ((0, 0), (0, LANES - w_r.shape[1]))).astype(BF16)
    b_r = jnp.pad(jnp.concatenate([b_rg, b_re.reshape(-1)]), (0, LANES - MOE_GROUPS - N_EXPERTS)).reshape(1, LANES)
    hm, comb = _router(x, mod, w_r, b_r, tp, ls)
    gf = MOE_EXPERTS * D_FF_E
    wg = w_gate.transpose(0, 2, 1, 3).reshape(MOE_GROUPS, D_MODEL, gf).astype(BF16)
    wu = w_up.transpose(0, 2, 1, 3).reshape(MOE_GROUPS, D_MODEL, gf).astype(BF16)
    wd = w_down.reshape(MOE_GROUPS, gf, D_MODEL).astype(BF16)
    return _moe(hm, comb, wg, wu, wd, x, mod, lnw, lnb, tp, ls)


def _blockdiag_lanes(w):
    per = LANES // QKV_BLOCK
    wb = w.reshape(-1, per, QKV_BLOCK, QKV_BLOCK)
    eye = jnp.eye(per, dtype=w.dtype)
    return jnp.einsum('bncd,nm->bncmd', wb, eye).reshape(-1, LANES, LANES)


def _grid_to_cols(x):
    b, l, dd = x.shape
    return x.reshape(b, l // GRID_W, GRID_W, dd).transpose(0, 2, 1, 3).reshape(b, l, dd)


def _cols_to_grid(x):
    b, l, dd = x.shape
    return x.reshape(b, GRID_W, l // GRID_W, dd).transpose(0, 2, 1, 3).reshape(b, l, dd)


def kernel(x_prompt, x_sample, state_mlstm_C, state_mlstm_n, state_mlstm_m, state_ssd, c, c_ctx, ada_w, ada_b, ln_w, ln_b, a_w_in, a_conv_w, a_conv_b, a_w_q, a_w_k, a_w_v, a_w_gates, a_b_gates, a_norm_w, a_skip, a_w_down, b_w_in, b_conv_w, b_conv_b, b_dt_bias, b_a_log, b_d_skip, b_norm_w, b_w_out, moe_w_rg, moe_b_rg, moe_w_re, moe_b_re, moe_w_gate, moe_w_up, moe_w_down):
    bp, lp, _ = x_prompt.shape
    bs, ls, _ = x_sample.shape
    tp, ts = bp * lp, bs * ls
    assert lp % SCAN_CHUNK == 0 and ls % SCAN_CHUNK == 0 and tp % 512 == 0 and ls % 512 == 0
    assert bs + 1 <= N_MOD_SEQ

    cvec = jnp.concatenate([c_ctx[None, :], c, jnp.zeros((N_MOD_SEQ - 1 - bs, D_MODEL), F32)], axis=0)
    mods = _ada(cvec, ada_w, ada_b).reshape(DEPTH, N_MOD_SEQ, 6, D_MODEL)
    mods = jnp.pad(mods, ((0, 0), (0, 0), (0, MOD_ROWS - 6), (0, 0)))

    t = tp + ts
    x_src = (x_prompt.reshape(tp, D_MODEL), 0, x_sample.reshape(ts, D_MODEL), 0)

    mod = mods[0]
    (proj,) = _modmm(x_src, t, mod, a_w_in[0].astype(BF16), 0, tp, ls, 0, 1, "mlstm_in_proj")
    xc = _conv_silu(proj, 0, E_A, a_conv_w[0], a_conv_b[0], tp, lp, ls, "mlstm_conv")
    wqk = jnp.concatenate([_blockdiag_lanes(a_w_q[0]), _blockdiag_lanes(a_w_k[0])], axis=-1).astype(BF16)
    wv = _blockdiag_lanes(a_w_v[0]).astype(BF16)
    ngate = 4 * H_A
    wg = jnp.pad(a_w_gates[0].reshape(3, E_A, ngate), ((0, 0), (0, 0), (0, LANES - ngate))).astype(BF16)
    bg = jnp.pad(a_b_gates[0], (0, LANES - ngate)).reshape(1, LANES)
    q, k, v, g = _qkv_gates(xc, proj, wqk, wv, wg, bg)
    gt = g[:, :ngate].T
    hs_p, c_new, n_new, m_new = _mlstm_scan(q, k, v, gt, 0, bp, lp, emit_state=True)
    (hs_s,) = _mlstm_scan(q, k, v, gt, tp, bs, ls,
                          init=(state_mlstm_C[:, 0], state_mlstm_n[:, 0], state_mlstm_m[:, 0]))
    x = _mixer_out(hs_p, hs_s, xc, proj, 1, x_src, t, mod, a_norm_w[0], a_skip[0], a_w_down[0].astype(BF16),
                   ln_w[0, 0], ln_b[0, 0], tp, ls, True, "mlstm_out")
    x = _moe_layer(x, mod, moe_w_rg[0], moe_b_rg[0], moe_w_re[0], moe_b_re[0], moe_w_gate[0], moe_w_up[0],
                   moe_w_down[0], ln_w[0, 1], ln_b[0, 1], tp, ls)

    mod = mods[1]
    x_src = (x, 0, _grid_to_cols(x[tp:].reshape(bs, ls, D_MODEL)).reshape(ts, D_MODEL), 0)
    w_in = jnp.pad(b_w_in[0], ((0, 0), (0, -b_w_in.shape[2] % LANES))).astype(BF16)
    proj, dt = _modmm(x_src, t, mod, w_in, LANES, tp, ls, 0, 1, "ssd_in_proj")
    dt = dt[:, :2 * H_B]
    xbc = _conv_silu(proj, E_B, E_B + 2 * GN_B, b_conv_w[0], b_conv_b[0], tp, lp, ls, "ssd_conv")
    lane_bcast = (2, G_B, HG_B, LANES)
    a_neg = jnp.broadcast_to(-jnp.exp(b_a_log[0]).reshape(2, G_B, HG_B, 1), lane_bcast)
    dt_bias = jnp.broadcast_to(b_dt_bias[0].reshape(2, G_B, HG_B, 1), lane_bcast)
    dtr = dt.T.reshape(2, G_B, HG_B, tp + ts)
    y_p, s_new = _ssd_scan(xbc, dtr, a_neg, dt_bias, 0, bp, lp, emit_state=True)
    (y_s,) = _ssd_scan(xbc, dtr, a_neg, dt_bias, tp, bs, ls, init=state_ssd[:, 0])
    x = _mixer_out(y_p, y_s, xbc, proj, 0, x_src, t, mod, b_norm_w[0], jnp.repeat(b_d_skip[0], P_B),
                   b_w_out[0].astype(BF16), ln_w[1, 0], ln_b[1, 0], tp, ls, False, "ssd_out")
    x = _moe_layer(x, mod, moe_w_rg[1], moe_b_rg[1], moe_w_re[1], moe_b_re[1], moe_w_gate[1], moe_w_up[1],
                   moe_w_down[1], ln_w[1, 1], ln_b[1, 1], tp, ls)

    y_prompt = x[:tp].reshape(bp, lp, D_MODEL)
    y_sample = _cols_to_grid(x[tp:].reshape(bs, ls, D_MODEL))
    return (y_prompt, y_sample,
            c_new[:, None],
            n_new.reshape(bp, 1, 2, H_A, DH_A),
            m_new[:, :, :, 0, 0][:, None],
            s_new[:, None])
```

```python
import functools
import math

import jax
import jax.numpy as jnp
from jax import lax
from jax.experimental import pallas as pl
from jax.experimental.pallas import tpu as pltpu

F32 = jnp.float32
BF16 = jnp.bfloat16

D_MODEL = 1024
DEPTH = 2
GRID_W = 64
CONV_W = 5
E_A = 2 * D_MODEL
H_A = 4
DH_A = E_A // H_A
QKV_BLOCK = 4
E_B = 2 * D_MODEL
P_B = 64
H_B = E_B // P_B
G_B = 4
HG_B = H_B // G_B
D_STATE = 128
GN_B = G_B * D_STATE
MOE_GROUPS = 4
MOE_EXPERTS = 4
N_EXPERTS = MOE_GROUPS * MOE_EXPERTS
D_FF_E = 256
ALPHA = (2.0 * DEPTH) ** 0.25
EPS = 1e-5

LANES = 128
SUBLANES = 8
BF16_ROWS = 16
TM = 256
TM_MOE = 512
SCAN_CHUNK = 256
MOD_ROWS = 8
N_MOD_SEQ = 16
VMEM_LIMIT = 56 * 1024 * 1024
ROUTER_OFF = MOE_GROUPS


def _params(sem):
    return pltpu.CompilerParams(dimension_semantics=sem, vmem_limit_bytes=VMEM_LIMIT)


def _seq_of_tile(m, tm, tp, ls):
    npt = tp // tm
    return jnp.where(m < npt, 0, 1 + (m - npt) // (ls // tm))


def _silu(x):
    return x * jax.nn.sigmoid(x)


def _softplus(x):
    return jnp.maximum(x, 0.0) + jnp.log1p(jnp.exp(-jnp.abs(x)))


def _layer_norm_rows(r, w, b):
    mu = jnp.mean(r, axis=-1, keepdims=True)
    rc = r - mu
    var = jnp.mean(rc * rc, axis=-1, keepdims=True)
    return rc * lax.rsqrt(var + EPS) * w + b


def _ada_kernel(c_ref, w_ref, b_ref, o_ref):
    s = _silu(c_ref[...])
    o_ref[...] = jnp.dot(s.astype(BF16), w_ref[...].astype(BF16), preferred_element_type=F32) + b_ref[...]


def _ada(cvec, ada_w, ada_b):
    nl = ada_w.shape[0]
    n_out = ada_w.shape[2]
    tn = 1536
    return pl.pallas_call(
        _ada_kernel,
        grid=(nl, n_out // tn),
        in_specs=[pl.BlockSpec((N_MOD_SEQ, D_MODEL), lambda l, n: (0, 0)),
                  pl.BlockSpec((None, D_MODEL, tn), lambda l, n: (l, 0, n)),
                  pl.BlockSpec((None, 1, tn), lambda l, n: (l, 0, n))],
        out_specs=pl.BlockSpec((None, N_MOD_SEQ, tn), lambda l, n: (l, 0, n)),
        out_shape=jax.ShapeDtypeStruct((nl, N_MOD_SEQ, n_out), F32),
        compiler_params=_params(("arbitrary", "arbitrary")),
        name="ada_mod",
    )(cvec, ada_w, ada_b.reshape(nl, 1, n_out))


def _token_tile_specs(src, tm, tp, width=D_MODEL):
    pa, p0, sa, s0 = src
    npt = tp // tm
    nst = (sa.shape[0] - s0) // tm
    return ([pl.BlockSpec((tm, width), lambda m, *_: (p0 // tm + jnp.minimum(m, npt - 1), 0)),
             pl.BlockSpec((tm, width), lambda m, *_: (s0 // tm + jnp.clip(m - npt, 0, nst - 1), 0))], [pa, sa])


def _select_token_tile(p_ref, s_ref, tm, tp):
    return jnp.where(pl.program_id(0) < tp // tm, p_ref[...], s_ref[...])


def _modmm_kernel(xp_ref, xs_ref, mod_ref, w_ref, o_ref, *tail_ref, tm, tp, shift_row, scale_row):
    mod = mod_ref[...]
    x = _select_token_tile(xp_ref, xs_ref, tm, tp)
    h = x * (1.0 + mod[scale_row:scale_row + 1, :]) + mod[shift_row:shift_row + 1, :]
    res = jnp.dot(h.astype(BF16), w_ref[...], preferred_element_type=F32)
    n_main = o_ref.shape[1]
    o_ref[...] = res[:, :n_main].astype(o_ref.dtype)
    if tail_ref:
        tail_ref[0][...] = res[:, n_main:]


def _resident(shape):
    return pl.BlockSpec(shape, lambda *_: (0,) * len(shape), pipeline_mode=pl.Buffered(1))


def _modmm(src, t, mod, w_bf, n_tail, tp, ls, shift_row, scale_row, name):
    n_main = w_bf.shape[1] - n_tail
    tm = TM
    x_specs, x_args = _token_tile_specs(src, tm, tp)
    out_specs = [pl.BlockSpec((tm, n_main), lambda m: (m, 0))]
    out_shape = [jax.ShapeDtypeStruct((t, n_main), BF16)]
    if n_tail:
        out_specs.append(pl.BlockSpec((tm, n_tail), lambda m: (m, 0)))
        out_shape.append(jax.ShapeDtypeStruct((t, n_tail), F32))
    return pl.pallas_call(
        functools.partial(_modmm_kernel, tm=tm, tp=tp, shift_row=shift_row, scale_row=scale_row),
        grid=(t // tm,),
        in_specs=x_specs + [
            pl.BlockSpec((None, MOD_ROWS, D_MODEL), lambda m: (_seq_of_tile(m, tm, tp, ls), 0, 0)),
            _resident((D_MODEL, w_bf.shape[1]))],
        out_specs=out_specs,
        out_shape=out_shape,
        compiler_params=_params(("arbitrary",)),
        name=name,
    )(*x_args, mod, w_bf)


def _conv_kernel(cur_ref, prev_ref, next_ref, w_ref, b_ref, o_ref, ext_ref, *, tm, tp, lp, ls):
    m = pl.program_id(0)
    npt = tp // tm
    pos = jnp.where(m < npt, m % (lp // tm), (m - npt) % (ls // tm))
    last_pos = jnp.where(m < npt, lp // tm - 1, ls // tm - 1)
    halo = SUBLANES
    ext_ref[0:halo, :] = jnp.where(pos == 0, 0.0, prev_ref[...].astype(F32)[BF16_ROWS - halo:, :])
    ext_ref[halo:halo + tm, :] = cur_ref[...].astype(F32)
    ext_ref[halo + tm:2 * halo + tm, :] = jnp.where(pos == last_pos, 0.0, next_ref[...].astype(F32)[:halo, :])
    acc = jnp.zeros(o_ref.shape, F32) + b_ref[...]
    for j in range(CONV_W):
        acc = acc + ext_ref[pl.ds(halo - CONV_W // 2 + j, tm), :] * w_ref[j:j + 1, :]
    o_ref[...] = _silu(acc).astype(o_ref.dtype)


def _conv_silu(a, col0, width, w, b, tp, lp, ls, name):
    t = a.shape[0]
    tm, tc = TM, 1024
    cb0 = col0 // tc
    nhalo = t // BF16_ROWS
    per = tm // BF16_ROWS
    return pl.pallas_call(
        functools.partial(_conv_kernel, tm=tm, tp=tp, lp=lp, ls=ls),
        grid=(t // tm, width // tc),
        in_specs=[pl.BlockSpec((tm, tc), lambda m, j: (m, cb0 + j)),
                  pl.BlockSpec((BF16_ROWS, tc), lambda m, j: (jnp.maximum(m * per - 1, 0), cb0 + j)),
                  pl.BlockSpec((BF16_ROWS, tc), lambda m, j: (jnp.minimum((m + 1) * per, nhalo - 1), cb0 + j)),
                  pl.BlockSpec((CONV_W, tc), lambda m, j: (0, j)),
                  pl.BlockSpec((1, tc), lambda m, j: (0, j))],
        out_specs=pl.BlockSpec((tm, tc), lambda m, j: (m, j)),
        out_shape=jax.ShapeDtypeStruct((t, width), BF16),
        scratch_shapes=[pltpu.VMEM((tm + 2 * SUBLANES, tc), F32)],
        compiler_params=_params(("arbitrary", "arbitrary")),
        name=name,
    )(a, a, a, w, b.reshape(1, width))


def _qkv_kernel(xc_ref, xu_ref, wqk_ref, wv_ref, wg_ref, bg_ref, q_ref, k_ref, v_ref, g_ref):
    nblk = E_A // LANES
    g = jnp.zeros(g_ref.shape, F32) + bg_ref[...]
    for b in range(nblk):
        sl = slice(b * LANES, (b + 1) * LANES)
        xcb = xc_ref[:, sl]
        xub = xu_ref[:, sl]
        qk = jnp.dot(xcb, wqk_ref[b], preferred_element_type=F32)
        vv = jnp.dot(xub, wv_ref[b], preferred_element_type=F32)
        qb = qk[:, :LANES].astype(BF16)
        kb = qk[:, LANES:].astype(BF16)
        vb = vv.astype(BF16)
        q_ref[:, sl] = qb
        k_ref[:, sl] = kb
        v_ref[:, sl] = vb
        g = g + jnp.dot(qb, wg_ref[0, sl, :], preferred_element_type=F32)
        g = g + jnp.dot(kb, wg_ref[1, sl, :], preferred_element_type=F32)
        g = g + jnp.dot(vb, wg_ref[2, sl, :], preferred_element_type=F32)
    lane = lax.broadcasted_iota(jnp.int32, g.shape, 1)
    is_forget = (lane & (2 * H_A - 1)) >= H_A
    log_sig = jnp.minimum(g, 0.0) - jnp.log1p(jnp.exp(-jnp.abs(g)))
    g_ref[...] = jnp.where(is_forget, log_sig, g)


def _qkv_gates(xc, proj, wqk, wv, wg, bg):
    t = xc.shape[0]
    nblk = E_A // LANES
    tile = pl.BlockSpec((TM, E_A), lambda m: (m, 0))
    return pl.pallas_call(
        _qkv_kernel,
        grid=(t // TM,),
        in_specs=[tile, tile,
                  pl.BlockSpec((nblk, LANES, 2 * LANES), lambda m: (0, 0, 0)),
                  pl.BlockSpec((nblk, LANES, LANES), lambda m: (0, 0, 0)),
                  pl.BlockSpec((3, E_A, LANES), lambda m: (0, 0, 0)),
                  pl.BlockSpec((1, LANES), lambda m: (0, 0))],
        out_specs=[tile, tile, tile, pl.BlockSpec((TM, LANES), lambda m: (m, 0))],
        out_shape=[jax.ShapeDtypeStruct((t, E_A), BF16)] * 3 + [jax.ShapeDtypeStruct((t, LANES), F32)],
        compiler_params=_params(("arbitrary",)),
        name="mlstm_qkv_gates",
    )(xc, proj, wqk, wv, wg, bg)


def _order_mask(dirn, n, rows_are_later):
    a_io = lax.broadcasted_iota(jnp.int32, (n, n), 0)
    b_io = lax.broadcasted_iota(jnp.int32, (n, n), 1)
    if (dirn == 0) == rows_are_later:
        return b_io <= a_io
    return b_io >= a_io


def _exact_cumsum_rows(rows, tri):
    nrow = rows.shape[0]
    p1 = rows.astype(BF16).astype(F32)
    r1 = rows - p1
    p2 = r1.astype(BF16).astype(F32)
    p3 = r1 - p2
    pieces = jnp.concatenate([p1, p2, p3, jnp.zeros_like(p1)], axis=0).astype(BF16)
    cs = jnp.dot(pieces, tri, preferred_element_type=F32)
    return cs[0:nrow] + cs[nrow:2 * nrow] + cs[2 * nrow:3 * nrow]


def _pick_row(blk, idx):
    sub = lax.broadcasted_iota(jnp.int32, (blk.shape[0], 1), 0)
    return jnp.sum(jnp.where(sub == idx, blk, 0.0), axis=0, keepdims=True)


def _rows_to_cols(rows):
    pad = jnp.zeros((LANES - rows.shape[0], rows.shape[1]), F32)
    return jnp.concatenate([rows, pad], axis=0).T


def _pick_col(blk, idx):
    lane = lax.broadcasted_iota(jnp.int32, (1, blk.shape[1]), 1)
    return jnp.sum(jnp.where(lane == idx, blk, 0.0), axis=1, keepdims=True)


def _mlstm_direction(dirn, h, q, k, v, s_qk, kt, gt_ref, c_ref, n_ref, m_ref, use_state, need_update):
    lc = SCAN_CHUNK
    scale = DH_A ** -0.5
    log_scale = -0.5 * math.log(DH_A)
    gates = gt_ref[dirn * 2 * H_A:(dirn + 1) * 2 * H_A, :]
    tri = jnp.where(_order_mask(dirn, lc, False), 1.0, 0.0).astype(BF16)
    csum = _exact_cumsum_rows(gates, tri)
    i_row = _pick_row(gates, h)
    f_row = _pick_row(gates, H_A + h)
    b_row = _pick_row(csum, H_A + h)
    v_row = i_row - b_row
    valid = _order_mask(dirn, lc, True)
    m_prev = m_ref[:, 0:1]
    run_max = jnp.max(jnp.where(valid, v_row, -jnp.inf), axis=1, keepdims=True)
    u_col = -jnp.maximum(m_prev, run_max)
    p = s_qk * jnp.exp(jnp.where(valid, u_col + (v_row + log_scale), -jnp.inf))
    num = jnp.dot(p.astype(BF16), v, preferred_element_type=F32)
    den = jnp.sum(p, axis=1, keepdims=True)
    if use_state:
        w_inter = jnp.exp(m_prev + u_col) * scale
        qc = jnp.dot(q, c_ref[...].astype(BF16), preferred_element_type=F32)
        qn = jnp.sum(q.astype(F32) * n_ref[...], axis=1, keepdims=True)
        num = num + w_inter * qc
        den = den + w_inter * qn
    b_col = _rows_to_cols(b_row)[:, 0:1]
    h_out = num / jnp.maximum(jnp.abs(den), jnp.exp(u_col - b_col))
    if not need_update:
        return h_out, None
    b_last = jnp.sum(f_row, axis=1, keepdims=True)
    logk = b_last + v_row
    m_new = jnp.maximum(b_last + m_prev, jnp.max(logk, axis=1, keepdims=True))
    wk = jnp.exp(logk - m_new)
    c_new = jnp.dot((kt * wk).astype(BF16), v, preferred_element_type=F32)
    wk8 = (jnp.zeros((SUBLANES, lc), F32) + wk).astype(BF16)
    n_new = jnp.dot(wk8, k, preferred_element_type=F32)[0:1, :]
    return h_out, (c_new, n_new, m_new, jnp.exp(b_last + m_prev - m_new))


def _mlstm_kernel(*refs, nc, has_init, emit_state):
    shared = nc == 1
    pos = 4 if shared else 8
    chunk_refs = [refs[0:4], refs[0:4] if shared else refs[4:8]]
    if has_init:
        c0_ref, n0_ref, m0_ref = refs[pos:pos + 3]
        pos += 3
    hs_ref = refs[pos]
    pos += 1
    if emit_state:
        cn_ref, nn_ref, mn_ref = refs[pos:pos + 3]
        pos += 3
    c_scr, n_scr, m_scr = refs[pos:pos + 3]

    lc = SCAN_CHUNK
    s_id, h, c = (pl.program_id(i) for i in range(3))
    use_state = has_init or nc > 1

    @pl.when(c == 0)
    def _():
        for dirn in range(2):
            if has_init:
                c_scr[dirn] = c0_ref[dirn]
                n_scr[dirn] = _pick_row(n0_ref[dirn], h)
                m_scr[dirn] = jnp.zeros(m_scr.shape[1:], F32) + m0_ref[(s_id * 2 + dirn) * H_A + h]
            else:
                c_scr[dirn] = jnp.zeros(c_scr.shape[1:], F32)
                n_scr[dirn] = jnp.zeros(n_scr.shape[1:], F32)
                m_scr[dirn] = jnp.zeros(m_scr.shape[1:], F32)
        if not shared:
            hs_ref[...] = jnp.zeros(hs_ref.shape, F32)

    def chunk_operands(q_ref, k_ref, v_ref, gt_ref):
        q = q_ref[...]
        k = k_ref[...]
        s_qk = lax.dot_general(q, k, (((1,), (1,)), ((), ())), preferred_element_type=F32)
        return q, k, v_ref[...], s_qk, k.astype(F32).T, gt_ref

    ops = [chunk_operands(*chunk_refs[0])]
    ops.append(ops[0] if shared else chunk_operands(*chunk_refs[1]))

    def run(need_update):
        return [_mlstm_direction(dirn, h, *ops[dirn], c_scr.at[dirn], n_scr.at[dirn], m_scr.at[dirn],
                                 use_state, need_update) for dirn in range(2)]

    def write_out(res):
        if shared:
            hs_ref[...] = res[0][0] + res[1][0]
        else:
            rows_f = pl.ds(pl.multiple_of(c * lc, lc), lc)
            rows_b = pl.ds(pl.multiple_of((nc - 1 - c) * lc, lc), lc)
            hs_ref[rows_f, :] = hs_ref[rows_f, :] + res[0][0]
            hs_ref[rows_b, :] = hs_ref[rows_b, :] + res[1][0]

    def with_update():
        res = run(True)
        write_out(res)
        for dirn in range(2):
            c_new, n_new, m_new, decay = res[dirn][1]
            if use_state:
                c_new = c_new + decay * c_scr[dirn]
                n_new = n_new + decay * n_scr[dirn]
            m_new_row = jnp.zeros(m_scr.shape[1:], F32) + m_new
            if nc > 1:
                c_scr[dirn] = c_new
                n_scr[dirn] = n_new
                m_scr[dirn] = m_new_row
            if emit_state:
                @pl.when(c == nc - 1)
                def _():
                    cn_ref[dirn] = c_new
                    nn_ref[dirn] = n_new
                    mn_ref[dirn] = m_new_row

    def without_update():
        write_out(run(False))

    if emit_state:
        with_update()
    else:
        pl.when(c < nc - 1)(with_update)
        pl.when(c == nc - 1)(without_update)


def _mlstm_scan(q, k, v, gt, row0, nseq, seqlen, init=None, emit_state=False):
    lc = SCAN_CHUNK
    nc = seqlen // lc
    blk0 = row0 // lc
    has_init = init is not None

    def blk_f(s, c):
        return blk0 + s * nc + c

    def blk_b(s, c):
        return blk0 + s * nc + nc - 1 - c

    def chunk_specs(blk):
        return [pl.BlockSpec((lc, DH_A), lambda s, h, c: (blk(s, c), h))] * 3 + [
            pl.BlockSpec((4 * H_A, lc), lambda s, h, c: (0, blk(s, c)))]

    in_specs = chunk_specs(blk_f)
    args = [q, k, v, gt]
    if nc > 1:
        in_specs += chunk_specs(blk_b)
        args += [q, k, v, gt]
    if has_init:
        c0, n0, m0 = init
        in_specs += [pl.BlockSpec((None, 2, None, DH_A, DH_A), lambda s, h, c: (s, 0, h, 0, 0)),
                     pl.BlockSpec((None, 2, H_A, DH_A), lambda s, h, c: (s, 0, 0, 0)),
                     pl.BlockSpec(memory_space=pltpu.SMEM)]
        args += [c0, n0, m0.reshape(-1)]
    out_specs = [pl.BlockSpec((seqlen, DH_A), lambda s, h, c: (s, h))]
    out_shape = [jax.ShapeDtypeStruct((nseq * seqlen, E_A), F32)]
    if emit_state:
        out_specs += [pl.BlockSpec((None, 2, None, DH_A, DH_A), lambda s, h, c: (s, 0, h, 0, 0)),
                      pl.BlockSpec((None, 2, None, 1, DH_A), lambda s, h, c: (s, 0, h, 0, 0)),
                      pl.BlockSpec((None, 2, None, 1, LANES), lambda s, h, c: (s, 0, h, 0, 0))]
        out_shape += [jax.ShapeDtypeStruct((nseq, 2, H_A, DH_A, DH_A), F32),
                      jax.ShapeDtypeStruct((nseq, 2, H_A, 1, DH_A), F32),
                      jax.ShapeDtypeStruct((nseq, 2, H_A, 1, LANES), F32)]
    return pl.pallas_call(
        functools.partial(_mlstm_kernel, nc=nc, has_init=has_init, emit_state=emit_state),
        grid=(nseq, H_A, nc),
        in_specs=in_specs,
        out_specs=out_specs,
        out_shape=out_shape,
        scratch_shapes=[pltpu.VMEM((2, DH_A, DH_A), F32), pltpu.VMEM((2, 1, DH_A), F32),
                        pltpu.VMEM((2, 1, LANES), F32)],
        compiler_params=_params(("arbitrary",) * 3),
        name="mlstm_scan_state" if emit_state else "mlstm_scan_init",
    )(*args)


def _ssd_direction(dirn, xt, bb, ct, cbt_raw, dtr_ref, a_ref, bias_ref, h_ref, yt_ref, xw_ref, use_state):
    lc = SCAN_CHUNK
    r_io = lax.broadcasted_iota(jnp.int32, (lc, lc), 0)
    s_io = lax.broadcasted_iota(jnp.int32, (lc, lc), 1)
    valid = (r_io <= s_io) if dirn == 0 else (r_io >= s_io)
    tri = jnp.where(valid, 1.0, 0.0).astype(BF16)

    dt_rows = _softplus(dtr_ref[...] + bias_ref[:, 0:1])
    a_rows = dt_rows * a_ref[:, 0:1]
    a1 = a_rows.astype(BF16).astype(F32)
    r1 = a_rows - a1
    a2 = r1.astype(BF16).astype(F32)
    a3 = r1 - a2
    pieces = jnp.concatenate([a1, a2, a3, jnp.zeros_like(a1)], axis=0).astype(BF16)
    cs = jnp.dot(pieces, tri, preferred_element_type=F32)
    acs_rows = cs[0:HG_B] + cs[HG_B:2 * HG_B] + cs[2 * HG_B:3 * HG_B]
    a_last = jnp.sum(a_rows, axis=1, keepdims=True)
    wr_rows = jnp.exp(a_last - acs_rows) * dt_rows
    acs_cols = jnp.concatenate([acs_rows, jnp.zeros((LANES - HG_B, lc), F32)], axis=0).T

    cbt = jnp.where(valid, cbt_raw, 0.0)
    if use_state:
        inter_t = jnp.dot(h_ref[...].astype(BF16), ct, preferred_element_type=F32)
        e_rows = jnp.exp(acs_rows)
    for hg in range(HG_B):
        sl = slice(hg * P_B, (hg + 1) * P_B)
        row = slice(hg, hg + 1)
        seg_t = acs_rows[row, :] - acs_cols[:, hg:hg + 1]
        wt = cbt * jnp.exp(jnp.minimum(seg_t, 0.0))
        xth = xt[sl, :]
        yth = jnp.dot((xth * dt_rows[row, :]).astype(BF16), wt.astype(BF16), preferred_element_type=F32)
        if use_state:
            yth = yth + inter_t[sl, :] * e_rows[row, :]
        yt_ref[sl, :] = yth
        xw_ref[sl, :] = xth * wr_rows[row, :]
    return a_last


def _ssd_kernel(*refs, nc, has_init, emit_state):
    shared = nc == 1
    pos = 3 if shared else 6
    chunk_refs = [refs[0:3], refs[0:3] if shared else refs[3:6]]
    dtr_refs = refs[pos:pos + 2]
    a_ref, bias_ref = refs[pos + 2:pos + 4]
    pos += 4
    if has_init:
        h0_ref = refs[pos]
        pos += 1
    y_ref = refs[pos]
    pos += 1
    if emit_state:
        sn_ref = refs[pos]
        pos += 1
    h_scr, yt_scr, xw_scr = refs[pos:pos + 3]

    lc = SCAN_CHUNK
    c = pl.program_id(2)
    use_state = has_init or nc > 1
    hp = HG_B * P_B

    @pl.when(c == 0)
    def _():
        if has_init:
            h_scr[...] = h0_ref[...].reshape(2, hp, D_STATE)
        else:
            h_scr[...] = jnp.zeros(h_scr.shape, F32)
        if not shared:
            y_ref[...] = jnp.zeros(y_ref.shape, F32)

    def chunk_operands(x_ref, b_ref, c_ref):
        bb = b_ref[...]
        ct = c_ref[...].astype(F32).T.astype(BF16)
        return x_ref[...].astype(F32).T, bb, ct, jnp.dot(bb, ct, preferred_element_type=F32)

    ops = [chunk_operands(*chunk_refs[0])]
    ops.append(ops[0] if shared else chunk_operands(*chunk_refs[1]))
    a_last = [_ssd_direction(dirn, *ops[dirn], dtr_refs[dirn], a_ref.at[dirn], bias_ref.at[dirn], h_scr.at[dirn],
                             yt_scr.at[dirn], xw_scr.at[dirn], use_state) for dirn in range(2)]

    if shared:
        y_ref[...] = (yt_scr[0] + yt_scr[1]).T
    else:
        rows_f = pl.ds(pl.multiple_of(c * lc, lc), lc)
        rows_b = pl.ds(pl.multiple_of((nc - 1 - c) * lc, lc), lc)
        y_ref[rows_f, :] = y_ref[rows_f, :] + yt_scr[0].T
        y_ref[rows_b, :] = y_ref[rows_b, :] + yt_scr[1].T

    def update_state():
        for dirn in range(2):
            upd = jnp.dot(xw_scr[dirn].astype(BF16), ops[dirn][1], preferred_element_type=F32)
            dec = jnp.exp(a_last[dirn])
            for hg in range(HG_B):
                sl = slice(hg * P_B, (hg + 1) * P_B)
                new = upd[sl, :]
                if use_state:
                    new = new + dec[hg:hg + 1, :] * h_scr[dirn, sl, :]
                h_scr[dirn, sl, :] = new
        if emit_state:
            @pl.when(c == nc - 1)
            def _():
                sn_ref[...] = h_scr[...].reshape(2, HG_B, P_B, D_STATE)

    if emit_state:
        update_state()
    elif nc > 1:
        pl.when(c < nc - 1)(update_state)


def _ssd_scan(xbc, dtr, a_neg, dt_bias, row0, nseq, seqlen, init=None, emit_state=False):
    lc = SCAN_CHUNK
    nc = seqlen // lc
    blk0 = row0 // lc
    has_init = init is not None
    hp = HG_B * P_B
    b_off = E_B // D_STATE
    c_off = (E_B + GN_B) // D_STATE

    def blk_f(s, c):
        return blk0 + s * nc + c

    def blk_b(s, c):
        return blk0 + s * nc + nc - 1 - c

    def chunk_specs(blk):
        return [pl.BlockSpec((lc, hp), lambda s, g, c: (blk(s, c), g)),
                pl.BlockSpec((lc, D_STATE), lambda s, g, c: (blk(s, c), b_off + g)),
                pl.BlockSpec((lc, D_STATE), lambda s, g, c: (blk(s, c), c_off + g))]

    in_specs = chunk_specs(blk_f)
    args = [xbc, xbc, xbc]
    if nc > 1:
        in_specs += chunk_specs(blk_b)
        args += [xbc, xbc, xbc]
    table_spec = pl.BlockSpec((2, None, HG_B, LANES), lambda s, g, c: (0, g, 0, 0))
    in_specs += [pl.BlockSpec((None, None, HG_B, lc), lambda s, g, c: (0, g, 0, blk_f(s, c))),
                 pl.BlockSpec((None, None, HG_B, lc), lambda s, g, c: (1, g, 0, blk_b(s, c))),
                 table_spec, table_spec]
    args += [dtr, dtr, a_neg, dt_bias]
    state_spec = pl.BlockSpec((None, 2, HG_B, P_B, D_STATE), lambda s, g, c: (s, 0, g, 0, 0))
    if has_init:
        in_specs.append(state_spec)
        args.append(init)
    out_specs = [pl.BlockSpec((seqlen, hp), lambda s, g, c: (s, g))]
    out_shape = [jax.ShapeDtypeStruct((nseq * seqlen, E_B), F32)]
    if emit_state:
        out_specs.append(state_spec)
        out_shape.append(jax.ShapeDtypeStruct((nseq, 2, H_B, P_B, D_STATE), F32))
    return pl.pallas_call(
        functools.partial(_ssd_kernel, nc=nc, has_init=has_init, emit_state=emit_state),
        grid=(nseq, G_B, nc),
        in_specs=in_specs,
        out_specs=out_specs,
        out_shape=out_shape,
        scratch_shapes=[pltpu.VMEM((2, hp, D_STATE), F32), pltpu.VMEM((2, hp, lc), F32),
                        pltpu.VMEM((2, hp, lc), F32)],
        compiler_params=_params(("arbitrary",) * 3),
        name="ssd_scan_state" if emit_state else "ssd_scan_init",
    )(*args)


def _mixer_out_kernel(yp_ref, ys_ref, u_ref, z_ref, xp_ref, xs_ref, mod_ref, nw_ref, sk_ref, w_ref, lnw_ref, lnb_ref,
                      wr_ref, br_ref, o_ref, hm_ref, comb_ref, *, tp, grouped):
    y = _select_token_tile(yp_ref, ys_ref, TM, tp)
    u = u_ref[...].astype(F32)
    gate = _silu(z_ref[...].astype(F32))
    if grouped:
        parts = []
        for hd in range(H_A):
            seg = y[:, hd * DH_A:(hd + 1) * DH_A]
            mu = jnp.mean(seg, axis=-1, keepdims=True)
            sc = seg - mu
            var = jnp.mean(sc * sc, axis=-1, keepdims=True)
            parts.append(sc * lax.rsqrt(var + EPS))
        hn = jnp.concatenate(parts, axis=1) * nw_ref[...]
        t = (hn + sk_ref[...] * u) * gate
    else:
        t0 = (y + sk_ref[...] * u) * gate
        t = t0 * lax.rsqrt(jnp.mean(t0 * t0, axis=-1, keepdims=True) + EPS) * nw_ref[...]
    out = jnp.dot(t.astype(BF16), w_ref[...], preferred_element_type=F32)
    mod = mod_ref[...]
    r = ALPHA * _select_token_tile(xp_ref, xs_ref, TM, tp) + mod[2:3, :] * out
    x_new = _layer_norm_rows(r, lnw_ref[...], lnb_ref[...])
    o_ref[...] = x_new
    hm_ref[...], comb_ref[...] = _route(x_new, mod, wr_ref[...], br_ref[...])


def _mixer_out(yp, ys, u, proj, zblk, x_src, t, mod, nw, sk, w_bf, lnw, lnb, w_r, b_r, tp, ls, grouped, name):
    e = w_bf.shape[0]
    y_specs, y_args = _token_tile_specs((yp, 0, ys, 0), TM, tp, e)
    x_specs, x_args = _token_tile_specs(x_src, TM, tp)
    vec_e = pl.BlockSpec((1, e), lambda m: (0, 0))
    vec_d = pl.BlockSpec((1, D_MODEL), lambda m: (0, 0))
    return pl.pallas_call(
        functools.partial(_mixer_out_kernel, tp=tp, grouped=grouped),
        grid=(t // TM,),
        in_specs=y_specs + [
                  pl.BlockSpec((TM, e), lambda m: (m, 0)),
                  pl.BlockSpec((TM, e), lambda m: (m, zblk))] + x_specs + [
                  pl.BlockSpec((None, MOD_ROWS, D_MODEL), lambda m: (_seq_of_tile(m, TM, tp, ls), 0, 0)),
                  vec_e, vec_e,
                  pl.BlockSpec((e, D_MODEL), lambda m: (0, 0)),
                  vec_d, vec_d,
                  pl.BlockSpec((D_MODEL, LANES), lambda m: (0, 0)),
                  pl.BlockSpec((1, LANES), lambda m: (0, 0))],
        out_specs=[pl.BlockSpec((TM, D_MODEL), lambda m: (m, 0)),
                   pl.BlockSpec((TM, D_MODEL), lambda m: (m, 0)),
                   pl.BlockSpec((TM, LANES), lambda m: (m, 0))],
        out_shape=[jax.ShapeDtypeStruct((t, D_MODEL), F32), jax.ShapeDtypeStruct((t, D_MODEL), BF16),
                   jax.ShapeDtypeStruct((t, LANES), F32)],
        compiler_params=_params(("arbitrary",)),
        name=name,
    )(*y_args, u, proj, *x_args, mod, nw.reshape(1, e), sk.reshape(1, e), w_bf, lnw.reshape(1, D_MODEL),
      lnb.reshape(1, D_MODEL), w_r, b_r)


def _route(x, mod, w_r, b_r):
    hm = (x * (1.0 + mod[4:5, :]) + mod[3:4, :]).astype(BF16)
    logits = jnp.dot(hm, w_r, preferred_element_type=F32) + b_r
    lane = lax.broadcasted_iota(jnp.int32, logits.shape, 1).astype(F32)
    big = float(2 * LANES)
    glog = jnp.where(lane < MOE_GROUPS, logits, -jnp.inf)
    ge = jnp.exp(glog - jnp.max(glog, axis=1, keepdims=True))
    prob = ge / jnp.sum(ge, axis=1, keepdims=True)
    gp = jnp.max(prob, axis=1, keepdims=True)
    gi = jnp.min(jnp.where(prob == gp, lane, big), axis=1, keepdims=True)
    lo = ROUTER_OFF + gi * MOE_EXPERTS
    ev = jnp.where((lane >= lo) & (lane < lo + MOE_EXPERTS), logits, -jnp.inf)
    e1 = jnp.max(ev, axis=1, keepdims=True)
    i1 = jnp.min(jnp.where(ev == e1, lane, big), axis=1, keepdims=True)
    ev2 = jnp.where(lane == i1, -jnp.inf, ev)
    e2 = jnp.max(ev2, axis=1, keepdims=True)
    i2 = jnp.min(jnp.where(ev2 == e2, lane, big), axis=1, keepdims=True)
    t2 = jnp.exp(e2 - e1)
    w1 = 1.0 / (1.0 + t2)
    w2 = t2 / (1.0 + t2)
    return hm, jnp.where(lane == i1, w1 * gp, jnp.where(lane == i2, w2 * gp, 0.0))


def _moe_kernel(hm_ref, comb_ref, wg_ref, wu_ref, wd_ref, x_ref, mod_ref, lnw_ref, lnb_ref, op_ref, os_ref, *, npt):
    hm = hm_ref[...]
    comb = comb_ref[...]
    acc = None
    for g in range(MOE_GROUPS):
        a = _silu(jnp.dot(hm, wg_ref[g], preferred_element_type=F32)) * jnp.dot(
            hm, wu_ref[g], preferred_element_type=F32)
        parts = []
        for e in range(MOE_EXPERTS):
            lane = ROUTER_OFF + g * MOE_EXPERTS + e
            parts.append(a[:, e * D_FF_E:(e + 1) * D_FF_E] * comb[:, lane:lane + 1])
        part = jnp.dot(jnp.concatenate(parts, axis=1).astype(BF16), wd_ref[g], preferred_element_type=F32)
        acc = part if acc is None else acc + part
    mod = mod_ref[...]
    r = ALPHA * x_ref[...] + mod[5:6, :] * acc
    y = _layer_norm_rows(r, lnw_ref[...], lnb_ref[...])
    m = pl.program_id(0)

    @pl.when(m < npt)
    def _():
        op_ref[...] = y

    @pl.when(m >= npt)
    def _():
        os_ref[...] = y


def _moe(hm, comb, wg, wu, wd, x, mod, lnw, lnb, tp, ls):
    t = x.shape[0]
    tm = TM_MOE
    npt = tp // tm
    nst = (t - tp) // tm
    gf = MOE_EXPERTS * D_FF_E
    vec_d = pl.BlockSpec((1, D_MODEL), lambda m: (0, 0))
    return pl.pallas_call(
        functools.partial(_moe_kernel, npt=npt),
        grid=(t // tm,),
        in_specs=[pl.BlockSpec((tm, D_MODEL), lambda m: (m, 0)),
                  pl.BlockSpec((tm, LANES), lambda m: (m, 0)),
                  _resident((MOE_GROUPS, D_MODEL, gf)),
                  _resident((MOE_GROUPS, D_MODEL, gf)),
                  _resident((MOE_GROUPS, gf, D_MODEL)),
                  pl.BlockSpec((tm, D_MODEL), lambda m: (m, 0)),
                  pl.BlockSpec((None, MOD_ROWS, D_MODEL), lambda m: (_seq_of_tile(m, tm, tp, ls), 0, 0)),
                  vec_d, vec_d],
        out_specs=[pl.BlockSpec((tm, D_MODEL), lambda m: (jnp.minimum(m, npt - 1), 0)),
                   pl.BlockSpec((tm, D_MODEL), lambda m: (jnp.clip(m - npt, 0, nst - 1), 0))],
        out_shape=[jax.ShapeDtypeStruct((tp, D_MODEL), F32), jax.ShapeDtypeStruct((t - tp, D_MODEL), F32)],
        compiler_params=_params(("arbitrary",)),
        name="moe_experts",
    )(hm, comb, wg, wu, wd, x, mod, lnw.reshape(1, D_MODEL), lnb.reshape(1, D_MODEL))


def _router_weights(w_rg, b_rg, w_re, b_re):
    w_r = jnp.concatenate([w_rg, jnp.moveaxis(w_re, 0, 1).reshape(D_MODEL, N_EXPERTS)], axis=1)
    w_r = jnp.pad(w_r, ((0, 0), (0, LANES - w_r.shape[1]))).astype(BF16)
    b_r = jnp.pad(jnp.concatenate([b_rg, b_re.reshape(-1)]), (0, LANES - MOE_GROUPS - N_EXPERTS)).reshape(1, LANES)
    return w_r, b_r


def _expert_weights(w_gate, w_up, w_down):
    gf = MOE_EXPERTS * D_FF_E
    wg = w_gate.transpose(0, 2, 1, 3).reshape(MOE_GROUPS, D_MODEL, gf).astype(BF16)
    wu = w_up.transpose(0, 2, 1, 3).reshape(MOE_GROUPS, D_MODEL, gf).astype(BF16)
    wd = w_down.reshape(MOE_GROUPS, gf, D_MODEL).astype(BF16)
    return wg, wu, wd


def _blockdiag_lanes(w):
    per = LANES // QKV_BLOCK
    wb = w.reshape(-1, per, QKV_BLOCK, QKV_BLOCK)
    eye = jnp.eye(per, dtype=w.dtype)
    return jnp.einsum('bncd,nm->bncmd', wb, eye).reshape(-1, LANES, LANES)


def _grid_to_cols(x):
    b, l, dd = x.shape
    return x.reshape(b, l // GRID_W, GRID_W, dd).transpose(0, 2, 1, 3).reshape(b, l, dd)


def _cols_to_grid(x):
    b, l, dd = x.shape
    return x.reshape(b, GRID_W, l // GRID_W, dd).transpose(0, 2, 1, 3).reshape(b, l, dd)


def kernel(x_prompt, x_sample, state_mlstm_C, state_mlstm_n, state_mlstm_m, state_ssd, c, c_ctx, ada_w, ada_b, ln_w, ln_b, a_w_in, a_conv_w, a_conv_b, a_w_q, a_w_k, a_w_v, a_w_gates, a_b_gates, a_norm_w, a_skip, a_w_down, b_w_in, b_conv_w, b_conv_b, b_dt_bias, b_a_log, b_d_skip, b_norm_w, b_w_out, moe_w_rg, moe_b_rg, moe_w_re, moe_b_re, moe_w_gate, moe_w_up, moe_w_down):
    bp, lp, _ = x_prompt.shape
    bs, ls, _ = x_sample.shape
    tp, ts = bp * lp, bs * ls
    assert lp % SCAN_CHUNK == 0 and ls % SCAN_CHUNK == 0 and tp % 512 == 0 and ls % 512 == 0
    assert bs + 1 <= N_MOD_SEQ

    cvec = jnp.concatenate([c_ctx[None, :], c, jnp.zeros((N_MOD_SEQ - 1 - bs, D_MODEL), F32)], axis=0)
    mods = _ada(cvec, ada_w, ada_b).reshape(DEPTH, N_MOD_SEQ, 6, D_MODEL)
    mods = jnp.pad(mods, ((0, 0), (0, 0), (0, MOD_ROWS - 6), (0, 0)))

    t = tp + ts
    x_src = (x_prompt.reshape(tp, D_MODEL), 0, x_sample.reshape(ts, D_MODEL), 0)

    mod = mods[0]
    (proj,) = _modmm(x_src, t, mod, a_w_in[0].astype(BF16), 0, tp, ls, 0, 1, "mlstm_in_proj")
    xc = _conv_silu(proj, 0, E_A, a_conv_w[0], a_conv_b[0], tp, lp, ls, "mlstm_conv")
    wqk = jnp.concatenate([_blockdiag_lanes(a_w_q[0]), _blockdiag_lanes(a_w_k[0])], axis=-1).astype(BF16)
    wv = _blockdiag_lanes(a_w_v[0]).astype(BF16)
    ngate = 4 * H_A
    wg = jnp.pad(a_w_gates[0].reshape(3, E_A, ngate), ((0, 0), (0, 0), (0, LANES - ngate))).astype(BF16)
    bg = jnp.pad(a_b_gates[0], (0, LANES - ngate)).reshape(1, LANES)
    q, k, v, g = _qkv_gates(xc, proj, wqk, wv, wg, bg)
    gt = g[:, :ngate].T
    hs_p, c_new, n_new, m_new = _mlstm_scan(q, k, v, gt, 0, bp, lp, emit_state=True)
    (hs_s,) = _mlstm_scan(q, k, v, gt, tp, bs, ls,
                          init=(state_mlstm_C[:, 0], state_mlstm_n[:, 0], state_mlstm_m[:, 0]))
    x, hm, comb = _mixer_out(hs_p, hs_s, xc, proj, 1, x_src, t, mod, a_norm_w[0], a_skip[0],
                             a_w_down[0].astype(BF16), ln_w[0, 0], ln_b[0, 0],
                             *_router_weights(moe_w_rg[0], moe_b_rg[0], moe_w_re[0], moe_b_re[0]),
                             tp, ls, True, "mlstm_out")
    x_p, x_s = _moe(hm, comb, *_expert_weights(moe_w_gate[0], moe_w_up[0], moe_w_down[0]), x, mod,
                    ln_w[0, 1], ln_b[0, 1], tp, ls)

    mod = mods[1]
    x_src = (x_p, 0, _grid_to_cols(x_s.reshape(bs, ls, D_MODEL)).reshape(ts, D_MODEL), 0)
    w_in = jnp.pad(b_w_in[0], ((0, 0), (0, -b_w_in.shape[2] % LANES))).astype(BF16)
    proj, dt = _modmm(x_src, t, mod, w_in, LANES, tp, ls, 0, 1, "ssd_in_proj")
    dt = dt[:, :2 * H_B]
    xbc = _conv_silu(proj, E_B, E_B + 2 * GN_B, b_conv_w[0], b_conv_b[0], tp, lp, ls, "ssd_conv")
    lane_bcast = (2, G_B, HG_B, LANES)
    a_neg = jnp.broadcast_to(-jnp.exp(b_a_log[0]).reshape(2, G_B, HG_B, 1), lane_bcast)
    dt_bias = jnp.broadcast_to(b_dt_bias[0].reshape(2, G_B, HG_B, 1), lane_bcast)
    dtr = dt.T.reshape(2, G_B, HG_B, tp + ts)
    y_p, s_new = _ssd_scan(xbc, dtr, a_neg, dt_bias, 0, bp, lp, emit_state=True)
    (y_s,) = _ssd_scan(xbc, dtr, a_neg, dt_bias, tp, bs, ls, init=state_ssd[:, 0])
    x, hm, comb = _mixer_out(y_p, y_s, xbc, proj, 0, x_src, t, mod, b_norm_w[0], jnp.repeat(b_d_skip[0], P_B),
                             b_w_out[0].astype(BF16), ln_w[1, 0], ln_b[1, 0],
                             *_router_weights(moe_w_rg[1], moe_b_rg[1], moe_w_re[1], moe_b_re[1]),
                             tp, ls, False, "ssd_out")
    x_p, x_s = _moe(hm, comb, *_expert_weights(moe_w_gate[1], moe_w_up[1], moe_w_down[1]), x, mod,
                    ln_w[1, 1], ln_b[1, 1], tp, ls)

    y_prompt = x_p.reshape(bp, lp, D_MODEL)
    y_sample = _cols_to_grid(x_s.reshape(bs, ls, D_MODEL))
    return (y_prompt, y_sample,
            c_new[:, None],
            n_new.reshape(bp, 1, 2, H_A, DH_A),
            m_new[:, :, :, 0, 0][:, None],
            s_new[:, None])
```

```python
import functools
import math

import jax
import jax.numpy as jnp
from jax import lax
from jax.experimental import pallas as pl
from jax.experimental.pallas import tpu as pltpu

F32 = jnp.float32
BF16 = jnp.bfloat16

D_MODEL = 1024
DEPTH = 2
GRID_W = 64
CONV_W = 5
E_A = 2 * D_MODEL
H_A = 4
DH_A = E_A // H_A
QKV_BLOCK = 4
E_B = 2 * D_MODEL
P_B = 64
H_B = E_B // P_B
G_B = 4
HG_B = H_B // G_B
D_STATE = 128
GN_B = G_B * D_STATE
MOE_GROUPS = 4
MOE_EXPERTS = 4
N_EXPERTS = MOE_GROUPS * MOE_EXPERTS
D_FF_E = 256
ALPHA = (2.0 * DEPTH) ** 0.25
EPS = 1e-5

LANES = 128
SUBLANES = 8
BF16_ROWS = 16
TM = 512
TM_CONV = 256
TM_MOE = 512
SCAN_CHUNK = 256
MOD_ROWS = 8
N_MOD_SEQ = 16
VMEM_LIMIT = 56 * 1024 * 1024
ROUTER_OFF = MOE_GROUPS


def _params(sem):
    return pltpu.CompilerParams(dimension_semantics=sem, vmem_limit_bytes=VMEM_LIMIT)


def _seq_of_tile(m, tm, tp, ls):
    npt = tp // tm
    return jnp.where(m < npt, 0, 1 + (m - npt) // (ls // tm))


def _silu(x):
    return x * jax.nn.sigmoid(x)


def _softplus(x):
    return jnp.maximum(x, 0.0) + jnp.log1p(jnp.exp(-jnp.abs(x)))


def _layer_norm_rows(r, w, b):
    mu = jnp.mean(r, axis=-1, keepdims=True)
    rc = r - mu
    var = jnp.mean(rc * rc, axis=-1, keepdims=True)
    return rc * lax.rsqrt(var + EPS) * w + b


def _ada_kernel(c_ref, w_ref, b_ref, o_ref):
    s = _silu(c_ref[...])
    o_ref[...] = jnp.dot(s.astype(BF16), w_ref[...].astype(BF16), preferred_element_type=F32) + b_ref[...]


def _ada(cvec, ada_w, ada_b):
    nl = ada_w.shape[0]
    n_out = ada_w.shape[2]
    tn = 1536
    return pl.pallas_call(
        _ada_kernel,
        grid=(nl, n_out // tn),
        in_specs=[pl.BlockSpec((N_MOD_SEQ, D_MODEL), lambda l, n: (0, 0)),
                  pl.BlockSpec((None, D_MODEL, tn), lambda l, n: (l, 0, n)),
                  pl.BlockSpec((None, 1, tn), lambda l, n: (l, 0, n))],
        out_specs=pl.BlockSpec((None, N_MOD_SEQ, tn), lambda l, n: (l, 0, n)),
        out_shape=jax.ShapeDtypeStruct((nl, N_MOD_SEQ, n_out), F32),
        compiler_params=_params(("arbitrary", "arbitrary")),
        name="ada_mod",
    )(cvec, ada_w, ada_b.reshape(nl, 1, n_out))


def _token_tile_specs(src, tm, tp, width=D_MODEL):
    pa, p0, sa, s0 = src
    npt = tp // tm
    nst = (sa.shape[0] - s0) // tm
    return ([pl.BlockSpec((tm, width), lambda m, *_: (p0 // tm + jnp.minimum(m, npt - 1), 0)),
             pl.BlockSpec((tm, width), lambda m, *_: (s0 // tm + jnp.clip(m - npt, 0, nst - 1), 0))], [pa, sa])


def _select_token_tile(p_ref, s_ref, tm, tp):
    return jnp.where(pl.program_id(0) < tp // tm, p_ref[...], s_ref[...])


def _modmm_kernel(xp_ref, xs_ref, mod_ref, w_ref, o_ref, *tail_ref, tm, tp, shift_row, scale_row):
    mod = mod_ref[...]
    x = _select_token_tile(xp_ref, xs_ref, tm, tp)
    h = x * (1.0 + mod[scale_row:scale_row + 1, :]) + mod[shift_row:shift_row + 1, :]
    res = jnp.dot(h.astype(BF16), w_ref[...], preferred_element_type=F32)
    n_main = o_ref.shape[1]
    o_ref[...] = res[:, :n_main].astype(o_ref.dtype)
    if tail_ref:
        tail_ref[0][...] = res[:, n_main:]


def _resident(shape):
    return pl.BlockSpec(shape, lambda *_: (0,) * len(shape), pipeline_mode=pl.Buffered(1))


def _modmm(src, t, mod, w_bf, n_tail, tp, ls, shift_row, scale_row, name):
    n_main = w_bf.shape[1] - n_tail
    tm = TM
    x_specs, x_args = _token_tile_specs(src, tm, tp)
    out_specs = [pl.BlockSpec((tm, n_main), lambda m: (m, 0))]
    out_shape = [jax.ShapeDtypeStruct((t, n_main), BF16)]
    if n_tail:
        out_specs.append(pl.BlockSpec((tm, n_tail), lambda m: (m, 0)))
        out_shape.append(jax.ShapeDtypeStruct((t, n_tail), F32))
    return pl.pallas_call(
        functools.partial(_modmm_kernel, tm=tm, tp=tp, shift_row=shift_row, scale_row=scale_row),
        grid=(t // tm,),
        in_specs=x_specs + [
            pl.BlockSpec((None, MOD_ROWS, D_MODEL), lambda m: (_seq_of_tile(m, tm, tp, ls), 0, 0)),
            _resident((D_MODEL, w_bf.shape[1]))],
        out_specs=out_specs,
        out_shape=out_shape,
        compiler_params=_params(("arbitrary",)),
        name=name,
    )(*x_args, mod, w_bf)


def _conv_kernel(cur_ref, prev_ref, next_ref, w_ref, b_ref, o_ref, ext_ref, *, tm, tp, lp, ls):
    m = pl.program_id(0)
    npt = tp // tm
    pos = jnp.where(m < npt, m % (lp // tm), (m - npt) % (ls // tm))
    last_pos = jnp.where(m < npt, lp // tm - 1, ls // tm - 1)
    halo = SUBLANES
    prev = jnp.where(pos == 0, 0.0, prev_ref[...].astype(F32)[BF16_ROWS - halo:, :])
    cur = cur_ref[...].astype(F32)
    nxt = jnp.where(pos == last_pos, 0.0, next_ref[...].astype(F32)[:halo, :])
    first_tap = jnp.minimum(m, 0) + (halo - CONV_W // 2)
    for c in range(ext_ref.shape[0]):
        lanes = slice(c * LANES, (c + 1) * LANES)
        ext_ref[c, 0:halo, :] = prev[:, lanes]
        ext_ref[c, halo:halo + tm, :] = cur[:, lanes]
        ext_ref[c, halo + tm:2 * halo + tm, :] = nxt[:, lanes]
    for c in range(ext_ref.shape[0]):
        lanes = slice(c * LANES, (c + 1) * LANES)
        acc = jnp.zeros((tm, LANES), F32) + b_ref[:, lanes]
        for j in range(CONV_W):
            acc = acc + ext_ref[c, pl.ds(first_tap + j, tm), :] * w_ref[j:j + 1, lanes]
        o_ref[:, lanes] = _silu(acc).astype(o_ref.dtype)


def _conv_silu(a, col0, width, w, b, tp, lp, ls, name):
    t = a.shape[0]
    tm, tc = TM_CONV, 1024
    cb0 = col0 // tc
    nhalo = t // BF16_ROWS
    per = tm // BF16_ROWS
    return pl.pallas_call(
        functools.partial(_conv_kernel, tm=tm, tp=tp, lp=lp, ls=ls),
        grid=(t // tm, width // tc),
        in_specs=[pl.BlockSpec((tm, tc), lambda m, j: (m, cb0 + j)),
                  pl.BlockSpec((BF16_ROWS, tc), lambda m, j: (jnp.maximum(m * per - 1, 0), cb0 + j)),
                  pl.BlockSpec((BF16_ROWS, tc), lambda m, j: (jnp.minimum((m + 1) * per, nhalo - 1), cb0 + j)),
                  pl.BlockSpec((CONV_W, tc), lambda m, j: (0, j)),
                  pl.BlockSpec((1, tc), lambda m, j: (0, j))],
        out_specs=pl.BlockSpec((tm, tc), lambda m, j: (m, j)),
        out_shape=jax.ShapeDtypeStruct((t, width), BF16),
        scratch_shapes=[pltpu.VMEM((tc // LANES, tm + 2 * SUBLANES, LANES), F32)],
        compiler_params=_params(("arbitrary", "arbitrary")),
        name=name,
    )(a, a, a, w, b.reshape(1, width))


def _qkv_kernel(xc_ref, xu_ref, wqk_ref, wv_ref, wg_ref, bg_ref, q_ref, k_ref, v_ref, g_ref):
    nblk = E_A // LANES
    g = jnp.zeros(g_ref.shape, F32) + bg_ref[...]
    for b in range(nblk):
        sl = slice(b * LANES, (b + 1) * LANES)
        xcb = xc_ref[:, sl]
        xub = xu_ref[:, sl]
        qk = jnp.dot(xcb, wqk_ref[b], preferred_element_type=F32)
        vv = jnp.dot(xub, wv_ref[b], preferred_element_type=F32)
        qb = qk[:, :LANES].astype(BF16)
        kb = qk[:, LANES:].astype(BF16)
        vb = vv.astype(BF16)
        q_ref[:, sl] = qb
        k_ref[:, sl] = kb
        v_ref[:, sl] = vb
        g = g + jnp.dot(qb, wg_ref[0, sl, :], preferred_element_type=F32)
        g = g + jnp.dot(kb, wg_ref[1, sl, :], preferred_element_type=F32)
        g = g + jnp.dot(vb, wg_ref[2, sl, :], preferred_element_type=F32)
    lane = lax.broadcasted_iota(jnp.int32, g.shape, 1)
    is_forget = (lane & (2 * H_A - 1)) >= H_A
    log_sig = jnp.minimum(g, 0.0) - jnp.log1p(jnp.exp(-jnp.abs(g)))
    g_ref[...] = jnp.where(is_forget, log_sig, g)


def _qkv_gates(xc, proj, wqk, wv, wg, bg):
    t = xc.shape[0]
    nblk = E_A // LANES
    tile = pl.BlockSpec((TM, E_A), lambda m: (m, 0))
    return pl.pallas_call(
        _qkv_kernel,
        grid=(t // TM,),
        in_specs=[tile, tile,
                  pl.BlockSpec((nblk, LANES, 2 * LANES), lambda m: (0, 0, 0)),
                  pl.BlockSpec((nblk, LANES, LANES), lambda m: (0, 0, 0)),
                  pl.BlockSpec((3, E_A, LANES), lambda m: (0, 0, 0)),
                  pl.BlockSpec((1, LANES), lambda m: (0, 0))],
        out_specs=[tile, tile, tile, pl.BlockSpec((TM, LANES), lambda m: (m, 0))],
        out_shape=[jax.ShapeDtypeStruct((t, E_A), BF16)] * 3 + [jax.ShapeDtypeStruct((t, LANES), F32)],
        compiler_params=_params(("arbitrary",)),
        name="mlstm_qkv_gates",
    )(xc, proj, wqk, wv, wg, bg)


def _order_mask(dirn, n, rows_are_later):
    a_io = lax.broadcasted_iota(jnp.int32, (n, n), 0)
    b_io = lax.broadcasted_iota(jnp.int32, (n, n), 1)
    if (dirn == 0) == rows_are_later:
        return b_io <= a_io
    return b_io >= a_io


def _exact_cumsum_rows(rows, tri):
    nrow = rows.shape[0]
    p1 = rows.astype(BF16).astype(F32)
    r1 = rows - p1
    p2 = r1.astype(BF16).astype(F32)
    p3 = r1 - p2
    pieces = jnp.concatenate([p1, p2, p3, jnp.zeros_like(p1)], axis=0).astype(BF16)
    cs = jnp.dot(pieces, tri, preferred_element_type=F32)
    return cs[0:nrow] + cs[nrow:2 * nrow] + cs[2 * nrow:3 * nrow]


def _pick_row(blk, idx):
    sub = lax.broadcasted_iota(jnp.int32, (blk.shape[0], 1), 0)
    return jnp.sum(jnp.where(sub == idx, blk, 0.0), axis=0, keepdims=True)


def _rows_to_cols(rows):
    pad = jnp.zeros((LANES - rows.shape[0], rows.shape[1]), F32)
    return jnp.concatenate([rows, pad], axis=0).T


def _pick_col(blk, idx):
    lane = lax.broadcasted_iota(jnp.int32, (1, blk.shape[1]), 1)
    return jnp.sum(jnp.where(lane == idx, blk, 0.0), axis=1, keepdims=True)


def _mlstm_direction(dirn, h, q, k, v, s_qk, kt, gt_ref, c_ref, n_ref, m_ref, use_state, need_update):
    lc = SCAN_CHUNK
    scale = DH_A ** -0.5
    log_scale = -0.5 * math.log(DH_A)
    gates = gt_ref[dirn * 2 * H_A:(dirn + 1) * 2 * H_A, :]
    tri = jnp.where(_order_mask(dirn, lc, False), 1.0, 0.0).astype(BF16)
    csum = _exact_cumsum_rows(gates, tri)
    i_row = _pick_row(gates, h)
    f_row = _pick_row(gates, H_A + h)
    b_row = _pick_row(csum, H_A + h)
    v_row = i_row - b_row
    valid = _order_mask(dirn, lc, True)
    m_prev = m_ref[:, 0:1]
    run_max = jnp.max(jnp.where(valid, v_row, -jnp.inf), axis=1, keepdims=True)
    u_col = -jnp.maximum(m_prev, run_max)
    p = s_qk * jnp.exp(jnp.where(valid, u_col + (v_row + log_scale), -jnp.inf))
    num = jnp.dot(p.astype(BF16), v, preferred_element_type=F32)
    den = jnp.sum(p, axis=1, keepdims=True)
    if use_state:
        w_inter = jnp.exp(m_prev + u_col) * scale
        qc = jnp.dot(q, c_ref[...].astype(BF16), preferred_element_type=F32)
        qn = jnp.sum(q.astype(F32) * n_ref[...], axis=1, keepdims=True)
        num = num + w_inter * qc
        den = den + w_inter * qn
    b_col = _rows_to_cols(b_row)[:, 0:1]
    h_out = num / jnp.maximum(jnp.abs(den), jnp.exp(u_col - b_col))
    if not need_update:
        return h_out, None
    b_last = jnp.sum(f_row, axis=1, keepdims=True)
    logk = b_last + v_row
    m_new = jnp.maximum(b_last + m_prev, jnp.max(logk, axis=1, keepdims=True))
    wk = jnp.exp(logk - m_new)
    c_new = jnp.dot((kt * wk).astype(BF16), v, preferred_element_type=F32)
    wk8 = (jnp.zeros((SUBLANES, lc), F32) + wk).astype(BF16)
    n_new = jnp.dot(wk8, k, preferred_element_type=F32)[0:1, :]
    return h_out, (c_new, n_new, m_new, jnp.exp(b_last + m_prev - m_new))


def _mlstm_kernel(*refs, nc, has_init, emit_state):
    shared = nc == 1
    pos = 4 if shared else 8
    chunk_refs = [refs[0:4], refs[0:4] if shared else refs[4:8]]
    if has_init:
        c0_ref, n0_ref, m0_ref = refs[pos:pos + 3]
        pos += 3
    hs_ref = refs[pos]
    pos += 1
    if emit_state:
        cn_ref, nn_ref, mn_ref = refs[pos:pos + 3]
        pos += 3
    c_scr, n_scr, m_scr = refs[pos:pos + 3]

    lc = SCAN_CHUNK
    s_id, h, c = (pl.program_id(i) for i in range(3))
    use_state = has_init or nc > 1

    @pl.when(c == 0)
    def _():
        for dirn in range(2):
            if has_init:
                c_scr[dirn] = c0_ref[dirn]
                n_scr[dirn] = _pick_row(n0_ref[dirn], h)
                m_scr[dirn] = jnp.zeros(m_scr.shape[1:], F32) + m0_ref[(s_id * 2 + dirn) * H_A + h]
            else:
                c_scr[dirn] = jnp.zeros(c_scr.shape[1:], F32)
                n_scr[dirn] = jnp.zeros(n_scr.shape[1:], F32)
                m_scr[dirn] = jnp.zeros(m_scr.shape[1:], F32)
        if not shared:
            hs_ref[...] = jnp.zeros(hs_ref.shape, F32)

    def chunk_operands(q_ref, k_ref, v_ref, gt_ref):
        q = q_ref[...]
        k = k_ref[...]
        s_qk = lax.dot_general(q, k, (((1,), (1,)), ((), ())), preferred_element_type=F32)
        return q, k, v_ref[...], s_qk, k.astype(F32).T, gt_ref

    ops = [chunk_operands(*chunk_refs[0])]
    ops.append(ops[0] if shared else chunk_operands(*chunk_refs[1]))

    def run(need_update):
        return [_mlstm_direction(dirn, h, *ops[dirn], c_scr.at[dirn], n_scr.at[dirn], m_scr.at[dirn],
                                 use_state, need_update) for dirn in range(2)]

    def write_out(res):
        if shared:
            hs_ref[...] = res[0][0] + res[1][0]
        else:
            rows_f = pl.ds(pl.multiple_of(c * lc, lc), lc)
            rows_b = pl.ds(pl.multiple_of((nc - 1 - c) * lc, lc), lc)
            hs_ref[rows_f, :] = hs_ref[rows_f, :] + res[0][0]
            hs_ref[rows_b, :] = hs_ref[rows_b, :] + res[1][0]

    def with_update():
        res = run(True)
        write_out(res)
        for dirn in range(2):
            c_new, n_new, m_new, decay = res[dirn][1]
            if use_state:
                c_new = c_new + decay * c_scr[dirn]
                n_new = n_new + decay * n_scr[dirn]
            m_new_row = jnp.zeros(m_scr.shape[1:], F32) + m_new
            if nc > 1:
                c_scr[dirn] = c_new
                n_scr[dirn] = n_new
                m_scr[dirn] = m_new_row
            if emit_state:
                @pl.when(c == nc - 1)
                def _():
                    cn_ref[dirn] = c_new
                    nn_ref[dirn] = n_new
                    mn_ref[dirn] = m_new_row

    def without_update():
        write_out(run(False))

    if emit_state:
        with_update()
    else:
        pl.when(c < nc - 1)(with_update)
        pl.when(c == nc - 1)(without_update)


def _mlstm_scan(q, k, v, gt, row0, nseq, seqlen, init=None, emit_state=False):
    lc = SCAN_CHUNK
    nc = seqlen // lc
    blk0 = row0 // lc
    has_init = init is not None

    def blk_f(s, c):
        return blk0 + s * nc + c

    def blk_b(s, c):
        return blk0 + s * nc + nc - 1 - c

    def chunk_specs(blk):
        return [pl.BlockSpec((lc, DH_A), lambda s, h, c: (blk(s, c), h))] * 3 + [
            pl.BlockSpec((4 * H_A, lc), lambda s, h, c: (0, blk(s, c)))]

    in_specs = chunk_specs(blk_f)
    args = [q, k, v, gt]
    if nc > 1:
        in_specs += chunk_specs(blk_b)
        args += [q, k, v, gt]
    if has_init:
        c0, n0, m0 = init
        in_specs += [pl.BlockSpec((None, 2, None, DH_A, DH_A), lambda s, h, c: (s, 0, h, 0, 0)),
                     pl.BlockSpec((None, 2, H_A, DH_A), lambda s, h, c: (s, 0, 0, 0)),
                     pl.BlockSpec(memory_space=pltpu.SMEM)]
        args += [c0, n0, m0.reshape(-1)]
    out_specs = [pl.BlockSpec((seqlen, DH_A), lambda s, h, c: (s, h))]
    out_shape = [jax.ShapeDtypeStruct((nseq * seqlen, E_A), F32)]
    if emit_state:
        out_specs += [pl.BlockSpec((None, 2, None, DH_A, DH_A), lambda s, h, c: (s, 0, h, 0, 0)),
                      pl.BlockSpec((None, 2, None, 1, DH_A), lambda s, h, c: (s, 0, h, 0, 0)),
                      pl.BlockSpec((None, 2, None, 1, LANES), lambda s, h, c: (s, 0, h, 0, 0))]
        out_shape += [jax.ShapeDtypeStruct((nseq, 2, H_A, DH_A, DH_A), F32),
                      jax.ShapeDtypeStruct((nseq, 2, H_A, 1, DH_A), F32),
                      jax.ShapeDtypeStruct((nseq, 2, H_A, 1, LANES), F32)]
    return pl.pallas_call(
        functools.partial(_mlstm_kernel, nc=nc, has_init=has_init, emit_state=emit_state),
        grid=(nseq, H_A, nc),
        in_specs=in_specs,
        out_specs=out_specs,
        out_shape=out_shape,
        scratch_shapes=[pltpu.VMEM((2, DH_A, DH_A), F32), pltpu.VMEM((2, 1, DH_A), F32),
                        pltpu.VMEM((2, 1, LANES), F32)],
        compiler_params=_params(("arbitrary",) * 3),
        name="mlstm_scan_state" if emit_state else "mlstm_scan_init",
    )(*args)


def _ssd_direction(dirn, xt, bb, ct, cbt_raw, dtr_ref, a_ref, bias_ref, h_ref, yt_ref, xw_ref, use_state):
    lc = SCAN_CHUNK
    r_io = lax.broadcasted_iota(jnp.int32, (lc, lc), 0)
    s_io = lax.broadcasted_iota(jnp.int32, (lc, lc), 1)
    valid = (r_io <= s_io) if dirn == 0 else (r_io >= s_io)
    tri = jnp.where(valid, 1.0, 0.0).astype(BF16)

    dt_rows = _softplus(dtr_ref[...] + bias_ref[:, 0:1])
    a_rows = dt_rows * a_ref[:, 0:1]
    a1 = a_rows.astype(BF16).astype(F32)
    r1 = a_rows - a1
    a2 = r1.astype(BF16).astype(F32)
    a3 = r1 - a2
    pieces = jnp.concatenate([a1, a2, a3, jnp.zeros_like(a1)], axis=0).astype(BF16)
    cs = jnp.dot(pieces, tri, preferred_element_type=F32)
    acs_rows = cs[0:HG_B] + cs[HG_B:2 * HG_B] + cs[2 * HG_B:3 * HG_B]
    a_last = jnp.sum(a_rows, axis=1, keepdims=True)
    wr_rows = jnp.exp(a_last - acs_rows) * dt_rows
    acs_cols = jnp.concatenate([acs_rows, jnp.zeros((LANES - HG_B, lc), F32)], axis=0).T

    cbt = jnp.where(valid, cbt_raw, 0.0)
    if use_state:
        inter_t = jnp.dot(h_ref[...].astype(BF16), ct, preferred_element_type=F32)
        e_rows = jnp.exp(acs_rows)
    for hg in range(HG_B):
        sl = slice(hg * P_B, (hg + 1) * P_B)
        row = slice(hg, hg + 1)
        seg_t = acs_rows[row, :] - acs_cols[:, hg:hg + 1]
        wt = cbt * jnp.exp(jnp.minimum(seg_t, 0.0))
        xth = xt[sl, :]
        yth = jnp.dot((xth * dt_rows[row, :]).astype(BF16), wt.astype(BF16), preferred_element_type=F32)
        if use_state:
            yth = yth + inter_t[sl, :] * e_rows[row, :]
        yt_ref[sl, :] = yth
        xw_ref[sl, :] = xth * wr_rows[row, :]
    return a_last


def _ssd_kernel(*refs, nc, has_init, emit_state):
    shared = nc == 1
    pos = 3 if shared else 6
    chunk_refs = [refs[0:3], refs[0:3] if shared else refs[3:6]]
    dtr_refs = refs[pos:pos + 2]
    a_ref, bias_ref = refs[pos + 2:pos + 4]
    pos += 4
    if has_init:
        h0_ref = refs[pos]
        pos += 1
    y_ref = refs[pos]
    pos += 1
    if emit_state:
        sn_ref = refs[pos]
        pos += 1
    h_scr, yt_scr, xw_scr = refs[pos:pos + 3]

    lc = SCAN_CHUNK
    c = pl.program_id(2)
    use_state = has_init or nc > 1
    hp = HG_B * P_B

    @pl.when(c == 0)
    def _():
        if has_init:
            h_scr[...] = h0_ref[...].reshape(2, hp, D_STATE)
        else:
            h_scr[...] = jnp.zeros(h_scr.shape, F32)
        if not shared:
            y_ref[...] = jnp.zeros(y_ref.shape, F32)

    def chunk_operands(x_ref, b_ref, c_ref):
        bb = b_ref[...]
        ct = c_ref[...].astype(F32).T.astype(BF16)
        return x_ref[...].astype(F32).T, bb, ct, jnp.dot(bb, ct, preferred_element_type=F32)

    ops = [chunk_operands(*chunk_refs[0])]
    ops.append(ops[0] if shared else chunk_operands(*chunk_refs[1]))
    a_last = [_ssd_direction(dirn, *ops[dirn], dtr_refs[dirn], a_ref.at[dirn], bias_ref.at[dirn], h_scr.at[dirn],
                             yt_scr.at[dirn], xw_scr.at[dirn], use_state) for dirn in range(2)]

    if shared:
        y_ref[...] = (yt_scr[0] + yt_scr[1]).T
    else:
        rows_f = pl.ds(pl.multiple_of(c * lc, lc), lc)
        rows_b = pl.ds(pl.multiple_of((nc - 1 - c) * lc, lc), lc)
        y_ref[rows_f, :] = y_ref[rows_f, :] + yt_scr[0].T
        y_ref[rows_b, :] = y_ref[rows_b, :] + yt_scr[1].T

    def update_state():
        for dirn in range(2):
            upd = jnp.dot(xw_scr[dirn].astype(BF16), ops[dirn][1], preferred_element_type=F32)
            dec = jnp.exp(a_last[dirn])
            for hg in range(HG_B):
                sl = slice(hg * P_B, (hg + 1) * P_B)
                new = upd[sl, :]
                if use_state:
                    new = new + dec[hg:hg + 1, :] * h_scr[dirn, sl, :]
                h_scr[dirn, sl, :] = new
        if emit_state:
            @pl.when(c == nc - 1)
            def _():
                sn_ref[...] = h_scr[...].reshape(2, HG_B, P_B, D_STATE)

    if emit_state:
        update_state()
    elif nc > 1:
        pl.when(c < nc - 1)(update_state)


def _ssd_scan(xbc, dtr, a_neg, dt_bias, row0, nseq, seqlen, init=None, emit_state=False):
    lc = SCAN_CHUNK
    nc = seqlen // lc
    blk0 = row0 // lc
    has_init = init is not None
    hp = HG_B * P_B
    b_off = E_B // D_STATE
    c_off = (E_B + GN_B) // D_STATE

    def blk_f(s, c):
        return blk0 + s * nc + c

    def blk_b(s, c):
        return blk0 + s * nc + nc - 1 - c

    def chunk_specs(blk):
        return [pl.BlockSpec((lc, hp), lambda s, g, c: (blk(s, c), g)),
                pl.BlockSpec((lc, D_STATE), lambda s, g, c: (blk(s, c), b_off + g)),
                pl.BlockSpec((lc, D_STATE), lambda s, g, c: (blk(s, c), c_off + g))]

    in_specs = chunk_specs(blk_f)
    args = [xbc, xbc, xbc]
    if nc > 1:
        in_specs += chunk_specs(blk_b)
        args += [xbc, xbc, xbc]
    table_spec = pl.BlockSpec((2, None, HG_B, LANES), lambda s, g, c: (0, g, 0, 0))
    in_specs += [pl.BlockSpec((None, None, HG_B, lc), lambda s, g, c: (0, g, 0, blk_f(s, c))),
                 pl.BlockSpec((None, None, HG_B, lc), lambda s, g, c: (1, g, 0, blk_b(s, c))),
                 table_spec, table_spec]
    args += [dtr, dtr, a_neg, dt_bias]
    state_spec = pl.BlockSpec((None, 2, HG_B, P_B, D_STATE), lambda s, g, c: (s, 0, g, 0, 0))
    if has_init:
        in_specs.append(state_spec)
        args.append(init)
    out_specs = [pl.BlockSpec((seqlen, hp), lambda s, g, c: (s, g))]
    out_shape = [jax.ShapeDtypeStruct((nseq * seqlen, E_B), F32)]
    if emit_state:
        out_specs.append(state_spec)
        out_shape.append(jax.ShapeDtypeStruct((nseq, 2, H_B, P_B, D_STATE), F32))
    return pl.pallas_call(
        functools.partial(_ssd_kernel, nc=nc, has_init=has_init, emit_state=emit_state),
        grid=(nseq, G_B, nc),
        in_specs=in_specs,
        out_specs=out_specs,
        out_shape=out_shape,
        scratch_shapes=[pltpu.VMEM((2, hp, D_STATE), F32), pltpu.VMEM((2, hp, lc), F32),
                        pltpu.VMEM((2, hp, lc), F32)],
        compiler_params=_params(("arbitrary",) * 3),
        name="ssd_scan_state" if emit_state else "ssd_scan_init",
    )(*args)


def _mixer_out_kernel(yp_ref, ys_ref, u_ref, z_ref, xp_ref, xs_ref, mod_ref, nw_ref, sk_ref, w_ref, lnw_ref, lnb_ref,
                      wr_ref, br_ref, o_ref, hm_ref, comb_ref, *, tp, grouped):
    y = _select_token_tile(yp_ref, ys_ref, TM, tp)
    u = u_ref[...].astype(F32)
    gate = _silu(z_ref[...].astype(F32))
    if grouped:
        parts = []
        for hd in range(H_A):
            seg = y[:, hd * DH_A:(hd + 1) * DH_A]
            mu = jnp.mean(seg, axis=-1, keepdims=True)
            sc = seg - mu
            var = jnp.mean(sc * sc, axis=-1, keepdims=True)
            parts.append(sc * lax.rsqrt(var + EPS))
        hn = jnp.concatenate(parts, axis=1) * nw_ref[...]
        t = (hn + sk_ref[...] * u) * gate
    else:
        t0 = (y + sk_ref[...] * u) * gate
        t = t0 * lax.rsqrt(jnp.mean(t0 * t0, axis=-1, keepdims=True) + EPS) * nw_ref[...]
    out = jnp.dot(t.astype(BF16), w_ref[...], preferred_element_type=F32)
    mod = mod_ref[...]
    r = ALPHA * _select_token_tile(xp_ref, xs_ref, TM, tp) + mod[2:3, :] * out
    x_new = _layer_norm_rows(r, lnw_ref[...], lnb_ref[...])
    o_ref[...] = x_new
    hm_ref[...], comb_ref[...] = _route(x_new, mod, wr_ref[...], br_ref[...])


def _mixer_out(yp, ys, u, proj, zblk, x_src, t, mod, nw, sk, w_bf, lnw, lnb, w_r, b_r, tp, ls, grouped, name):
    e = w_bf.shape[0]
    y_specs, y_args = _token_tile_specs((yp, 0, ys, 0), TM, tp, e)
    x_specs, x_args = _token_tile_specs(x_src, TM, tp)
    vec_e = pl.BlockSpec((1, e), lambda m: (0, 0))
    vec_d = pl.BlockSpec((1, D_MODEL), lambda m: (0, 0))
    return pl.pallas_call(
        functools.partial(_mixer_out_kernel, tp=tp, grouped=grouped),
        grid=(t // TM,),
        in_specs=y_specs + [
                  pl.BlockSpec((TM, e), lambda m: (m, 0)),
                  pl.BlockSpec((TM, e), lambda m: (m, zblk))] + x_specs + [
                  pl.BlockSpec((None, MOD_ROWS, D_MODEL), lambda m: (_seq_of_tile(m, TM, tp, ls), 0, 0)),
                  vec_e, vec_e,
                  pl.BlockSpec((e, D_MODEL), lambda m: (0, 0)),
                  vec_d, vec_d,
                  pl.BlockSpec((D_MODEL, LANES), lambda m: (0, 0)),
                  pl.BlockSpec((1, LANES), lambda m: (0, 0))],
        out_specs=[pl.BlockSpec((TM, D_MODEL), lambda m: (m, 0)),
                   pl.BlockSpec((TM, D_MODEL), lambda m: (m, 0)),
                   pl.BlockSpec((TM, LANES), lambda m: (m, 0))],
        out_shape=[jax.ShapeDtypeStruct((t, D_MODEL), F32), jax.ShapeDtypeStruct((t, D_MODEL), BF16),
                   jax.ShapeDtypeStruct((t, LANES), F32)],
        compiler_params=_params(("arbitrary",)),
        name=name,
    )(*y_args, u, proj, *x_args, mod, nw.reshape(1, e), sk.reshape(1, e), w_bf, lnw.reshape(1, D_MODEL),
      lnb.reshape(1, D_MODEL), w_r, b_r)


def _route(x, mod, w_r, b_r):
    hm = (x * (1.0 + mod[4:5, :]) + mod[3:4, :]).astype(BF16)
    logits = jnp.dot(hm, w_r, preferred_element_type=F32) + b_r
    n_used = ROUTER_OFF + N_EXPERTS
    n_rows = -(-n_used // SUBLANES) * SUBLANES
    lt = logits.T[:n_rows, :]
    row = lax.broadcasted_iota(jnp.int32, lt.shape, 0).astype(F32)
    big = float(2 * LANES)
    glog = jnp.where(row < MOE_GROUPS, lt, -jnp.inf)
    ge = jnp.exp(glog - jnp.max(glog, axis=0, keepdims=True))
    prob = ge / jnp.sum(ge, axis=0, keepdims=True)
    gp = jnp.max(prob, axis=0, keepdims=True)
    gi = jnp.min(jnp.where(prob == gp, row, big), axis=0, keepdims=True)
    lo = ROUTER_OFF + gi * MOE_EXPERTS
    ev = jnp.where((row >= lo) & (row < lo + MOE_EXPERTS), lt, -jnp.inf)
    e1 = jnp.max(ev, axis=0, keepdims=True)
    i1 = jnp.min(jnp.where(ev == e1, row, big), axis=0, keepdims=True)
    ev2 = jnp.where(row == i1, -jnp.inf, ev)
    e2 = jnp.max(ev2, axis=0, keepdims=True)
    i2 = jnp.min(jnp.where(ev2 == e2, row, big), axis=0, keepdims=True)
    t2 = jnp.exp(e2 - e1)
    w1 = 1.0 / (1.0 + t2)
    w2 = t2 / (1.0 + t2)
    comb_t = jnp.where(row == i1, w1 * gp, jnp.where(row == i2, w2 * gp, 0.0))
    return hm, _rows_to_cols(comb_t)


def _moe_kernel(hm_ref, comb_ref, wg_ref, wu_ref, wd_ref, x_ref, mod_ref, lnw_ref, lnb_ref, op_ref, os_ref, *, npt):
    hm = hm_ref[...]
    comb = comb_ref[...]
    acc = None
    for g in range(MOE_GROUPS):
        a = _silu(jnp.dot(hm, wg_ref[g], preferred_element_type=F32)) * jnp.dot(
            hm, wu_ref[g], preferred_element_type=F32)
        parts = []
        for e in range(MOE_EXPERTS):
            lane = ROUTER_OFF + g * MOE_EXPERTS + e
            parts.append(a[:, e * D_FF_E:(e + 1) * D_FF_E] * comb[:, lane:lane + 1])
        part = jnp.dot(jnp.concatenate(parts, axis=1).astype(BF16), wd_ref[g], preferred_element_type=F32)
        acc = part if acc is None else acc + part
    mod = mod_ref[...]
    r = ALPHA * x_ref[...] + mod[5:6, :] * acc
    y = _layer_norm_rows(r, lnw_ref[...], lnb_ref[...])
    m = pl.program_id(0)

    @pl.when(m < npt)
    def _():
        op_ref[...] = y

    @pl.when(m >= npt)
    def _():
        os_ref[...] = y


def _moe(hm, comb, wg, wu, wd, x, mod, lnw, lnb, tp, ls):
    t = x.shape[0]
    tm = TM_MOE
    npt = tp // tm
    nst = (t - tp) // tm
    gf = MOE_EXPERTS * D_FF_E
    vec_d = pl.BlockSpec((1, D_MODEL), lambda m: (0, 0))
    return pl.pallas_call(
        functools.partial(_moe_kernel, npt=npt),
        grid=(t // tm,),
        in_specs=[pl.BlockSpec((tm, D_MODEL), lambda m: (m, 0)),
                  pl.BlockSpec((tm, LANES), lambda m: (m, 0)),
                  _resident((MOE_GROUPS, D_MODEL, gf)),
                  _resident((MOE_GROUPS, D_MODEL, gf)),
                  _resident((MOE_GROUPS, gf, D_MODEL)),
                  pl.BlockSpec((tm, D_MODEL), lambda m: (m, 0)),
                  pl.BlockSpec((None, MOD_ROWS, D_MODEL), lambda m: (_seq_of_tile(m, tm, tp, ls), 0, 0)),
                  vec_d, vec_d],
        out_specs=[pl.BlockSpec((tm, D_MODEL), lambda m: (jnp.minimum(m, npt - 1), 0)),
                   pl.BlockSpec((tm, D_MODEL), lambda m: (jnp.clip(m - npt, 0, nst - 1), 0))],
        out_shape=[jax.ShapeDtypeStruct((tp, D_MODEL), F32), jax.ShapeDtypeStruct((t - tp, D_MODEL), F32)],
        compiler_params=_params(("arbitrary",)),
        name="moe_experts",
    )(hm, comb, wg, wu, wd, x, mod, lnw.reshape(1, D_MODEL), lnb.reshape(1, D_MODEL))


def _router_weights(w_rg, b_rg, w_re, b_re):
    w_r = jnp.concatenate([w_rg, jnp.moveaxis(w_re, 0, 1).reshape(D_MODEL, N_EXPERTS)], axis=1)
    w_r = jnp.pad(w_r, ((0, 0), (0, LANES - w_r.shape[1]))).astype(BF16)
    b_r = jnp.pad(jnp.concatenate([b_rg, b_re.reshape(-1)]), (0, LANES - MOE_GROUPS - N_EXPERTS)).reshape(1, LANES)
    return w_r, b_r


def _expert_weights(w_gate, w_up, w_down):
    gf = MOE_EXPERTS * D_FF_E
    wg = w_gate.transpose(0, 2, 1, 3).reshape(MOE_GROUPS, D_MODEL, gf).astype(BF16)
    wu = w_up.transpose(0, 2, 1, 3).reshape(MOE_GROUPS, D_MODEL, gf).astype(BF16)
    wd = w_down.reshape(MOE_GROUPS, gf, D_MODEL).astype(BF16)
    return wg, wu, wd


def _blockdiag_lanes(w):
    per = LANES // QKV_BLOCK
    wb = w.reshape(-1, per, QKV_BLOCK, QKV_BLOCK)
    eye = jnp.eye(per, dtype=w.dtype)
    return jnp.einsum('bncd,nm->bncmd', wb, eye).reshape(-1, LANES, LANES)


def _grid_to_cols(x):
    b, l, dd = x.shape
    return x.reshape(b, l // GRID_W, GRID_W, dd).transpose(0, 2, 1, 3).reshape(b, l, dd)


def _cols_to_grid(x):
    b, l, dd = x.shape
    return x.reshape(b, GRID_W, l // GRID_W, dd).transpose(0, 2, 1, 3).reshape(b, l, dd)


def kernel(x_prompt, x_sample, state_mlstm_C, state_mlstm_n, state_mlstm_m, state_ssd, c, c_ctx, ada_w, ada_b, ln_w, ln_b, a_w_in, a_conv_w, a_conv_b, a_w_q, a_w_k, a_w_v, a_w_gates, a_b_gates, a_norm_w, a_skip, a_w_down, b_w_in, b_conv_w, b_conv_b, b_dt_bias, b_a_log, b_d_skip, b_norm_w, b_w_out, moe_w_rg, moe_b_rg, moe_w_re, moe_b_re, moe_w_gate, moe_w_up, moe_w_down):
    bp, lp, _ = x_prompt.shape
    bs, ls, _ = x_sample.shape
    tp, ts = bp * lp, bs * ls
    assert lp % SCAN_CHUNK == 0 and ls % SCAN_CHUNK == 0 and tp % 512 == 0 and ls % 512 == 0
    assert bs + 1 <= N_MOD_SEQ

    cvec = jnp.concatenate([c_ctx[None, :], c, jnp.zeros((N_MOD_SEQ - 1 - bs, D_MODEL), F32)], axis=0)
    mods = _ada(cvec, ada_w, ada_b).reshape(DEPTH, N_MOD_SEQ, 6, D_MODEL)
    mods = jnp.pad(mods, ((0, 0), (0, 0), (0, MOD_ROWS - 6), (0, 0)))

    t = tp + ts
    x_src = (x_prompt.reshape(tp, D_MODEL), 0, x_sample.reshape(ts, D_MODEL), 0)

    mod = mods[0]
    (proj,) = _modmm(x_src, t, mod, a_w_in[0].astype(BF16), 0, tp, ls, 0, 1, "mlstm_in_proj")
    xc = _conv_silu(proj, 0, E_A, a_conv_w[0], a_conv_b[0], tp, lp, ls, "mlstm_conv")
    wqk = jnp.concatenate([_blockdiag_lanes(a_w_q[0]), _blockdiag_lanes(a_w_k[0])], axis=-1).astype(BF16)
    wv = _blockdiag_lanes(a_w_v[0]).astype(BF16)
    ngate = 4 * H_A
    wg = jnp.pad(a_w_gates[0].reshape(3, E_A, ngate), ((0, 0), (0, 0), (0, LANES - ngate))).astype(BF16)
    bg = jnp.pad(a_b_gates[0], (0, LANES - ngate)).reshape(1, LANES)
    q, k, v, g = _qkv_gates(xc, proj, wqk, wv, wg, bg)
    gt = g[:, :ngate].T
    hs_p, c_new, n_new, m_new = _mlstm_scan(q, k, v, gt, 0, bp, lp, emit_state=True)
    (hs_s,) = _mlstm_scan(q, k, v, gt, tp, bs, ls,
                          init=(state_mlstm_C[:, 0], state_mlstm_n[:, 0], state_mlstm_m[:, 0]))
    x, hm, comb = _mixer_out(hs_p, hs_s, xc, proj, 1, x_src, t, mod, a_norm_w[0], a_skip[0],
                             a_w_down[0].astype(BF16), ln_w[0, 0], ln_b[0, 0],
                             *_router_weights(moe_w_rg[0], moe_b_rg[0], moe_w_re[0], moe_b_re[0]),
                             tp, ls, True, "mlstm_out")
    x_p, x_s = _moe(hm, comb, *_expert_weights(moe_w_gate[0], moe_w_up[0], moe_w_down[0]), x, mod,
                    ln_w[0, 1], ln_b[0, 1], tp, ls)

    mod = mods[1]
    x_src = (x_p, 0, _grid_to_cols(x_s.reshape(bs, ls, D_MODEL)).reshape(ts, D_MODEL), 0)
    w_in = jnp.pad(b_w_in[0], ((0, 0), (0, -b_w_in.shape[2] % LANES))).astype(BF16)
    proj, dt = _modmm(x_src, t, mod, w_in, LANES, tp, ls, 0, 1, "ssd_in_proj")
    dt = dt[:, :2 * H_B]
    xbc = _conv_silu(proj, E_B, E_B + 2 * GN_B, b_conv_w[0], b_conv_b[0], tp, lp, ls, "ssd_conv")
    lane_bcast = (2, G_B, HG_B, LANES)
    a_neg = jnp.broadcast_to(-jnp.exp(b_a_log[0]).reshape(2, G_B, HG_B, 1), lane_bcast)
    dt_bias = jnp.broadcast_to(b_dt_bias[0].reshape(2, G_B, HG_B, 1), lane_bcast)
    dtr = dt.T.reshape(2, G_B, HG_B, tp + ts)
    y_p, s_new = _ssd_scan(xbc, dtr, a_neg, dt_bias, 0, bp, lp, emit_state=True)
    (y_s,) = _ssd_scan(xbc, dtr, a_neg, dt_bias, tp, bs, ls, init=state_ssd[:, 0])
    x, hm, comb = _mixer_out(y_p, y_s, xbc, proj, 0, x_src, t, mod, b_norm_w[0], jnp.repeat(b_d_skip[0], P_B),
                             b_w_out[0].astype(BF16), ln_w[1, 0], ln_b[1, 0],
                             *_router_weights(moe_w_rg[1], moe_b_rg[1], moe_w_re[1], moe_b_re[1]),
                             tp, ls, False, "ssd_out")
    x_p, x_s = _moe(hm, comb, *_expert_weights(moe_w_gate[1], moe_w_up[1], moe_w_down[1]), x, mod,
                    ln_w[1, 1], ln_b[1, 1], tp, ls)

    y_prompt = x_p.reshape(bp, lp, D_MODEL)
    y_sample = _cols_to_grid(x_s.reshape(bs, ls, D_MODEL))
    return (y_prompt, y_sample,
            c_new[:, None],
            n_new.reshape(bp, 1, 2, H_A, DH_A),
            m_new[:, :, :, 0, 0][:, None],
            s_new[:, None])
```

```python
import functools
import math

import jax
import jax.numpy as jnp
from jax import lax
from jax.experimental import pallas as pl
from jax.experimental.pallas import tpu as pltpu

F32 = jnp.float32
BF16 = jnp.bfloat16

D_MODEL = 1024
DEPTH = 2
GRID_W = 64
CONV_W = 5
E_A = 2 * D_MODEL
H_A = 4
DH_A = E_A // H_A
QKV_BLOCK = 4
E_B = 2 * D_MODEL
P_B = 64
H_B = E_B // P_B
G_B = 4
HG_B = H_B // G_B
D_STATE = 128
GN_B = G_B * D_STATE
MOE_GROUPS = 4
MOE_EXPERTS = 4
N_EXPERTS = MOE_GROUPS * MOE_EXPERTS
D_FF_E = 256
ALPHA = (2.0 * DEPTH) ** 0.25
EPS = 1e-5

LANES = 128
SUBLANES = 8
BF16_ROWS = 16
TM = 512
TM_CONV = 256
TM_MOE = 512
SCAN_CHUNK = 256
MOD_ROWS = 8
N_MOD_SEQ = 16
VMEM_LIMIT = 56 * 1024 * 1024
ROUTER_OFF = MOE_GROUPS


def _params(sem):
    return pltpu.CompilerParams(dimension_semantics=sem, vmem_limit_bytes=VMEM_LIMIT)


def _seq_of_tile(m, tm, tp, ls):
    npt = tp // tm
    return jnp.where(m < npt, 0, 1 + (m - npt) // (ls // tm))


def _silu(x):
    return x * jax.nn.sigmoid(x)


def _softplus(x):
    return jnp.maximum(x, 0.0) + jnp.log1p(jnp.exp(-jnp.abs(x)))


def _layer_norm_rows(r, w, b):
    mu = jnp.mean(r, axis=-1, keepdims=True)
    rc = r - mu
    var = jnp.mean(rc * rc, axis=-1, keepdims=True)
    return rc * lax.rsqrt(var + EPS) * w + b


def _ada_kernel(c_ref, w_ref, b_ref, o_ref):
    s = _silu(c_ref[...])
    o_ref[...] = jnp.dot(s.astype(BF16), w_ref[...].astype(BF16), preferred_element_type=F32) + b_ref[...]


def _ada(cvec, ada_w, ada_b):
    nl = ada_w.shape[0]
    n_out = ada_w.shape[2]
    tn = 1536
    return pl.pallas_call(
        _ada_kernel,
        grid=(nl, n_out // tn),
        in_specs=[pl.BlockSpec((N_MOD_SEQ, D_MODEL), lambda l, n: (0, 0)),
                  pl.BlockSpec((None, D_MODEL, tn), lambda l, n: (l, 0, n)),
                  pl.BlockSpec((None, 1, tn), lambda l, n: (l, 0, n))],
        out_specs=pl.BlockSpec((None, N_MOD_SEQ, tn), lambda l, n: (l, 0, n)),
        out_shape=jax.ShapeDtypeStruct((nl, N_MOD_SEQ, n_out), F32),
        compiler_params=_params(("arbitrary", "arbitrary")),
        name="ada_mod",
    )(cvec, ada_w, ada_b.reshape(nl, 1, n_out))


def _token_tile_specs(src, tm, tp, width=D_MODEL):
    pa, p0, sa, s0 = src
    npt = tp // tm
    nst = (sa.shape[0] - s0) // tm
    return ([pl.BlockSpec((tm, width), lambda m, *_: (p0 // tm + jnp.minimum(m, npt - 1), 0)),
             pl.BlockSpec((tm, width), lambda m, *_: (s0 // tm + jnp.clip(m - npt, 0, nst - 1), 0))], [pa, sa])


def _select_token_tile(p_ref, s_ref, tm, tp):
    return jnp.where(pl.program_id(0) < tp // tm, p_ref[...], s_ref[...])


def _modmm_kernel(xp_ref, xs_ref, mod_ref, w_ref, o_ref, *tail_ref, tm, tp, shift_row, scale_row):
    mod = mod_ref[...]
    x = _select_token_tile(xp_ref, xs_ref, tm, tp)
    h = x * (1.0 + mod[scale_row:scale_row + 1, :]) + mod[shift_row:shift_row + 1, :]
    res = jnp.dot(h.astype(BF16), w_ref[...], preferred_element_type=F32)
    n_main = o_ref.shape[1]
    o_ref[...] = res[:, :n_main].astype(o_ref.dtype)
    if tail_ref:
        tail_ref[0][...] = res[:, n_main:]


def _resident(shape):
    return pl.BlockSpec(shape, lambda *_: (0,) * len(shape), pipeline_mode=pl.Buffered(1))


def _modmm(src, t, mod, w_bf, n_tail, tp, ls, shift_row, scale_row, name):
    n_main = w_bf.shape[1] - n_tail
    tm = TM
    x_specs, x_args = _token_tile_specs(src, tm, tp)
    out_specs = [pl.BlockSpec((tm, n_main), lambda m: (m, 0))]
    out_shape = [jax.ShapeDtypeStruct((t, n_main), BF16)]
    if n_tail:
        out_specs.append(pl.BlockSpec((tm, n_tail), lambda m: (m, 0)))
        out_shape.append(jax.ShapeDtypeStruct((t, n_tail), F32))
    return pl.pallas_call(
        functools.partial(_modmm_kernel, tm=tm, tp=tp, shift_row=shift_row, scale_row=scale_row),
        grid=(t // tm,),
        in_specs=x_specs + [
            pl.BlockSpec((None, MOD_ROWS, D_MODEL), lambda m: (_seq_of_tile(m, tm, tp, ls), 0, 0)),
            _resident((D_MODEL, w_bf.shape[1]))],
        out_specs=out_specs,
        out_shape=out_shape,
        compiler_params=_params(("arbitrary",)),
        name=name,
    )(*x_args, mod, w_bf)


def _conv_kernel(cur_ref, prev_ref, next_ref, w_ref, b_ref, o_ref, ext_ref, *, tm, tp, lp, ls):
    m = pl.program_id(0)
    npt = tp // tm
    pos = jnp.where(m < npt, m % (lp // tm), (m - npt) % (ls // tm))
    last_pos = jnp.where(m < npt, lp // tm - 1, ls // tm - 1)
    halo = SUBLANES
    prev = jnp.where(pos == 0, 0.0, prev_ref[...].astype(F32)[BF16_ROWS - halo:, :])
    cur = cur_ref[...].astype(F32)
    nxt = jnp.where(pos == last_pos, 0.0, next_ref[...].astype(F32)[:halo, :])
    first_tap = jnp.minimum(m, 0) + (halo - CONV_W // 2)
    for c in range(ext_ref.shape[0]):
        lanes = slice(c * LANES, (c + 1) * LANES)
        ext_ref[c, 0:halo, :] = prev[:, lanes]
        ext_ref[c, halo:halo + tm, :] = cur[:, lanes]
        ext_ref[c, halo + tm:2 * halo + tm, :] = nxt[:, lanes]
    for c in range(ext_ref.shape[0]):
        lanes = slice(c * LANES, (c + 1) * LANES)
        acc = jnp.zeros((tm, LANES), F32) + b_ref[:, lanes]
        for j in range(CONV_W):
            acc = acc + ext_ref[c, pl.ds(first_tap + j, tm), :] * w_ref[j:j + 1, lanes]
        o_ref[:, lanes] = _silu(acc).astype(o_ref.dtype)


def _conv_silu(a, col0, width, w, b, tp, lp, ls, name):
    t = a.shape[0]
    tm, tc = TM_CONV, 1024
    cb0 = col0 // tc
    nhalo = t // BF16_ROWS
    per = tm // BF16_ROWS
    return pl.pallas_call(
        functools.partial(_conv_kernel, tm=tm, tp=tp, lp=lp, ls=ls),
        grid=(t // tm, width // tc),
        in_specs=[pl.BlockSpec((tm, tc), lambda m, j: (m, cb0 + j)),
                  pl.BlockSpec((BF16_ROWS, tc), lambda m, j: (jnp.maximum(m * per - 1, 0), cb0 + j)),
                  pl.BlockSpec((BF16_ROWS, tc), lambda m, j: (jnp.minimum((m + 1) * per, nhalo - 1), cb0 + j)),
                  pl.BlockSpec((CONV_W, tc), lambda m, j: (0, j)),
                  pl.BlockSpec((1, tc), lambda m, j: (0, j))],
        out_specs=pl.BlockSpec((tm, tc), lambda m, j: (m, j)),
        out_shape=jax.ShapeDtypeStruct((t, width), BF16),
        scratch_shapes=[pltpu.VMEM((tc // LANES, tm + 2 * SUBLANES, LANES), F32)],
        compiler_params=_params(("arbitrary", "arbitrary")),
        name=name,
    )(a, a, a, w, b.reshape(1, width))


def _qkv_kernel(xc_ref, xu_ref, wqk_ref, wv_ref, wg_ref, bg_ref, q_ref, k_ref, v_ref, g_ref):
    nblk = E_A // LANES
    g = jnp.zeros(g_ref.shape, F32) + bg_ref[...]
    for b in range(nblk):
        sl = slice(b * LANES, (b + 1) * LANES)
        xcb = xc_ref[:, sl]
        xub = xu_ref[:, sl]
        qk = jnp.dot(xcb, wqk_ref[b], preferred_element_type=F32)
        vv = jnp.dot(xub, wv_ref[b], preferred_element_type=F32)
        qb = qk[:, :LANES].astype(BF16)
        kb = qk[:, LANES:].astype(BF16)
        vb = vv.astype(BF16)
        q_ref[:, sl] = qb
        k_ref[:, sl] = kb
        v_ref[:, sl] = vb
        g = g + jnp.dot(qb, wg_ref[0, sl, :], preferred_element_type=F32)
        g = g + jnp.dot(kb, wg_ref[1, sl, :], preferred_element_type=F32)
        g = g + jnp.dot(vb, wg_ref[2, sl, :], preferred_element_type=F32)
    lane = lax.broadcasted_iota(jnp.int32, g.shape, 1)
    is_forget = (lane & (2 * H_A - 1)) >= H_A
    log_sig = jnp.minimum(g, 0.0) - jnp.log1p(jnp.exp(-jnp.abs(g)))
    g_ref[...] = jnp.where(is_forget, log_sig, g)


def _conv_qkv_kernel(cur_ref, prev_ref, next_ref, cw_ref, cb_ref, wqk_ref, wv_ref, wg_ref, bg_ref,
                     xc_ref, q_ref, k_ref, v_ref, g_ref, ext_ref, *, tm, tp, lp, ls):
    _conv_kernel(cur_ref, prev_ref, next_ref, cw_ref, cb_ref, xc_ref, ext_ref, tm=tm, tp=tp, lp=lp, ls=ls)
    _qkv_kernel(xc_ref, cur_ref, wqk_ref, wv_ref, wg_ref, bg_ref, q_ref, k_ref, v_ref, g_ref)


def _conv_qkv(proj, conv_w, conv_b, wqk, wv, wg, bg, tp, lp, ls):
    t = proj.shape[0]
    tm = TM_CONV
    nblk = E_A // LANES
    nhalo = t // BF16_ROWS
    per = tm // BF16_ROWS
    tile = pl.BlockSpec((tm, E_A), lambda m: (m, 0))
    return pl.pallas_call(
        functools.partial(_conv_qkv_kernel, tm=tm, tp=tp, lp=lp, ls=ls),
        grid=(t // tm,),
        in_specs=[tile,
                  pl.BlockSpec((BF16_ROWS, E_A), lambda m: (jnp.maximum(m * per - 1, 0), 0)),
                  pl.BlockSpec((BF16_ROWS, E_A), lambda m: (jnp.minimum((m + 1) * per, nhalo - 1), 0)),
                  pl.BlockSpec((CONV_W, E_A), lambda m: (0, 0)),
                  pl.BlockSpec((1, E_A), lambda m: (0, 0)),
                  pl.BlockSpec((nblk, LANES, 2 * LANES), lambda m: (0, 0, 0)),
                  pl.BlockSpec((nblk, LANES, LANES), lambda m: (0, 0, 0)),
                  pl.BlockSpec((3, E_A, LANES), lambda m: (0, 0, 0)),
                  pl.BlockSpec((1, LANES), lambda m: (0, 0))],
        out_specs=[tile, tile, tile, tile, pl.BlockSpec((tm, LANES), lambda m: (m, 0))],
        out_shape=[jax.ShapeDtypeStruct((t, E_A), BF16)] * 4 + [jax.ShapeDtypeStruct((t, LANES), F32)],
        scratch_shapes=[pltpu.VMEM((E_A // LANES, tm + 2 * SUBLANES, LANES), F32)],
        compiler_params=_params(("arbitrary",)),
        name="mlstm_conv_qkv_gates",
    )(proj, proj, proj, conv_w, conv_b.reshape(1, E_A), wqk, wv, wg, bg)


def _order_mask(dirn, n, rows_are_later):
    a_io = lax.broadcasted_iota(jnp.int32, (n, n), 0)
    b_io = lax.broadcasted_iota(jnp.int32, (n, n), 1)
    if (dirn == 0) == rows_are_later:
        return b_io <= a_io
    return b_io >= a_io


def _exact_cumsum_rows(rows, tri):
    nrow = rows.shape[0]
    p1 = rows.astype(BF16).astype(F32)
    r1 = rows - p1
    p2 = r1.astype(BF16).astype(F32)
    p3 = r1 - p2
    pieces = jnp.concatenate([p1, p2, p3, jnp.zeros_like(p1)], axis=0).astype(BF16)
    cs = jnp.dot(pieces, tri, preferred_element_type=F32)
    return cs[0:nrow] + cs[nrow:2 * nrow] + cs[2 * nrow:3 * nrow]


def _pick_row(blk, idx):
    sub = lax.broadcasted_iota(jnp.int32, (blk.shape[0], 1), 0)
    return jnp.sum(jnp.where(sub == idx, blk, 0.0), axis=0, keepdims=True)


def _rows_to_cols(rows):
    pad = jnp.zeros((LANES - rows.shape[0], rows.shape[1]), F32)
    return jnp.concatenate([rows, pad], axis=0).T


def _pick_col(blk, idx):
    lane = lax.broadcasted_iota(jnp.int32, (1, blk.shape[1]), 1)
    return jnp.sum(jnp.where(lane == idx, blk, 0.0), axis=1, keepdims=True)


def _mlstm_direction(dirn, h, q, k, v, s_qk, kt, gt_ref, c_ref, n_ref, m_ref, use_state, need_update):
    lc = SCAN_CHUNK
    scale = DH_A ** -0.5
    log_scale = -0.5 * math.log(DH_A)
    gates = gt_ref[dirn * 2 * H_A:(dirn + 1) * 2 * H_A, :]
    tri = jnp.where(_order_mask(dirn, lc, False), 1.0, 0.0).astype(BF16)
    csum = _exact_cumsum_rows(gates, tri)
    i_row = _pick_row(gates, h)
    f_row = _pick_row(gates, H_A + h)
    b_row = _pick_row(csum, H_A + h)
    v_row = i_row - b_row
    valid = _order_mask(dirn, lc, True)
    m_prev = m_ref[:, 0:1]
    run_max = jnp.max(jnp.where(valid, v_row, -jnp.inf), axis=1, keepdims=True)
    u_col = -jnp.maximum(m_prev, run_max)
    p = s_qk * jnp.exp(jnp.where(valid, u_col + (v_row + log_scale), -jnp.inf))
    num = jnp.dot(p.astype(BF16), v, preferred_element_type=F32)
    den = jnp.sum(p, axis=1, keepdims=True)
    if use_state:
        w_inter = jnp.exp(m_prev + u_col) * scale
        qc = jnp.dot(q, c_ref[...].astype(BF16), preferred_element_type=F32)
        qn = jnp.sum(q.astype(F32) * n_ref[...], axis=1, keepdims=True)
        num = num + w_inter * qc
        den = den + w_inter * qn
    b_col = _rows_to_cols(b_row)[:, 0:1]
    h_out = num / jnp.maximum(jnp.abs(den), jnp.exp(u_col - b_col))
    if not need_update:
        return h_out, None
    b_last = jnp.sum(f_row, axis=1, keepdims=True)
    logk = b_last + v_row
    m_new = jnp.maximum(b_last + m_prev, jnp.max(logk, axis=1, keepdims=True))
    wk = jnp.exp(logk - m_new)
    c_new = jnp.dot((kt * wk).astype(BF16), v, preferred_element_type=F32)
    wk8 = (jnp.zeros((SUBLANES, lc), F32) + wk).astype(BF16)
    n_new = jnp.dot(wk8, k, preferred_element_type=F32)[0:1, :]
    return h_out, (c_new, n_new, m_new, jnp.exp(b_last + m_prev - m_new))


def _mlstm_kernel(*refs, nc, has_init, emit_state):
    shared = nc == 1
    pos = 4 if shared else 8
    chunk_refs = [refs[0:4], refs[0:4] if shared else refs[4:8]]
    if has_init:
        c0_ref, n0_ref, m0_ref = refs[pos:pos + 3]
        pos += 3
    hs_ref = refs[pos]
    pos += 1
    if emit_state:
        cn_ref, nn_ref, mn_ref = refs[pos:pos + 3]
        pos += 3
    c_scr, n_scr, m_scr = refs[pos:pos + 3]

    lc = SCAN_CHUNK
    s_id, h, c = (pl.program_id(i) for i in range(3))
    use_state = has_init or nc > 1

    @pl.when(c == 0)
    def _():
        for dirn in range(2):
            if has_init:
                c_scr[dirn] = c0_ref[dirn]
                n_scr[dirn] = _pick_row(n0_ref[dirn], h)
                m_scr[dirn] = jnp.zeros(m_scr.shape[1:], F32) + m0_ref[(s_id * 2 + dirn) * H_A + h]
            else:
                c_scr[dirn] = jnp.zeros(c_scr.shape[1:], F32)
                n_scr[dirn] = jnp.zeros(n_scr.shape[1:], F32)
                m_scr[dirn] = jnp.zeros(m_scr.shape[1:], F32)
        if not shared:
            hs_ref[...] = jnp.zeros(hs_ref.shape, F32)

    def chunk_operands(q_ref, k_ref, v_ref, gt_ref):
        q = q_ref[...]
        k = k_ref[...]
        s_qk = lax.dot_general(q, k, (((1,), (1,)), ((), ())), preferred_element_type=F32)
        return q, k, v_ref[...], s_qk, k.astype(F32).T, gt_ref

    ops = [chunk_operands(*chunk_refs[0])]
    ops.append(ops[0] if shared else chunk_operands(*chunk_refs[1]))

    def run(need_update):
        return [_mlstm_direction(dirn, h, *ops[dirn], c_scr.at[dirn], n_scr.at[dirn], m_scr.at[dirn],
                                 use_state, need_update) for dirn in range(2)]

    def write_out(res):
        if shared:
            hs_ref[...] = res[0][0] + res[1][0]
        else:
            rows_f = pl.ds(pl.multiple_of(c * lc, lc), lc)
            rows_b = pl.ds(pl.multiple_of((nc - 1 - c) * lc, lc), lc)
            hs_ref[rows_f, :] = hs_ref[rows_f, :] + res[0][0]
            hs_ref[rows_b, :] = hs_ref[rows_b, :] + res[1][0]

    def with_update():
        res = run(True)
        write_out(res)
        for dirn in range(2):
            c_new, n_new, m_new, decay = res[dirn][1]
            if use_state:
                c_new = c_new + decay * c_scr[dirn]
                n_new = n_new + decay * n_scr[dirn]
            m_new_row = jnp.zeros(m_scr.shape[1:], F32) + m_new
            if nc > 1:
                c_scr[dirn] = c_new
                n_scr[dirn] = n_new
                m_scr[dirn] = m_new_row
            if emit_state:
                @pl.when(c == nc - 1)
                def _():
                    cn_ref[dirn] = c_new
                    nn_ref[dirn] = n_new
                    mn_ref[dirn] = m_new_row

    def without_update():
        write_out(run(False))

    if emit_state:
        with_update()
    else:
        pl.when(c < nc - 1)(with_update)
        pl.when(c == nc - 1)(without_update)


def _mlstm_scan(q, k, v, gt, row0, nseq, seqlen, init=None, emit_state=False):
    lc = SCAN_CHUNK
    nc = seqlen // lc
    blk0 = row0 // lc
    has_init = init is not None

    def blk_f(s, c):
        return blk0 + s * nc + c

    def blk_b(s, c):
        return blk0 + s * nc + nc - 1 - c

    def chunk_specs(blk):
        return [pl.BlockSpec((lc, DH_A), lambda s, h, c: (blk(s, c), h))] * 3 + [
            pl.BlockSpec((4 * H_A, lc), lambda s, h, c: (0, blk(s, c)))]

    in_specs = chunk_specs(blk_f)
    args = [q, k, v, gt]
    if nc > 1:
        in_specs += chunk_specs(blk_b)
        args += [q, k, v, gt]
    if has_init:
        c0, n0, m0 = init
        in_specs += [pl.BlockSpec((None, 2, None, DH_A, DH_A), lambda s, h, c: (s, 0, h, 0, 0)),
                     pl.BlockSpec((None, 2, H_A, DH_A), lambda s, h, c: (s, 0, 0, 0)),
                     pl.BlockSpec(memory_space=pltpu.SMEM)]
        args += [c0, n0, m0.reshape(-1)]
    out_specs = [pl.BlockSpec((seqlen, DH_A), lambda s, h, c: (s, h))]
    out_shape = [jax.ShapeDtypeStruct((nseq * seqlen, E_A), F32)]
    if emit_state:
        out_specs += [pl.BlockSpec((None, 2, None, DH_A, DH_A), lambda s, h, c: (s, 0, h, 0, 0)),
                      pl.BlockSpec((None, 2, None, 1, DH_A), lambda s, h, c: (s, 0, h, 0, 0)),
                      pl.BlockSpec((None, 2, None, 1, LANES), lambda s, h, c: (s, 0, h, 0, 0))]
        out_shape += [jax.ShapeDtypeStruct((nseq, 2, H_A, DH_A, DH_A), F32),
                      jax.ShapeDtypeStruct((nseq, 2, H_A, 1, DH_A), F32),
                      jax.ShapeDtypeStruct((nseq, 2, H_A, 1, LANES), F32)]
    return pl.pallas_call(
        functools.partial(_mlstm_kernel, nc=nc, has_init=has_init, emit_state=emit_state),
        grid=(nseq, H_A, nc),
        in_specs=in_specs,
        out_specs=out_specs,
        out_shape=out_shape,
        scratch_shapes=[pltpu.VMEM((2, DH_A, DH_A), F32), pltpu.VMEM((2, 1, DH_A), F32),
                        pltpu.VMEM((2, 1, LANES), F32)],
        compiler_params=_params(("arbitrary",) * 3),
        name="mlstm_scan_state" if emit_state else "mlstm_scan_init",
    )(*args)


def _ssd_direction(dirn, xt, bb, ct, cbt_raw, dtr_ref, a_ref, bias_ref, h_ref, yt_ref, xw_ref, use_state):
    lc = SCAN_CHUNK
    r_io = lax.broadcasted_iota(jnp.int32, (lc, lc), 0)
    s_io = lax.broadcasted_iota(jnp.int32, (lc, lc), 1)
    valid = (r_io <= s_io) if dirn == 0 else (r_io >= s_io)
    tri = jnp.where(valid, 1.0, 0.0).astype(BF16)

    dt_rows = _softplus(dtr_ref[...] + bias_ref[:, 0:1])
    a_rows = dt_rows * a_ref[:, 0:1]
    a1 = a_rows.astype(BF16).astype(F32)
    r1 = a_rows - a1
    a2 = r1.astype(BF16).astype(F32)
    a3 = r1 - a2
    pieces = jnp.concatenate([a1, a2, a3, jnp.zeros_like(a1)], axis=0).astype(BF16)
    cs = jnp.dot(pieces, tri, preferred_element_type=F32)
    acs_rows = cs[0:HG_B] + cs[HG_B:2 * HG_B] + cs[2 * HG_B:3 * HG_B]
    a_last = jnp.sum(a_rows, axis=1, keepdims=True)
    wr_rows = jnp.exp(a_last - acs_rows) * dt_rows
    acs_cols = jnp.concatenate([acs_rows, jnp.zeros((LANES - HG_B, lc), F32)], axis=0).T

    cbt = jnp.where(valid, cbt_raw, 0.0)
    if use_state:
        inter_t = jnp.dot(h_ref[...].astype(BF16), ct, preferred_element_type=F32)
        e_rows = jnp.exp(acs_rows)
    for hg in range(HG_B):
        sl = slice(hg * P_B, (hg + 1) * P_B)
        row = slice(hg, hg + 1)
        seg_t = acs_rows[row, :] - acs_cols[:, hg:hg + 1]
        wt = cbt * jnp.exp(jnp.minimum(seg_t, 0.0))
        xth = xt[sl, :]
        yth = jnp.dot((xth * dt_rows[row, :]).astype(BF16), wt.astype(BF16), preferred_element_type=F32)
        if use_state:
            yth = yth + inter_t[sl, :] * e_rows[row, :]
        yt_ref[sl, :] = yth
        xw_ref[sl, :] = xth * wr_rows[row, :]
    return a_last


def _ssd_kernel(*refs, nc, has_init, emit_state):
    shared = nc == 1
    pos = 3 if shared else 6
    chunk_refs = [refs[0:3], refs[0:3] if shared else refs[3:6]]
    dtr_refs = refs[pos:pos + 2]
    a_ref, bias_ref = refs[pos + 2:pos + 4]
    pos += 4
    if has_init:
        h0_ref = refs[pos]
        pos += 1
    y_ref = refs[pos]
    pos += 1
    if emit_state:
        sn_ref = refs[pos]
        pos += 1
    h_scr, yt_scr, xw_scr = refs[pos:pos + 3]

    lc = SCAN_CHUNK
    c = pl.program_id(2)
    use_state = has_init or nc > 1
    hp = HG_B * P_B

    @pl.when(c == 0)
    def _():
        if has_init:
            h_scr[...] = h0_ref[...].reshape(2, hp, D_STATE)
        else:
            h_scr[...] = jnp.zeros(h_scr.shape, F32)
        if not shared:
            y_ref[...] = jnp.zeros(y_ref.shape, F32)

    def chunk_operands(x_ref, b_ref, c_ref):
        bb = b_ref[...]
        ct = c_ref[...].astype(F32).T.astype(BF16)
        return x_ref[...].astype(F32).T, bb, ct, jnp.dot(bb, ct, preferred_element_type=F32)

    ops = [chunk_operands(*chunk_refs[0])]
    ops.append(ops[0] if shared else chunk_operands(*chunk_refs[1]))
    a_last = [_ssd_direction(dirn, *ops[dirn], dtr_refs[dirn], a_ref.at[dirn], bias_ref.at[dirn], h_scr.at[dirn],
                             yt_scr.at[dirn], xw_scr.at[dirn], use_state) for dirn in range(2)]

    if shared:
        y_ref[...] = (yt_scr[0] + yt_scr[1]).T
    else:
        rows_f = pl.ds(pl.multiple_of(c * lc, lc), lc)
        rows_b = pl.ds(pl.multiple_of((nc - 1 - c) * lc, lc), lc)
        y_ref[rows_f, :] = y_ref[rows_f, :] + yt_scr[0].T
        y_ref[rows_b, :] = y_ref[rows_b, :] + yt_scr[1].T

    def update_state():
        for dirn in range(2):
            upd = jnp.dot(xw_scr[dirn].astype(BF16), ops[dirn][1], preferred_element_type=F32)
            dec = jnp.exp(a_last[dirn])
            for hg in range(HG_B):
                sl = slice(hg * P_B, (hg + 1) * P_B)
                new = upd[sl, :]
                if use_state:
                    new = new + dec[hg:hg + 1, :] * h_scr[dirn, sl, :]
                h_scr[dirn, sl, :] = new
        if emit_state:
            @pl.when(c == nc - 1)
            def _():
                sn_ref[...] = h_scr[...].reshape(2, HG_B, P_B, D_STATE)

    if emit_state:
        update_state()
    elif nc > 1:
        pl.when(c < nc - 1)(update_state)


def _ssd_scan(xbc, dtr, a_neg, dt_bias, row0, nseq, seqlen, init=None, emit_state=False):
    lc = SCAN_CHUNK
    nc = seqlen // lc
    blk0 = row0 // lc
    has_init = init is not None
    hp = HG_B * P_B
    b_off = E_B // D_STATE
    c_off = (E_B + GN_B) // D_STATE

    def blk_f(s, c):
        return blk0 + s * nc + c

    def blk_b(s, c):
        return blk0 + s * nc + nc - 1 - c

    def chunk_specs(blk):
        return [pl.BlockSpec((lc, hp), lambda s, g, c: (blk(s, c), g)),
                pl.BlockSpec((lc, D_STATE), lambda s, g, c: (blk(s, c), b_off + g)),
                pl.BlockSpec((lc, D_STATE), lambda s, g, c: (blk(s, c), c_off + g))]

    in_specs = chunk_specs(blk_f)
    args = [xbc, xbc, xbc]
    if nc > 1:
        in_specs += chunk_specs(blk_b)
        args += [xbc, xbc, xbc]
    table_spec = pl.BlockSpec((2, None, HG_B, LANES), lambda s, g, c: (0, g, 0, 0))
    in_specs += [pl.BlockSpec((None, None, HG_B, lc), lambda s, g, c: (0, g, 0, blk_f(s, c))),
                 pl.BlockSpec((None, None, HG_B, lc), lambda s, g, c: (1, g, 0, blk_b(s, c))),
                 table_spec, table_spec]
    args += [dtr, dtr, a_neg, dt_bias]
    state_spec = pl.BlockSpec((None, 2, HG_B, P_B, D_STATE), lambda s, g, c: (s, 0, g, 0, 0))
    if has_init:
        in_specs.append(state_spec)
        args.append(init)
    out_specs = [pl.BlockSpec((seqlen, hp), lambda s, g, c: (s, g))]
    out_shape = [jax.ShapeDtypeStruct((nseq * seqlen, E_B), F32)]
    if emit_state:
        out_specs.append(state_spec)
        out_shape.append(jax.ShapeDtypeStruct((nseq, 2, H_B, P_B, D_STATE), F32))
    return pl.pallas_call(
        functools.partial(_ssd_kernel, nc=nc, has_init=has_init, emit_state=emit_state),
        grid=(nseq, G_B, nc),
        in_specs=in_specs,
        out_specs=out_specs,
        out_shape=out_shape,
        scratch_shapes=[pltpu.VMEM((2, hp, D_STATE), F32), pltpu.VMEM((2, hp, lc), F32),
                        pltpu.VMEM((2, hp, lc), F32)],
        compiler_params=_params(("arbitrary",) * 3),
        name="ssd_scan_state" if emit_state else "ssd_scan_init",
    )(*args)


def _mixer_out_kernel(yp_ref, ys_ref, u_ref, z_ref, xp_ref, xs_ref, mod_ref, nw_ref, sk_ref, w_ref, lnw_ref, lnb_ref,
                      wr_ref, br_ref, o_ref, hm_ref, comb_ref, *, tp, grouped):
    y = _select_token_tile(yp_ref, ys_ref, TM, tp)
    u = u_ref[...].astype(F32)
    gate = _silu(z_ref[...].astype(F32))
    if grouped:
        parts = []
        for hd in range(H_A):
            seg = y[:, hd * DH_A:(hd + 1) * DH_A]
            mu = jnp.mean(seg, axis=-1, keepdims=True)
            sc = seg - mu
            var = jnp.mean(sc * sc, axis=-1, keepdims=True)
            parts.append(sc * lax.rsqrt(var + EPS))
        hn = jnp.concatenate(parts, axis=1) * nw_ref[...]
        t = (hn + sk_ref[...] * u) * gate
    else:
        t0 = (y + sk_ref[...] * u) * gate
        t = t0 * lax.rsqrt(jnp.mean(t0 * t0, axis=-1, keepdims=True) + EPS) * nw_ref[...]
    out = jnp.dot(t.astype(BF16), w_ref[...], preferred_element_type=F32)
    mod = mod_ref[...]
    r = ALPHA * _select_token_tile(xp_ref, xs_ref, TM, tp) + mod[2:3, :] * out
    x_new = _layer_norm_rows(r, lnw_ref[...], lnb_ref[...])
    o_ref[...] = x_new
    hm_ref[...], comb_ref[...] = _route(x_new, mod, wr_ref[...], br_ref[...])


def _mixer_out(yp, ys, u, proj, zblk, x_src, t, mod, nw, sk, w_bf, lnw, lnb, w_r, b_r, tp, ls, grouped, name):
    e = w_bf.shape[0]
    y_specs, y_args = _token_tile_specs((yp, 0, ys, 0), TM, tp, e)
    x_specs, x_args = _token_tile_specs(x_src, TM, tp)
    vec_e = pl.BlockSpec((1, e), lambda m: (0, 0))
    vec_d = pl.BlockSpec((1, D_MODEL), lambda m: (0, 0))
    return pl.pallas_call(
        functools.partial(_mixer_out_kernel, tp=tp, grouped=grouped),
        grid=(t // TM,),
        in_specs=y_specs + [
                  pl.BlockSpec((TM, e), lambda m: (m, 0)),
                  pl.BlockSpec((TM, e), lambda m: (m, zblk))] + x_specs + [
                  pl.BlockSpec((None, MOD_ROWS, D_MODEL), lambda m: (_seq_of_tile(m, TM, tp, ls), 0, 0)),
                  vec_e, vec_e,
                  pl.BlockSpec((e, D_MODEL), lambda m: (0, 0)),
                  vec_d, vec_d,
                  pl.BlockSpec((D_MODEL, LANES), lambda m: (0, 0)),
                  pl.BlockSpec((1, LANES), lambda m: (0, 0))],
        out_specs=[pl.BlockSpec((TM, D_MODEL), lambda m: (m, 0)),
                   pl.BlockSpec((TM, D_MODEL), lambda m: (m, 0)),
                   pl.BlockSpec((TM, LANES), lambda m: (m, 0))],
        out_shape=[jax.ShapeDtypeStruct((t, D_MODEL), F32), jax.ShapeDtypeStruct((t, D_MODEL), BF16),
                   jax.ShapeDtypeStruct((t, LANES), F32)],
        compiler_params=_params(("arbitrary",)),
        name=name,
    )(*y_args, u, proj, *x_args, mod, nw.reshape(1, e), sk.reshape(1, e), w_bf, lnw.reshape(1, D_MODEL),
      lnb.reshape(1, D_MODEL), w_r, b_r)


def _route(x, mod, w_r, b_r):
    hm = (x * (1.0 + mod[4:5, :]) + mod[3:4, :]).astype(BF16)
    logits = jnp.dot(hm, w_r, preferred_element_type=F32) + b_r
    n_used = ROUTER_OFF + N_EXPERTS
    n_rows = -(-n_used // SUBLANES) * SUBLANES
    lt = logits.T[:n_rows, :]
    row = lax.broadcasted_iota(jnp.int32, lt.shape, 0).astype(F32)
    big = float(2 * LANES)
    glog = jnp.where(row < MOE_GROUPS, lt, -jnp.inf)
    ge = jnp.exp(glog - jnp.max(glog, axis=0, keepdims=True))
    prob = ge / jnp.sum(ge, axis=0, keepdims=True)
    gp = jnp.max(prob, axis=0, keepdims=True)
    gi = jnp.min(jnp.where(prob == gp, row, big), axis=0, keepdims=True)
    lo = ROUTER_OFF + gi * MOE_EXPERTS
    ev = jnp.where((row >= lo) & (row < lo + MOE_EXPERTS), lt, -jnp.inf)
    e1 = jnp.max(ev, axis=0, keepdims=True)
    i1 = jnp.min(jnp.where(ev == e1, row, big), axis=0, keepdims=True)
    ev2 = jnp.where(row == i1, -jnp.inf, ev)
    e2 = jnp.max(ev2, axis=0, keepdims=True)
    i2 = jnp.min(jnp.where(ev2 == e2, row, big), axis=0, keepdims=True)
    t2 = jnp.exp(e2 - e1)
    w1 = 1.0 / (1.0 + t2)
    w2 = t2 / (1.0 + t2)
    comb_t = jnp.where(row == i1, w1 * gp, jnp.where(row == i2, w2 * gp, 0.0))
    return hm, _rows_to_cols(comb_t)


def _moe_kernel(hm_ref, comb_ref, wg_ref, wu_ref, wd_ref, x_ref, mod_ref, lnw_ref, lnb_ref, op_ref, os_ref, *, npt):
    hm = hm_ref[...]
    comb = comb_ref[...]
    acc = None
    for g in range(MOE_GROUPS):
        a = _silu(jnp.dot(hm, wg_ref[g], preferred_element_type=F32)) * jnp.dot(
            hm, wu_ref[g], preferred_element_type=F32)
        parts = []
        for e in range(MOE_EXPERTS):
            lane = ROUTER_OFF + g * MOE_EXPERTS + e
            parts.append(a[:, e * D_FF_E:(e + 1) * D_FF_E] * comb[:, lane:lane + 1])
        part = jnp.dot(jnp.concatenate(parts, axis=1).astype(BF16), wd_ref[g], preferred_element_type=F32)
        acc = part if acc is None else acc + part
    mod = mod_ref[...]
    r = ALPHA * x_ref[...] + mod[5:6, :] * acc
    y = _layer_norm_rows(r, lnw_ref[...], lnb_ref[...])
    m = pl.program_id(0)

    @pl.when(m < npt)
    def _():
        op_ref[...] = y

    @pl.when(m >= npt)
    def _():
        os_ref[...] = y


def _moe(hm, comb, wg, wu, wd, x, mod, lnw, lnb, tp, ls):
    t = x.shape[0]
    tm = TM_MOE
    npt = tp // tm
    nst = (t - tp) // tm
    gf = MOE_EXPERTS * D_FF_E
    vec_d = pl.BlockSpec((1, D_MODEL), lambda m: (0, 0))
    return pl.pallas_call(
        functools.partial(_moe_kernel, npt=npt),
        grid=(t // tm,),
        in_specs=[pl.BlockSpec((tm, D_MODEL), lambda m: (m, 0)),
                  pl.BlockSpec((tm, LANES), lambda m: (m, 0)),
                  _resident((MOE_GROUPS, D_MODEL, gf)),
                  _resident((MOE_GROUPS, D_MODEL, gf)),
                  _resident((MOE_GROUPS, gf, D_MODEL)),
                  pl.BlockSpec((tm, D_MODEL), lambda m: (m, 0)),
                  pl.BlockSpec((None, MOD_ROWS, D_MODEL), lambda m: (_seq_of_tile(m, tm, tp, ls), 0, 0)),
                  vec_d, vec_d],
        out_specs=[pl.BlockSpec((tm, D_MODEL), lambda m: (jnp.minimum(m, npt - 1), 0)),
                   pl.BlockSpec((tm, D_MODEL), lambda m: (jnp.clip(m - npt, 0, nst - 1), 0))],
        out_shape=[jax.ShapeDtypeStruct((tp, D_MODEL), F32), jax.ShapeDtypeStruct((t - tp, D_MODEL), F32)],
        compiler_params=_params(("arbitrary",)),
        name="moe_experts",
    )(hm, comb, wg, wu, wd, x, mod, lnw.reshape(1, D_MODEL), lnb.reshape(1, D_MODEL))


def _router_weights(w_rg, b_rg, w_re, b_re):
    w_r = jnp.concatenate([w_rg, jnp.moveaxis(w_re, 0, 1).reshape(D_MODEL, N_EXPERTS)], axis=1)
    w_r = jnp.pad(w_r, ((0, 0), (0, LANES - w_r.shape[1]))).astype(BF16)
    b_r = jnp.pad(jnp.concatenate([b_rg, b_re.reshape(-1)]), (0, LANES - MOE_GROUPS - N_EXPERTS)).reshape(1, LANES)
    return w_r, b_r


def _expert_weights(w_gate, w_up, w_down):
    gf = MOE_EXPERTS * D_FF_E
    wg = w_gate.transpose(0, 2, 1, 3).reshape(MOE_GROUPS, D_MODEL, gf).astype(BF16)
    wu = w_up.transpose(0, 2, 1, 3).reshape(MOE_GROUPS, D_MODEL, gf).astype(BF16)
    wd = w_down.reshape(MOE_GROUPS, gf, D_MODEL).astype(BF16)
    return wg, wu, wd


def _blockdiag_lanes(w):
    per = LANES // QKV_BLOCK
    wb = w.reshape(-1, per, QKV_BLOCK, QKV_BLOCK)
    eye = jnp.eye(per, dtype=w.dtype)
    return jnp.einsum('bncd,nm->bncmd', wb, eye).reshape(-1, LANES, LANES)


def _grid_to_cols(x):
    b, l, dd = x.shape
    return x.reshape(b, l // GRID_W, GRID_W, dd).transpose(0, 2, 1, 3).reshape(b, l, dd)


def _cols_to_grid(x):
    b, l, dd = x.shape
    return x.reshape(b, GRID_W, l // GRID_W, dd).transpose(0, 2, 1, 3).reshape(b, l, dd)


def kernel(x_prompt, x_sample, state_mlstm_C, state_mlstm_n, state_mlstm_m, state_ssd, c, c_ctx, ada_w, ada_b, ln_w, ln_b, a_w_in, a_conv_w, a_conv_b, a_w_q, a_w_k, a_w_v, a_w_gates, a_b_gates, a_norm_w, a_skip, a_w_down, b_w_in, b_conv_w, b_conv_b, b_dt_bias, b_a_log, b_d_skip, b_norm_w, b_w_out, moe_w_rg, moe_b_rg, moe_w_re, moe_b_re, moe_w_gate, moe_w_up, moe_w_down):
    bp, lp, _ = x_prompt.shape
    bs, ls, _ = x_sample.shape
    tp, ts = bp * lp, bs * ls
    assert lp % SCAN_CHUNK == 0 and ls % SCAN_CHUNK == 0 and tp % 512 == 0 and ls % 512 == 0
    assert bs + 1 <= N_MOD_SEQ

    cvec = jnp.concatenate([c_ctx[None, :], c, jnp.zeros((N_MOD_SEQ - 1 - bs, D_MODEL), F32)], axis=0)
    mods = _ada(cvec, ada_w, ada_b).reshape(DEPTH, N_MOD_SEQ, 6, D_MODEL)
    mods = jnp.pad(mods, ((0, 0), (0, 0), (0, MOD_ROWS - 6), (0, 0)))

    t = tp + ts
    x_src = (x_prompt.reshape(tp, D_MODEL), 0, x_sample.reshape(ts, D_MODEL), 0)

    mod = mods[0]
    (proj,) = _modmm(x_src, t, mod, a_w_in[0].astype(BF16), 0, tp, ls, 0, 1, "mlstm_in_proj")
    wqk = jnp.concatenate([_blockdiag_lanes(a_w_q[0]), _blockdiag_lanes(a_w_k[0])], axis=-1).astype(BF16)
    wv = _blockdiag_lanes(a_w_v[0]).astype(BF16)
    ngate = 4 * H_A
    wg = jnp.pad(a_w_gates[0].reshape(3, E_A, ngate), ((0, 0), (0, 0), (0, LANES - ngate))).astype(BF16)
    bg = jnp.pad(a_b_gates[0], (0, LANES - ngate)).reshape(1, LANES)
    xc, q, k, v, g = _conv_qkv(proj, a_conv_w[0], a_conv_b[0], wqk, wv, wg, bg, tp, lp, ls)
    gt = g[:, :ngate].T
    hs_p, c_new, n_new, m_new = _mlstm_scan(q, k, v, gt, 0, bp, lp, emit_state=True)
    (hs_s,) = _mlstm_scan(q, k, v, gt, tp, bs, ls,
                          init=(state_mlstm_C[:, 0], state_mlstm_n[:, 0], state_mlstm_m[:, 0]))
    x, hm, comb = _mixer_out(hs_p, hs_s, xc, proj, 1, x_src, t, mod, a_norm_w[0], a_skip[0],
                             a_w_down[0].astype(BF16), ln_w[0, 0], ln_b[0, 0],
                             *_router_weights(moe_w_rg[0], moe_b_rg[0], moe_w_re[0], moe_b_re[0]),
                             tp, ls, True, "mlstm_out")
    x_p, x_s = _moe(hm, comb, *_expert_weights(moe_w_gate[0], moe_w_up[0], moe_w_down[0]), x, mod,
                    ln_w[0, 1], ln_b[0, 1], tp, ls)

    mod = mods[1]
    x_src = (x_p, 0, _grid_to_cols(x_s.reshape(bs, ls, D_MODEL)).reshape(ts, D_MODEL), 0)
    w_in = jnp.pad(b_w_in[0], ((0, 0), (0, -b_w_in.shape[2] % LANES))).astype(BF16)
    proj, dt = _modmm(x_src, t, mod, w_in, LANES, tp, ls, 0, 1, "ssd_in_proj")
    dt = dt[:, :2 * H_B]
    xbc = _conv_silu(proj, E_B, E_B + 2 * GN_B, b_conv_w[0], b_conv_b[0], tp, lp, ls, "ssd_conv")
    lane_bcast = (2, G_B, HG_B, LANES)
    a_neg = jnp.broadcast_to(-jnp.exp(b_a_log[0]).reshape(2, G_B, HG_B, 1), lane_bcast)
    dt_bias = jnp.broadcast_to(b_dt_bias[0].reshape(2, G_B, HG_B, 1), lane_bcast)
    dtr = dt.T.reshape(2, G_B, HG_B, tp + ts)
    y_p, s_new = _ssd_scan(xbc, dtr, a_neg, dt_bias, 0, bp, lp, emit_state=True)
    (y_s,) = _ssd_scan(xbc, dtr, a_neg, dt_bias, tp, bs, ls, init=state_ssd[:, 0])
    x, hm, comb = _mixer_out(y_p, y_s, xbc, proj, 0, x_src, t, mod, b_norm_w[0], jnp.repeat(b_d_skip[0], P_B),
                             b_w_out[0].astype(BF16), ln_w[1, 0], ln_b[1, 0],
                             *_router_weights(moe_w_rg[1], moe_b_rg[1], moe_w_re[1], moe_b_re[1]),
                             tp, ls, False, "ssd_out")
    x_p, x_s = _moe(hm, comb, *_expert_weights(moe_w_gate[1], moe_w_up[1], moe_w_down[1]), x, mod,
                    ln_w[1, 1], ln_b[1, 1], tp, ls)

    y_prompt = x_p.reshape(bp, lp, D_MODEL)
    y_sample = _cols_to_grid(x_s.reshape(bs, ls, D_MODEL))
    return (y_prompt, y_sample,
            c_new[:, None],
            n_new.reshape(bp, 1, 2, H_A, DH_A),
            m_new[:, :, :, 0, 0][:, None],
            s_new[:, None])
```

```python
import functools
import math

import jax
import jax.numpy as jnp
from jax import lax
from jax.experimental import pallas as pl
from jax.experimental.pallas import tpu as pltpu

F32 = jnp.float32
BF16 = jnp.bfloat16

D_MODEL = 1024
DEPTH = 2
GRID_W = 64
CONV_W = 5
E_A = 2 * D_MODEL
H_A = 4
DH_A = E_A // H_A
QKV_BLOCK = 4
E_B = 2 * D_MODEL
P_B = 64
H_B = E_B // P_B
G_B = 4
HG_B = H_B // G_B
D_STATE = 128
GN_B = G_B * D_STATE
MOE_GROUPS = 4
MOE_EXPERTS = 4
N_EXPERTS = MOE_GROUPS * MOE_EXPERTS
D_FF_E = 256
ALPHA = (2.0 * DEPTH) ** 0.25
EPS = 1e-5

LANES = 128
SUBLANES = 8
BF16_ROWS = 16
TM = 512
TM_CONV = 256
TM_MOE = 512
SCAN_CHUNK = 256
MOD_ROWS = 8
N_MOD_SEQ = 16
VMEM_LIMIT = 56 * 1024 * 1024
ROUTER_OFF = MOE_GROUPS


def _params(sem):
    return pltpu.CompilerParams(dimension_semantics=sem, vmem_limit_bytes=VMEM_LIMIT)


def _seq_of_tile(m, tm, tp, ls):
    npt = tp // tm
    return jnp.where(m < npt, 0, 1 + (m - npt) // (ls // tm))


def _silu(x):
    return x * jax.nn.sigmoid(x)


def _softplus(x):
    return jnp.maximum(x, 0.0) + jnp.log1p(jnp.exp(-jnp.abs(x)))


def _layer_norm_rows(r, w, b):
    mu = jnp.mean(r, axis=-1, keepdims=True)
    rc = r - mu
    var = jnp.mean(rc * rc, axis=-1, keepdims=True)
    return rc * lax.rsqrt(var + EPS) * w + b


def _ada_kernel(c_ref, w_ref, b_ref, o_ref):
    s = _silu(c_ref[...])
    o_ref[...] = jnp.dot(s.astype(BF16), w_ref[...].astype(BF16), preferred_element_type=F32) + b_ref[...]


def _ada(cvec, ada_w, ada_b):
    nl = ada_w.shape[0]
    n_out = ada_w.shape[2]
    tn = 1536
    return pl.pallas_call(
        _ada_kernel,
        grid=(nl, n_out // tn),
        in_specs=[pl.BlockSpec((N_MOD_SEQ, D_MODEL), lambda l, n: (0, 0)),
                  pl.BlockSpec((None, D_MODEL, tn), lambda l, n: (l, 0, n)),
                  pl.BlockSpec((None, 1, tn), lambda l, n: (l, 0, n))],
        out_specs=pl.BlockSpec((None, N_MOD_SEQ, tn), lambda l, n: (l, 0, n)),
        out_shape=jax.ShapeDtypeStruct((nl, N_MOD_SEQ, n_out), F32),
        compiler_params=_params(("arbitrary", "arbitrary")),
        name="ada_mod",
    )(cvec, ada_w, ada_b.reshape(nl, 1, n_out))


def _token_tile_specs(src, tm, tp, width=D_MODEL):
    pa, p0, sa, s0 = src
    npt = tp // tm
    nst = (sa.shape[0] - s0) // tm
    return ([pl.BlockSpec((tm, width), lambda m, *_: (p0 // tm + jnp.minimum(m, npt - 1), 0)),
             pl.BlockSpec((tm, width), lambda m, *_: (s0 // tm + jnp.clip(m - npt, 0, nst - 1), 0))], [pa, sa])


def _select_token_tile(p_ref, s_ref, tm, tp):
    return jnp.where(pl.program_id(0) < tp // tm, p_ref[...], s_ref[...])


def _modmm_kernel(xp_ref, xs_ref, mod_ref, w_ref, o_ref, *tail_ref, tm, tp, shift_row, scale_row):
    mod = mod_ref[...]
    x = _select_token_tile(xp_ref, xs_ref, tm, tp)
    h = x * (1.0 + mod[scale_row:scale_row + 1, :]) + mod[shift_row:shift_row + 1, :]
    res = jnp.dot(h.astype(BF16), w_ref[...], preferred_element_type=F32)
    n_main = o_ref.shape[1]
    o_ref[...] = res[:, :n_main].astype(o_ref.dtype)
    if tail_ref:
        tail_ref[0][...] = res[:, n_main:]


def _resident(shape):
    return pl.BlockSpec(shape, lambda *_: (0,) * len(shape), pipeline_mode=pl.Buffered(1))


def _modmm(src, t, mod, w_bf, n_tail, tp, ls, shift_row, scale_row, name):
    n_main = w_bf.shape[1] - n_tail
    tm = TM
    x_specs, x_args = _token_tile_specs(src, tm, tp)
    out_specs = [pl.BlockSpec((tm, n_main), lambda m: (m, 0))]
    out_shape = [jax.ShapeDtypeStruct((t, n_main), BF16)]
    if n_tail:
        out_specs.append(pl.BlockSpec((tm, n_tail), lambda m: (m, 0)))
        out_shape.append(jax.ShapeDtypeStruct((t, n_tail), F32))
    return pl.pallas_call(
        functools.partial(_modmm_kernel, tm=tm, tp=tp, shift_row=shift_row, scale_row=scale_row),
        grid=(t // tm,),
        in_specs=x_specs + [
            pl.BlockSpec((None, MOD_ROWS, D_MODEL), lambda m: (_seq_of_tile(m, tm, tp, ls), 0, 0)),
            _resident((D_MODEL, w_bf.shape[1]))],
        out_specs=out_specs,
        out_shape=out_shape,
        compiler_params=_params(("arbitrary",)),
        name=name,
    )(*x_args, mod, w_bf)


def _conv_kernel(cur_ref, prev_ref, next_ref, w_ref, b_ref, o_ref, ext_ref, *, tm, tp, lp, ls, col0=0):
    m = pl.program_id(0)
    npt = tp // tm
    pos = jnp.where(m < npt, m % (lp // tm), (m - npt) % (ls // tm))
    last_pos = jnp.where(m < npt, lp // tm - 1, ls // tm - 1)
    halo = SUBLANES
    prev = jnp.where(pos == 0, 0.0, prev_ref[...].astype(F32)[BF16_ROWS - halo:, :])
    cur = cur_ref[...].astype(F32)
    nxt = jnp.where(pos == last_pos, 0.0, next_ref[...].astype(F32)[:halo, :])
    first_tap = jnp.minimum(m, 0) + (halo - CONV_W // 2)
    for c in range(ext_ref.shape[0]):
        lanes = slice(c * LANES, (c + 1) * LANES)
        ext_ref[c, 0:halo, :] = prev[:, lanes]
        ext_ref[c, halo:halo + tm, :] = cur[:, lanes]
        ext_ref[c, halo + tm:2 * halo + tm, :] = nxt[:, lanes]
    for c in range(ext_ref.shape[0]):
        lanes = slice(col0 + c * LANES, col0 + (c + 1) * LANES)
        acc = jnp.zeros((tm, LANES), F32) + b_ref[:, lanes]
        for j in range(CONV_W):
            acc = acc + ext_ref[c, pl.ds(first_tap + j, tm), :] * w_ref[j:j + 1, lanes]
        o_ref[:, lanes] = _silu(acc).astype(o_ref.dtype)


def _conv_blocks_kernel(*refs, nblk, tc, tm, tp, lp, ls):
    w_ref, b_ref, o_ref, ext_ref = refs[3 * nblk:]
    slabs = tc // LANES
    for j in range(nblk):
        _conv_kernel(*refs[3 * j:3 * j + 3], w_ref, b_ref, o_ref, ext_ref.at[pl.ds(j * slabs, slabs)],
                     tm=tm, tp=tp, lp=lp, ls=ls, col0=j * tc)


def _conv_silu(a, col0, width, w, b, tp, lp, ls, name):
    t = a.shape[0]
    tm, tc = TM_CONV, 1024
    nblk = width // tc
    cb0 = col0 // tc
    nhalo = t // BF16_ROWS
    per = tm // BF16_ROWS
    in_specs, args = [], []
    for j in range(nblk):
        in_specs += [pl.BlockSpec((tm, tc), lambda m, j=j: (m, cb0 + j)),
                     pl.BlockSpec((BF16_ROWS, tc), lambda m, j=j: (jnp.maximum(m * per - 1, 0), cb0 + j)),
                     pl.BlockSpec((BF16_ROWS, tc), lambda m, j=j: (jnp.minimum((m + 1) * per, nhalo - 1), cb0 + j))]
        args += [a, a, a]
    return pl.pallas_call(
        functools.partial(_conv_blocks_kernel, nblk=nblk, tc=tc, tm=tm, tp=tp, lp=lp, ls=ls),
        grid=(t // tm,),
        in_specs=in_specs + [pl.BlockSpec((CONV_W, width), lambda m: (0, 0)),
                             pl.BlockSpec((1, width), lambda m: (0, 0))],
        out_specs=pl.BlockSpec((tm, width), lambda m: (m, 0)),
        out_shape=jax.ShapeDtypeStruct((t, width), BF16),
        scratch_shapes=[pltpu.VMEM((width // LANES, tm + 2 * SUBLANES, LANES), F32)],
        compiler_params=_params(("arbitrary",)),
        name=name,
    )(*args, w, b.reshape(1, width))


def _qkv_kernel(xc_ref, xu_ref, wqk_ref, wv_ref, wg_ref, bg_ref, q_ref, k_ref, v_ref, g_ref):
    nblk = E_A // LANES
    g = jnp.zeros(g_ref.shape, F32) + bg_ref[...]
    for b in range(nblk):
        sl = slice(b * LANES, (b + 1) * LANES)
        xcb = xc_ref[:, sl]
        xub = xu_ref[:, sl]
        qk = jnp.dot(xcb, wqk_ref[b], preferred_element_type=F32)
        vv = jnp.dot(xub, wv_ref[b], preferred_element_type=F32)
        qb = qk[:, :LANES].astype(BF16)
        kb = qk[:, LANES:].astype(BF16)
        vb = vv.astype(BF16)
        q_ref[:, sl] = qb
        k_ref[:, sl] = kb
        v_ref[:, sl] = vb
        g = g + jnp.dot(qb, wg_ref[0, sl, :], preferred_element_type=F32)
        g = g + jnp.dot(kb, wg_ref[1, sl, :], preferred_element_type=F32)
        g = g + jnp.dot(vb, wg_ref[2, sl, :], preferred_element_type=F32)
    lane = lax.broadcasted_iota(jnp.int32, g.shape, 1)
    is_forget = (lane & (2 * H_A - 1)) >= H_A
    log_sig = jnp.minimum(g, 0.0) - jnp.log1p(jnp.exp(-jnp.abs(g)))
    g_ref[...] = jnp.where(is_forget, log_sig, g)


def _conv_qkv_kernel(cur_ref, prev_ref, next_ref, cw_ref, cb_ref, wqk_ref, wv_ref, wg_ref, bg_ref,
                     xc_ref, q_ref, k_ref, v_ref, g_ref, ext_ref, *, tm, tp, lp, ls):
    _conv_kernel(cur_ref, prev_ref, next_ref, cw_ref, cb_ref, xc_ref, ext_ref, tm=tm, tp=tp, lp=lp, ls=ls)
    _qkv_kernel(xc_ref, cur_ref, wqk_ref, wv_ref, wg_ref, bg_ref, q_ref, k_ref, v_ref, g_ref)


def _conv_qkv(proj, conv_w, conv_b, wqk, wv, wg, bg, tp, lp, ls):
    t = proj.shape[0]
    tm = TM_CONV
    nblk = E_A // LANES
    nhalo = t // BF16_ROWS
    per = tm // BF16_ROWS
    tile = pl.BlockSpec((tm, E_A), lambda m: (m, 0))
    return pl.pallas_call(
        functools.partial(_conv_qkv_kernel, tm=tm, tp=tp, lp=lp, ls=ls),
        grid=(t // tm,),
        in_specs=[tile,
                  pl.BlockSpec((BF16_ROWS, E_A), lambda m: (jnp.maximum(m * per - 1, 0), 0)),
                  pl.BlockSpec((BF16_ROWS, E_A), lambda m: (jnp.minimum((m + 1) * per, nhalo - 1), 0)),
                  pl.BlockSpec((CONV_W, E_A), lambda m: (0, 0)),
                  pl.BlockSpec((1, E_A), lambda m: (0, 0)),
                  pl.BlockSpec((nblk, LANES, 2 * LANES), lambda m: (0, 0, 0)),
                  pl.BlockSpec((nblk, LANES, LANES), lambda m: (0, 0, 0)),
                  pl.BlockSpec((3, E_A, LANES), lambda m: (0, 0, 0)),
                  pl.BlockSpec((1, LANES), lambda m: (0, 0))],
        out_specs=[tile, tile, tile, tile, pl.BlockSpec((tm, LANES), lambda m: (m, 0))],
        out_shape=[jax.ShapeDtypeStruct((t, E_A), BF16)] * 4 + [jax.ShapeDtypeStruct((t, LANES), F32)],
        scratch_shapes=[pltpu.VMEM((E_A // LANES, tm + 2 * SUBLANES, LANES), F32)],
        compiler_params=_params(("arbitrary",)),
        name="mlstm_conv_qkv_gates",
    )(proj, proj, proj, conv_w, conv_b.reshape(1, E_A), wqk, wv, wg, bg)


def _order_mask(dirn, n, rows_are_later):
    a_io = lax.broadcasted_iota(jnp.int32, (n, n), 0)
    b_io = lax.broadcasted_iota(jnp.int32, (n, n), 1)
    if (dirn == 0) == rows_are_later:
        return b_io <= a_io
    return b_io >= a_io


def _exact_cumsum_rows(rows, tri):
    nrow = rows.shape[0]
    p1 = rows.astype(BF16).astype(F32)
    r1 = rows - p1
    p2 = r1.astype(BF16).astype(F32)
    p3 = r1 - p2
    pieces = jnp.concatenate([p1, p2, p3, jnp.zeros_like(p1)], axis=0).astype(BF16)
    cs = jnp.dot(pieces, tri, preferred_element_type=F32)
    return cs[0:nrow] + cs[nrow:2 * nrow] + cs[2 * nrow:3 * nrow]


def _pick_row(blk, idx):
    sub = lax.broadcasted_iota(jnp.int32, (blk.shape[0], 1), 0)
    return jnp.sum(jnp.where(sub == idx, blk, 0.0), axis=0, keepdims=True)


def _rows_to_cols(rows):
    pad = jnp.zeros((LANES - rows.shape[0], rows.shape[1]), F32)
    return jnp.concatenate([rows, pad], axis=0).T


def _pick_col(blk, idx):
    lane = lax.broadcasted_iota(jnp.int32, (1, blk.shape[1]), 1)
    return jnp.sum(jnp.where(lane == idx, blk, 0.0), axis=1, keepdims=True)


def _mlstm_direction(dirn, h, q, k, v, s_qk, kt, gt_ref, c_ref, n_ref, m_ref, use_state, need_update):
    lc = SCAN_CHUNK
    scale = DH_A ** -0.5
    log_scale = -0.5 * math.log(DH_A)
    gates = gt_ref[dirn * 2 * H_A:(dirn + 1) * 2 * H_A, :]
    tri = jnp.where(_order_mask(dirn, lc, False), 1.0, 0.0).astype(BF16)
    csum = _exact_cumsum_rows(gates, tri)
    i_row = _pick_row(gates, h)
    f_row = _pick_row(gates, H_A + h)
    b_row = _pick_row(csum, H_A + h)
    v_row = i_row - b_row
    valid = _order_mask(dirn, lc, True)
    m_prev = m_ref[:, 0:1]
    run_max = jnp.max(jnp.where(valid, v_row, -jnp.inf), axis=1, keepdims=True)
    u_col = -jnp.maximum(m_prev, run_max)
    p = s_qk * jnp.exp(jnp.where(valid, u_col + (v_row + log_scale), -jnp.inf))
    num = jnp.dot(p.astype(BF16), v, preferred_element_type=F32)
    den = jnp.sum(p, axis=1, keepdims=True)
    if use_state:
        w_inter = jnp.exp(m_prev + u_col) * scale
        qc = jnp.dot(q, c_ref[...].astype(BF16), preferred_element_type=F32)
        qn = jnp.sum(q.astype(F32) * n_ref[...], axis=1, keepdims=True)
        num = num + w_inter * qc
        den = den + w_inter * qn
    b_col = _rows_to_cols(b_row)[:, 0:1]
    h_out = num / jnp.maximum(jnp.abs(den), jnp.exp(u_col - b_col))
    if not need_update:
        return h_out, None
    b_last = jnp.sum(f_row, axis=1, keepdims=True)
    logk = b_last + v_row
    m_new = jnp.maximum(b_last + m_prev, jnp.max(logk, axis=1, keepdims=True))
    wk = jnp.exp(logk - m_new)
    c_new = jnp.dot((kt * wk).astype(BF16), v, preferred_element_type=F32)
    wk8 = (jnp.zeros((SUBLANES, lc), F32) + wk).astype(BF16)
    n_new = jnp.dot(wk8, k, preferred_element_type=F32)[0:1, :]
    return h_out, (c_new, n_new, m_new, jnp.exp(b_last + m_prev - m_new))


def _mlstm_kernel(*refs, nc, has_init, emit_state):
    shared = nc == 1
    pos = 4 if shared else 8
    chunk_refs = [refs[0:4], refs[0:4] if shared else refs[4:8]]
    if has_init:
        c0_ref, n0_ref, m0_ref = refs[pos:pos + 3]
        pos += 3
    hs_ref = refs[pos]
    pos += 1
    if emit_state:
        cn_ref, nn_ref, mn_ref = refs[pos:pos + 3]
        pos += 3
    c_scr, n_scr, m_scr = refs[pos:pos + 3]

    lc = SCAN_CHUNK
    s_id, h, c = (pl.program_id(i) for i in range(3))
    use_state = has_init or nc > 1

    @pl.when(c == 0)
    def _():
        for dirn in range(2):
            if has_init:
                c_scr[dirn] = c0_ref[dirn]
                n_scr[dirn] = _pick_row(n0_ref[dirn], h)
                m_scr[dirn] = jnp.zeros(m_scr.shape[1:], F32) + m0_ref[(s_id * 2 + dirn) * H_A + h]
            else:
                c_scr[dirn] = jnp.zeros(c_scr.shape[1:], F32)
                n_scr[dirn] = jnp.zeros(n_scr.shape[1:], F32)
                m_scr[dirn] = jnp.zeros(m_scr.shape[1:], F32)
        if not shared:
            hs_ref[...] = jnp.zeros(hs_ref.shape, F32)

    def chunk_operands(q_ref, k_ref, v_ref, gt_ref):
        q = q_ref[...]
        k = k_ref[...]
        s_qk = lax.dot_general(q, k, (((1,), (1,)), ((), ())), preferred_element_type=F32)
        return q, k, v_ref[...], s_qk, k.astype(F32).T, gt_ref

    ops = [chunk_operands(*chunk_refs[0])]
    ops.append(ops[0] if shared else chunk_operands(*chunk_refs[1]))

    def run(need_update):
        return [_mlstm_direction(dirn, h, *ops[dirn], c_scr.at[dirn], n_scr.at[dirn], m_scr.at[dirn],
                                 use_state, need_update) for dirn in range(2)]

    def write_out(res):
        if shared:
            hs_ref[...] = res[0][0] + res[1][0]
        else:
            rows_f = pl.ds(pl.multiple_of(c * lc, lc), lc)
            rows_b = pl.ds(pl.multiple_of((nc - 1 - c) * lc, lc), lc)
            hs_ref[rows_f, :] = hs_ref[rows_f, :] + res[0][0]
            hs_ref[rows_b, :] = hs_ref[rows_b, :] + res[1][0]

    def with_update():
        res = run(True)
        write_out(res)
        for dirn in range(2):
            c_new, n_new, m_new, decay = res[dirn][1]
            if use_state:
                c_new = c_new + decay * c_scr[dirn]
                n_new = n_new + decay * n_scr[dirn]
            m_new_row = jnp.zeros(m_scr.shape[1:], F32) + m_new
            if nc > 1:
                c_scr[dirn] = c_new
                n_scr[dirn] = n_new
                m_scr[dirn] = m_new_row
            if emit_state:
                @pl.when(c == nc - 1)
                def _():
                    cn_ref[dirn] = c_new
                    nn_ref[dirn] = n_new
                    mn_ref[dirn] = m_new_row

    def without_update():
        write_out(run(False))

    if emit_state:
        with_update()
    else:
        pl.when(c < nc - 1)(with_update)
        pl.when(c == nc - 1)(without_update)


def _mlstm_scan(q, k, v, gt, row0, nseq, seqlen, init=None, emit_state=False):
    lc = SCAN_CHUNK
    nc = seqlen // lc
    blk0 = row0 // lc
    has_init = init is not None

    def blk_f(s, c):
        return blk0 + s * nc + c

    def blk_b(s, c):
        return blk0 + s * nc + nc - 1 - c

    def chunk_specs(blk):
        return [pl.BlockSpec((lc, DH_A), lambda s, h, c: (blk(s, c), h))] * 3 + [
            pl.BlockSpec((4 * H_A, lc), lambda s, h, c: (0, blk(s, c)))]

    in_specs = chunk_specs(blk_f)
    args = [q, k, v, gt]
    if nc > 1:
        in_specs += chunk_specs(blk_b)
        args += [q, k, v, gt]
    if has_init:
        c0, n0, m0 = init
        in_specs += [pl.BlockSpec((None, 2, None, DH_A, DH_A), lambda s, h, c: (s, 0, h, 0, 0)),
                     pl.BlockSpec((None, 2, H_A, DH_A), lambda s, h, c: (s, 0, 0, 0)),
                     pl.BlockSpec(memory_space=pltpu.SMEM)]
        args += [c0, n0, m0.reshape(-1)]
    out_specs = [pl.BlockSpec((seqlen, DH_A), lambda s, h, c: (s, h))]
    out_shape = [jax.ShapeDtypeStruct((nseq * seqlen, E_A), F32)]
    if emit_state:
        out_specs += [pl.BlockSpec((None, 2, None, DH_A, DH_A), lambda s, h, c: (s, 0, h, 0, 0)),
                      pl.BlockSpec((None, 2, None, 1, DH_A), lambda s, h, c: (s, 0, h, 0, 0)),
                      pl.BlockSpec((None, 2, None, 1, LANES), lambda s, h, c: (s, 0, h, 0, 0))]
        out_shape += [jax.ShapeDtypeStruct((nseq, 2, H_A, DH_A, DH_A), F32),
                      jax.ShapeDtypeStruct((nseq, 2, H_A, 1, DH_A), F32),
                      jax.ShapeDtypeStruct((nseq, 2, H_A, 1, LANES), F32)]
    return pl.pallas_call(
        functools.partial(_mlstm_kernel, nc=nc, has_init=has_init, emit_state=emit_state),
        grid=(nseq, H_A, nc),
        in_specs=in_specs,
        out_specs=out_specs,
        out_shape=out_shape,
        scratch_shapes=[pltpu.VMEM((2, DH_A, DH_A), F32), pltpu.VMEM((2, 1, DH_A), F32),
                        pltpu.VMEM((2, 1, LANES), F32)],
        compiler_params=_params(("arbitrary",) * 3),
        name="mlstm_scan_state" if emit_state else "mlstm_scan_init",
    )(*args)


def _ssd_direction(dirn, xt, bb, ct, cbt_raw, dtr_ref, a_ref, bias_ref, h_ref, yt_ref, xw_ref, use_state):
    lc = SCAN_CHUNK
    r_io = lax.broadcasted_iota(jnp.int32, (lc, lc), 0)
    s_io = lax.broadcasted_iota(jnp.int32, (lc, lc), 1)
    valid = (r_io <= s_io) if dirn == 0 else (r_io >= s_io)
    tri = jnp.where(valid, 1.0, 0.0).astype(BF16)

    dt_rows = _softplus(dtr_ref[...] + bias_ref[:, 0:1])
    a_rows = dt_rows * a_ref[:, 0:1]
    a1 = a_rows.astype(BF16).astype(F32)
    r1 = a_rows - a1
    a2 = r1.astype(BF16).astype(F32)
    a3 = r1 - a2
    pieces = jnp.concatenate([a1, a2, a3, jnp.zeros_like(a1)], axis=0).astype(BF16)
    cs = jnp.dot(pieces, tri, preferred_element_type=F32)
    acs_rows = cs[0:HG_B] + cs[HG_B:2 * HG_B] + cs[2 * HG_B:3 * HG_B]
    a_last = jnp.sum(a_rows, axis=1, keepdims=True)
    wr_rows = jnp.exp(a_last - acs_rows) * dt_rows
    acs_cols = jnp.concatenate([acs_rows, jnp.zeros((LANES - HG_B, lc), F32)], axis=0).T

    cbt = jnp.where(valid, cbt_raw, 0.0)
    if use_state:
        inter_t = jnp.dot(h_ref[...].astype(BF16), ct, preferred_element_type=F32)
        e_rows = jnp.exp(acs_rows)
    for hg in range(HG_B):
        sl = slice(hg * P_B, (hg + 1) * P_B)
        row = slice(hg, hg + 1)
        seg_t = acs_rows[row, :] - acs_cols[:, hg:hg + 1]
        wt = cbt * jnp.exp(jnp.minimum(seg_t, 0.0))
        xth = xt[sl, :]
        yth = jnp.dot((xth * dt_rows[row, :]).astype(BF16), wt.astype(BF16), preferred_element_type=F32)
        if use_state:
            yth = yth + inter_t[sl, :] * e_rows[row, :]
        yt_ref[sl, :] = yth
        xw_ref[sl, :] = xth * wr_rows[row, :]
    return a_last


def _ssd_kernel(*refs, nc, has_init, emit_state):
    shared = nc == 1
    pos = 3 if shared else 6
    chunk_refs = [refs[0:3], refs[0:3] if shared else refs[3:6]]
    dtr_refs = refs[pos:pos + 2]
    a_ref, bias_ref = refs[pos + 2:pos + 4]
    pos += 4
    if has_init:
        h0_ref = refs[pos]
        pos += 1
    y_ref = refs[pos]
    pos += 1
    if emit_state:
        sn_ref = refs[pos]
        pos += 1
    h_scr, yt_scr, xw_scr = refs[pos:pos + 3]

    lc = SCAN_CHUNK
    c = pl.program_id(2)
    use_state = has_init or nc > 1
    hp = HG_B * P_B

    @pl.when(c == 0)
    def _():
        if has_init:
            h_scr[...] = h0_ref[...].reshape(2, hp, D_STATE)
        else:
            h_scr[...] = jnp.zeros(h_scr.shape, F32)
        if not shared:
            y_ref[...] = jnp.zeros(y_ref.shape, F32)

    def chunk_operands(x_ref, b_ref, c_ref):
        bb = b_ref[...]
        ct = c_ref[...].astype(F32).T.astype(BF16)
        return x_ref[...].astype(F32).T, bb, ct, jnp.dot(bb, ct, preferred_element_type=F32)

    ops = [chunk_operands(*chunk_refs[0])]
    ops.append(ops[0] if shared else chunk_operands(*chunk_refs[1]))
    a_last = [_ssd_direction(dirn, *ops[dirn], dtr_refs[dirn], a_ref.at[dirn], bias_ref.at[dirn], h_scr.at[dirn],
                             yt_scr.at[dirn], xw_scr.at[dirn], use_state) for dirn in range(2)]

    if shared:
        y_ref[...] = (yt_scr[0] + yt_scr[1]).T
    else:
        rows_f = pl.ds(pl.multiple_of(c * lc, lc), lc)
        rows_b = pl.ds(pl.multiple_of((nc - 1 - c) * lc, lc), lc)
        y_ref[rows_f, :] = y_ref[rows_f, :] + yt_scr[0].T
        y_ref[rows_b, :] = y_ref[rows_b, :] + yt_scr[1].T

    def update_state():
        for dirn in range(2):
            upd = jnp.dot(xw_scr[dirn].astype(BF16), ops[dirn][1], preferred_element_type=F32)
            dec = jnp.exp(a_last[dirn])
            for hg in range(HG_B):
                sl = slice(hg * P_B, (hg + 1) * P_B)
                new = upd[sl, :]
                if use_state:
                    new = new + dec[hg:hg + 1, :] * h_scr[dirn, sl, :]
                h_scr[dirn, sl, :] = new
        if emit_state:
            @pl.when(c == nc - 1)
            def _():
                sn_ref[...] = h_scr[...].reshape(2, HG_B, P_B, D_STATE)

    if emit_state:
        update_state()
    elif nc > 1:
        pl.when(c < nc - 1)(update_state)


def _ssd_scan(xbc, dtr, a_neg, dt_bias, row0, nseq, seqlen, init=None, emit_state=False):
    lc = SCAN_CHUNK
    nc = seqlen // lc
    blk0 = row0 // lc
    has_init = init is not None
    hp = HG_B * P_B
    b_off = E_B // D_STATE
    c_off = (E_B + GN_B) // D_STATE

    def blk_f(s, c):
        return blk0 + s * nc + c

    def blk_b(s, c):
        return blk0 + s * nc + nc - 1 - c

    def chunk_specs(blk):
        return [pl.BlockSpec((lc, hp), lambda s, g, c: (blk(s, c), g)),
                pl.BlockSpec((lc, D_STATE), lambda s, g, c: (blk(s, c), b_off + g)),
                pl.BlockSpec((lc, D_STATE), lambda s, g, c: (blk(s, c), c_off + g))]

    in_specs = chunk_specs(blk_f)
    args = [xbc, xbc, xbc]
    if nc > 1:
        in_specs += chunk_specs(blk_b)
        args += [xbc, xbc, xbc]
    table_spec = pl.BlockSpec((2, None, HG_B, LANES), lambda s, g, c: (0, g, 0, 0))
    in_specs += [pl.BlockSpec((None, None, HG_B, lc), lambda s, g, c: (0, g, 0, blk_f(s, c))),
                 pl.BlockSpec((None, None, HG_B, lc), lambda s, g, c: (1, g, 0, blk_b(s, c))),
                 table_spec, table_spec]
    args += [dtr, dtr, a_neg, dt_bias]
    state_spec = pl.BlockSpec((None, 2, HG_B, P_B, D_STATE), lambda s, g, c: (s, 0, g, 0, 0))
    if has_init:
        in_specs.append(state_spec)
        args.append(init)
    out_specs = [pl.BlockSpec((seqlen, hp), lambda s, g, c: (s, g))]
    out_shape = [jax.ShapeDtypeStruct((nseq * seqlen, E_B), F32)]
    if emit_state:
        out_specs.append(state_spec)
        out_shape.append(jax.ShapeDtypeStruct((nseq, 2, H_B, P_B, D_STATE), F32))
    return pl.pallas_call(
        functools.partial(_ssd_kernel, nc=nc, has_init=has_init, emit_state=emit_state),
        grid=(nseq, G_B, nc),
        in_specs=in_specs,
        out_specs=out_specs,
        out_shape=out_shape,
        scratch_shapes=[pltpu.VMEM((2, hp, D_STATE), F32), pltpu.VMEM((2, hp, lc), F32),
                        pltpu.VMEM((2, hp, lc), F32)],
        compiler_params=_params(("arbitrary",) * 3),
        name="ssd_scan_state" if emit_state else "ssd_scan_init",
    )(*args)


def _mixer_out_kernel(yp_ref, ys_ref, u_ref, z_ref, xp_ref, xs_ref, mod_ref, nw_ref, sk_ref, w_ref, lnw_ref, lnb_ref,
                      wr_ref, br_ref, o_ref, hm_ref, comb_ref, *, tp, grouped):
    y = _select_token_tile(yp_ref, ys_ref, TM, tp)
    u = u_ref[...].astype(F32)
    gate = _silu(z_ref[...].astype(F32))
    if grouped:
        parts = []
        for hd in range(H_A):
            seg = y[:, hd * DH_A:(hd + 1) * DH_A]
            mu = jnp.mean(seg, axis=-1, keepdims=True)
            sc = seg - mu
            var = jnp.mean(sc * sc, axis=-1, keepdims=True)
            parts.append(sc * lax.rsqrt(var + EPS))
        hn = jnp.concatenate(parts, axis=1) * nw_ref[...]
        t = (hn + sk_ref[...] * u) * gate
    else:
        t0 = (y + sk_ref[...] * u) * gate
        t = t0 * lax.rsqrt(jnp.mean(t0 * t0, axis=-1, keepdims=True) + EPS) * nw_ref[...]
    out = jnp.dot(t.astype(BF16), w_ref[...], preferred_element_type=F32)
    mod = mod_ref[...]
    r = ALPHA * _select_token_tile(xp_ref, xs_ref, TM, tp) + mod[2:3, :] * out
    x_new = _layer_norm_rows(r, lnw_ref[...], lnb_ref[...])
    o_ref[...] = x_new
    hm_ref[...], comb_ref[...] = _route(x_new, mod, wr_ref[...], br_ref[...])


def _mixer_out(yp, ys, u, proj, zblk, x_src, t, mod, nw, sk, w_bf, lnw, lnb, w_r, b_r, tp, ls, grouped, name):
    e = w_bf.shape[0]
    y_specs, y_args = _token_tile_specs((yp, 0, ys, 0), TM, tp, e)
    x_specs, x_args = _token_tile_specs(x_src, TM, tp)
    vec_e = pl.BlockSpec((1, e), lambda m: (0, 0))
    vec_d = pl.BlockSpec((1, D_MODEL), lambda m: (0, 0))
    return pl.pallas_call(
        functools.partial(_mixer_out_kernel, tp=tp, grouped=grouped),
        grid=(t // TM,),
        in_specs=y_specs + [
                  pl.BlockSpec((TM, e), lambda m: (m, 0)),
                  pl.BlockSpec((TM, e), lambda m: (m, zblk))] + x_specs + [
                  pl.BlockSpec((None, MOD_ROWS, D_MODEL), lambda m: (_seq_of_tile(m, TM, tp, ls), 0, 0)),
                  vec_e, vec_e,
                  pl.BlockSpec((e, D_MODEL), lambda m: (0, 0)),
                  vec_d, vec_d,
                  pl.BlockSpec((D_MODEL, LANES), lambda m: (0, 0)),
                  pl.BlockSpec((1, LANES), lambda m: (0, 0))],
        out_specs=[pl.BlockSpec((TM, D_MODEL), lambda m: (m, 0)),
                   pl.BlockSpec((TM, D_MODEL), lambda m: (m, 0)),
                   pl.BlockSpec((TM, LANES), lambda m: (m, 0))],
        out_shape=[jax.ShapeDtypeStruct((t, D_MODEL), F32), jax.ShapeDtypeStruct((t, D_MODEL), BF16),
                   jax.ShapeDtypeStruct((t, LANES), F32)],
        compiler_params=_params(("arbitrary",)),
        name=name,
    )(*y_args, u, proj, *x_args, mod, nw.reshape(1, e), sk.reshape(1, e), w_bf, lnw.reshape(1, D_MODEL),
      lnb.reshape(1, D_MODEL), w_r, b_r)


def _route(x, mod, w_r, b_r):
    hm = (x * (1.0 + mod[4:5, :]) + mod[3:4, :]).astype(BF16)
    logits = jnp.dot(hm, w_r, preferred_element_type=F32) + b_r
    n_used = ROUTER_OFF + N_EXPERTS
    n_rows = -(-n_used // SUBLANES) * SUBLANES
    lt = logits.T[:n_rows, :]
    row = lax.broadcasted_iota(jnp.int32, lt.shape, 0).astype(F32)
    big = float(2 * LANES)
    glog = jnp.where(row < MOE_GROUPS, lt, -jnp.inf)
    ge = jnp.exp(glog - jnp.max(glog, axis=0, keepdims=True))
    prob = ge / jnp.sum(ge, axis=0, keepdims=True)
    gp = jnp.max(prob, axis=0, keepdims=True)
    gi = jnp.min(jnp.where(prob == gp, row, big), axis=0, keepdims=True)
    lo = ROUTER_OFF + gi * MOE_EXPERTS
    ev = jnp.where((row >= lo) & (row < lo + MOE_EXPERTS), lt, -jnp.inf)
    e1 = jnp.max(ev, axis=0, keepdims=True)
    i1 = jnp.min(jnp.where(ev == e1, row, big), axis=0, keepdims=True)
    ev2 = jnp.where(row == i1, -jnp.inf, ev)
    e2 = jnp.max(ev2, axis=0, keepdims=True)
    i2 = jnp.min(jnp.where(ev2 == e2, row, big), axis=0, keepdims=True)
    t2 = jnp.exp(e2 - e1)
    w1 = 1.0 / (1.0 + t2)
    w2 = t2 / (1.0 + t2)
    comb_t = jnp.where(row == i1, w1 * gp, jnp.where(row == i2, w2 * gp, 0.0))
    return hm, _rows_to_cols(comb_t)


def _moe_kernel(hm_ref, comb_ref, wg_ref, wu_ref, wd_ref, x_ref, mod_ref, lnw_ref, lnb_ref, op_ref, os_ref, *, npt):
    hm = hm_ref[...]
    comb = comb_ref[...]
    acc = None
    for g in range(MOE_GROUPS):
        a = _silu(jnp.dot(hm, wg_ref[g], preferred_element_type=F32)) * jnp.dot(
            hm, wu_ref[g], preferred_element_type=F32)
        parts = []
        for e in range(MOE_EXPERTS):
            lane = ROUTER_OFF + g * MOE_EXPERTS + e
            parts.append(a[:, e * D_FF_E:(e + 1) * D_FF_E] * comb[:, lane:lane + 1])
        part = jnp.dot(jnp.concatenate(parts, axis=1).astype(BF16), wd_ref[g], preferred_element_type=F32)
        acc = part if acc is None else acc + part
    mod = mod_ref[...]
    r = ALPHA * x_ref[...] + mod[5:6, :] * acc
    y = _layer_norm_rows(r, lnw_ref[...], lnb_ref[...])
    m = pl.program_id(0)

    @pl.when(m < npt)
    def _():
        op_ref[...] = y

    @pl.when(m >= npt)
    def _():
        os_ref[...] = y


def _moe(hm, comb, wg, wu, wd, x, mod, lnw, lnb, tp, ls):
    t = x.shape[0]
    tm = TM_MOE
    npt = tp // tm
    nst = (t - tp) // tm
    gf = MOE_EXPERTS * D_FF_E
    vec_d = pl.BlockSpec((1, D_MODEL), lambda m: (0, 0))
    return pl.pallas_call(
        functools.partial(_moe_kernel, npt=npt),
        grid=(t // tm,),
        in_specs=[pl.BlockSpec((tm, D_MODEL), lambda m: (m, 0)),
                  pl.BlockSpec((tm, LANES), lambda m: (m, 0)),
                  _resident((MOE_GROUPS, D_MODEL, gf)),
                  _resident((MOE_GROUPS, D_MODEL, gf)),
                  _resident((MOE_GROUPS, gf, D_MODEL)),
                  pl.BlockSpec((tm, D_MODEL), lambda m: (m, 0)),
                  pl.BlockSpec((None, MOD_ROWS, D_MODEL), lambda m: (_seq_of_tile(m, tm, tp, ls), 0, 0)),
                  vec_d, vec_d],
        out_specs=[pl.BlockSpec((tm, D_MODEL), lambda m: (jnp.minimum(m, npt - 1), 0)),
                   pl.BlockSpec((tm, D_MODEL), lambda m: (jnp.clip(m - npt, 0, nst - 1), 0))],
        out_shape=[jax.ShapeDtypeStruct((tp, D_MODEL), F32), jax.ShapeDtypeStruct((t - tp, D_MODEL), F32)],
        compiler_params=_params(("arbitrary",)),
        name="moe_experts",
    )(hm, comb, wg, wu, wd, x, mod, lnw.reshape(1, D_MODEL), lnb.reshape(1, D_MODEL))


def _router_weights(w_rg, b_rg, w_re, b_re):
    w_r = jnp.concatenate([w_rg, jnp.moveaxis(w_re, 0, 1).reshape(D_MODEL, N_EXPERTS)], axis=1)
    w_r = jnp.pad(w_r, ((0, 0), (0, LANES - w_r.shape[1]))).astype(BF16)
    b_r = jnp.pad(jnp.concatenate([b_rg, b_re.reshape(-1)]), (0, LANES - MOE_GROUPS - N_EXPERTS)).reshape(1, LANES)
    return w_r, b_r


def _expert_weights(w_gate, w_up, w_down):
    gf = MOE_EXPERTS * D_FF_E
    wg = w_gate.transpose(0, 2, 1, 3).reshape(MOE_GROUPS, D_MODEL, gf).astype(BF16)
    wu = w_up.transpose(0, 2, 1, 3).reshape(MOE_GROUPS, D_MODEL, gf).astype(BF16)
    wd = w_down.reshape(MOE_GROUPS, gf, D_MODEL).astype(BF16)
    return wg, wu, wd


def _blockdiag_lanes(w):
    per = LANES // QKV_BLOCK
    wb = w.reshape(-1, per, QKV_BLOCK, QKV_BLOCK)
    eye = jnp.eye(per, dtype=w.dtype)
    return jnp.einsum('bncd,nm->bncmd', wb, eye).reshape(-1, LANES, LANES)


def _grid_to_cols(x):
    b, l, dd = x.shape
    return x.reshape(b, l // GRID_W, GRID_W, dd).transpose(0, 2, 1, 3).reshape(b, l, dd)


def _cols_to_grid(x):
    b, l, dd = x.shape
    return x.reshape(b, GRID_W, l // GRID_W, dd).transpose(0, 2, 1, 3).reshape(b, l, dd)


def kernel(x_prompt, x_sample, state_mlstm_C, state_mlstm_n, state_mlstm_m, state_ssd, c, c_ctx, ada_w, ada_b, ln_w, ln_b, a_w_in, a_conv_w, a_conv_b, a_w_q, a_w_k, a_w_v, a_w_gates, a_b_gates, a_norm_w, a_skip, a_w_down, b_w_in, b_conv_w, b_conv_b, b_dt_bias, b_a_log, b_d_skip, b_norm_w, b_w_out, moe_w_rg, moe_b_rg, moe_w_re, moe_b_re, moe_w_gate, moe_w_up, moe_w_down):
    bp, lp, _ = x_prompt.shape
    bs, ls, _ = x_sample.shape
    tp, ts = bp * lp, bs * ls
    assert lp % SCAN_CHUNK == 0 and ls % SCAN_CHUNK == 0 and tp % 512 == 0 and ls % 512 == 0
    assert bs + 1 <= N_MOD_SEQ

    cvec = jnp.concatenate([c_ctx[None, :], c, jnp.zeros((N_MOD_SEQ - 1 - bs, D_MODEL), F32)], axis=0)
    mods = _ada(cvec, ada_w, ada_b).reshape(DEPTH, N_MOD_SEQ, 6, D_MODEL)
    mods = jnp.pad(mods, ((0, 0), (0, 0), (0, MOD_ROWS - 6), (0, 0)))

    t = tp + ts
    x_src = (x_prompt.reshape(tp, D_MODEL), 0, x_sample.reshape(ts, D_MODEL), 0)

    mod = mods[0]
    (proj,) = _modmm(x_src, t, mod, a_w_in[0].astype(BF16), 0, tp, ls, 0, 1, "mlstm_in_proj")
    wqk = jnp.concatenate([_blockdiag_lanes(a_w_q[0]), _blockdiag_lanes(a_w_k[0])], axis=-1).astype(BF16)
    wv = _blockdiag_lanes(a_w_v[0]).astype(BF16)
    ngate = 4 * H_A
    wg = jnp.pad(a_w_gates[0].reshape(3, E_A, ngate), ((0, 0), (0, 0), (0, LANES - ngate))).astype(BF16)
    bg = jnp.pad(a_b_gates[0], (0, LANES - ngate)).reshape(1, LANES)
    xc, q, k, v, g = _conv_qkv(proj, a_conv_w[0], a_conv_b[0], wqk, wv, wg, bg, tp, lp, ls)
    gt = g[:, :ngate].T
    hs_p, c_new, n_new, m_new = _mlstm_scan(q, k, v, gt, 0, bp, lp, emit_state=True)
    (hs_s,) = _mlstm_scan(q, k, v, gt, tp, bs, ls,
                          init=(state_mlstm_C[:, 0], state_mlstm_n[:, 0], state_mlstm_m[:, 0]))
    x, hm, comb = _mixer_out(hs_p, hs_s, xc, proj, 1, x_src, t, mod, a_norm_w[0], a_skip[0],
                             a_w_down[0].astype(BF16), ln_w[0, 0], ln_b[0, 0],
                             *_router_weights(moe_w_rg[0], moe_b_rg[0], moe_w_re[0], moe_b_re[0]),
                             tp, ls, True, "mlstm_out")
    x_p, x_s = _moe(hm, comb, *_expert_weights(moe_w_gate[0], moe_w_up[0], moe_w_down[0]), x, mod,
                    ln_w[0, 1], ln_b[0, 1], tp, ls)

    mod = mods[1]
    x_src = (x_p, 0, _grid_to_cols(x_s.reshape(bs, ls, D_MODEL)).reshape(ts, D_MODEL), 0)
    w_in = jnp.pad(b_w_in[0].astype(BF16), ((0, 0), (0, -b_w_in.shape[2] % LANES)))
    proj, dt = _modmm(x_src, t, mod, w_in, LANES, tp, ls, 0, 1, "ssd_in_proj")
    dt = dt[:, :2 * H_B]
    xbc = _conv_silu(proj, E_B, E_B + 2 * GN_B, b_conv_w[0], b_conv_b[0], tp, lp, ls, "ssd_conv")
    lane_bcast = (2, G_B, HG_B, LANES)
    a_neg = jnp.broadcast_to(-jnp.exp(b_a_log[0]).reshape(2, G_B, HG_B, 1), lane_bcast)
    dt_bias = jnp.broadcast_to(b_dt_bias[0].reshape(2, G_B, HG_B, 1), lane_bcast)
    dtr = dt.T.reshape(2, G_B, HG_B, tp + ts)
    y_p, s_new = _ssd_scan(xbc, dtr, a_neg, dt_bias, 0, bp, lp, emit_state=True)
    (y_s,) = _ssd_scan(xbc, dtr, a_neg, dt_bias, tp, bs, ls, init=state_ssd[:, 0])
    x, hm, comb = _mixer_out(y_p, y_s, xbc, proj, 0, x_src, t, mod, b_norm_w[0], jnp.repeat(b_d_skip[0], P_B),
                             b_w_out[0].astype(BF16), ln_w[1, 0], ln_b[1, 0],
                             *_router_weights(moe_w_rg[1], moe_b_rg[1], moe_w_re[1], moe_b_re[1]),
                             tp, ls, False, "ssd_out")
    x_p, x_s = _moe(hm, comb, *_expert_weights(moe_w_gate[1], moe_w_up[1], moe_w_down[1]), x, mod,
                    ln_w[1, 1], ln_b[1, 1], tp, ls)

    y_prompt = x_p.reshape(bp, lp, D_MODEL)
    y_sample = _cols_to_grid(x_s.reshape(bs, ls, D_MODEL))
    return (y_prompt, y_sample,
            c_new[:, None],
            n_new.reshape(bp, 1, 2, H_A, DH_A),
            m_new[:, :, :, 0, 0][:, None],
            s_new[:, None])
```

```python
import functools
import math

import jax
import jax.numpy as jnp
from jax import lax
from jax.experimental import pallas as pl
from jax.experimental.pallas import tpu as pltpu

F32 = jnp.float32
BF16 = jnp.bfloat16

D_MODEL = 1024
DEPTH = 2
GRID_W = 64
CONV_W = 5
E_A = 2 * D_MODEL
H_A = 4
DH_A = E_A // H_A
QKV_BLOCK = 4
E_B = 2 * D_MODEL
P_B = 64
H_B = E_B // P_B
G_B = 4
HG_B = H_B // G_B
D_STATE = 128
GN_B = G_B * D_STATE
MOE_GROUPS = 4
MOE_EXPERTS = 4
N_EXPERTS = MOE_GROUPS * MOE_EXPERTS
D_FF_E = 256
ALPHA = (2.0 * DEPTH) ** 0.25
EPS = 1e-5

LANES = 128
SUBLANES = 8
BF16_ROWS = 16
TM = 512
TM_CONV = 256
TM_MOE = 512
SCAN_CHUNK = 256
SSD_GROUPS_PER_STEP = 2
MOD_ROWS = 8
N_MOD_SEQ = 16
VMEM_LIMIT = 56 * 1024 * 1024
ROUTER_OFF = MOE_GROUPS


def _params(sem):
    return pltpu.CompilerParams(dimension_semantics=sem, vmem_limit_bytes=VMEM_LIMIT)


def _seq_of_tile(m, tm, tp, ls):
    npt = tp // tm
    return jnp.where(m < npt, 0, 1 + (m - npt) // (ls // tm))


def _silu(x):
    return x * jax.nn.sigmoid(x)


def _softplus(x):
    return jnp.maximum(x, 0.0) + jnp.log1p(jnp.exp(-jnp.abs(x)))


def _layer_norm_rows(r, w, b):
    mu = jnp.mean(r, axis=-1, keepdims=True)
    rc = r - mu
    var = jnp.mean(rc * rc, axis=-1, keepdims=True)
    return rc * lax.rsqrt(var + EPS) * w + b


def _ada_kernel(c_ref, w_ref, b_ref, o_ref):
    s = _silu(c_ref[...])
    o_ref[...] = jnp.dot(s.astype(BF16), w_ref[...].astype(BF16), preferred_element_type=F32) + b_ref[...]


def _ada(cvec, ada_w, ada_b):
    nl = ada_w.shape[0]
    n_out = ada_w.shape[2]
    tn = 1536
    return pl.pallas_call(
        _ada_kernel,
        grid=(nl, n_out // tn),
        in_specs=[pl.BlockSpec((N_MOD_SEQ, D_MODEL), lambda l, n: (0, 0)),
                  pl.BlockSpec((None, D_MODEL, tn), lambda l, n: (l, 0, n)),
                  pl.BlockSpec((None, 1, tn), lambda l, n: (l, 0, n))],
        out_specs=pl.BlockSpec((None, N_MOD_SEQ, tn), lambda l, n: (l, 0, n)),
        out_shape=jax.ShapeDtypeStruct((nl, N_MOD_SEQ, n_out), F32),
        compiler_params=_params(("arbitrary", "arbitrary")),
        name="ada_mod",
    )(cvec, ada_w, ada_b.reshape(nl, 1, n_out))


def _token_tile_specs(src, tm, tp, width=D_MODEL):
    pa, p0, sa, s0 = src
    npt = tp // tm
    nst = (sa.shape[0] - s0) // tm
    return ([pl.BlockSpec((tm, width), lambda m, *_: (p0 // tm + jnp.minimum(m, npt - 1), 0)),
             pl.BlockSpec((tm, width), lambda m, *_: (s0 // tm + jnp.clip(m - npt, 0, nst - 1), 0))], [pa, sa])


def _select_token_tile(p_ref, s_ref, tm, tp):
    return jnp.where(pl.program_id(0) < tp // tm, p_ref[...], s_ref[...])


def _modmm_kernel(xp_ref, xs_ref, mod_ref, w_ref, o_ref, *tail_ref, tm, tp, shift_row, scale_row):
    mod = mod_ref[...]
    x = _select_token_tile(xp_ref, xs_ref, tm, tp)
    h = x * (1.0 + mod[scale_row:scale_row + 1, :]) + mod[shift_row:shift_row + 1, :]
    res = jnp.dot(h.astype(BF16), w_ref[...], preferred_element_type=F32)
    n_main = o_ref.shape[1]
    o_ref[...] = res[:, :n_main].astype(o_ref.dtype)
    if tail_ref:
        tail_ref[0][...] = res[:, n_main:]


def _resident(shape):
    return pl.BlockSpec(shape, lambda *_: (0,) * len(shape), pipeline_mode=pl.Buffered(1))


def _modmm(src, t, mod, w_bf, n_tail, tp, ls, shift_row, scale_row, name):
    n_main = w_bf.shape[1] - n_tail
    tm = TM
    x_specs, x_args = _token_tile_specs(src, tm, tp)
    out_specs = [pl.BlockSpec((tm, n_main), lambda m: (m, 0))]
    out_shape = [jax.ShapeDtypeStruct((t, n_main), BF16)]
    if n_tail:
        out_specs.append(pl.BlockSpec((tm, n_tail), lambda m: (m, 0)))
        out_shape.append(jax.ShapeDtypeStruct((t, n_tail), F32))
    return pl.pallas_call(
        functools.partial(_modmm_kernel, tm=tm, tp=tp, shift_row=shift_row, scale_row=scale_row),
        grid=(t // tm,),
        in_specs=x_specs + [
            pl.BlockSpec((None, MOD_ROWS, D_MODEL), lambda m: (_seq_of_tile(m, tm, tp, ls), 0, 0)),
            _resident((D_MODEL, w_bf.shape[1]))],
        out_specs=out_specs,
        out_shape=out_shape,
        compiler_params=_params(("arbitrary",)),
        name=name,
    )(*x_args, mod, w_bf)


def _conv_kernel(cur_ref, prev_ref, next_ref, w_ref, b_ref, o_ref, ext_ref, *, tm, tp, lp, ls, col0=0):
    m = pl.program_id(0)
    npt = tp // tm
    pos = jnp.where(m < npt, m % (lp // tm), (m - npt) % (ls // tm))
    last_pos = jnp.where(m < npt, lp // tm - 1, ls // tm - 1)
    halo = SUBLANES
    prev = jnp.where(pos == 0, 0.0, prev_ref[...].astype(F32)[BF16_ROWS - halo:, :])
    cur = cur_ref[...].astype(F32)
    nxt = jnp.where(pos == last_pos, 0.0, next_ref[...].astype(F32)[:halo, :])
    first_tap = jnp.minimum(m, 0) + (halo - CONV_W // 2)
    for c in range(ext_ref.shape[0]):
        lanes = slice(c * LANES, (c + 1) * LANES)
        ext_ref[c, 0:halo, :] = prev[:, lanes]
        ext_ref[c, halo:halo + tm, :] = cur[:, lanes]
        ext_ref[c, halo + tm:2 * halo + tm, :] = nxt[:, lanes]
    for c in range(ext_ref.shape[0]):
        lanes = slice(col0 + c * LANES, col0 + (c + 1) * LANES)
        acc = jnp.zeros((tm, LANES), F32) + b_ref[:, lanes]
        for j in range(CONV_W):
            acc = acc + ext_ref[c, pl.ds(first_tap + j, tm), :] * w_ref[j:j + 1, lanes]
        o_ref[:, lanes] = _silu(acc).astype(o_ref.dtype)


def _conv_blocks_kernel(*refs, nblk, tc, tm, tp, lp, ls):
    w_ref, b_ref, o_ref, ext_ref = refs[3 * nblk:]
    slabs = tc // LANES
    for j in range(nblk):
        _conv_kernel(*refs[3 * j:3 * j + 3], w_ref, b_ref, o_ref, ext_ref.at[pl.ds(j * slabs, slabs)],
                     tm=tm, tp=tp, lp=lp, ls=ls, col0=j * tc)


def _conv_silu(a, col0, width, w, b, tp, lp, ls, name):
    t = a.shape[0]
    tm, tc = TM_CONV, 1024
    nblk = width // tc
    cb0 = col0 // tc
    nhalo = t // BF16_ROWS
    per = tm // BF16_ROWS
    in_specs, args = [], []
    for j in range(nblk):
        in_specs += [pl.BlockSpec((tm, tc), lambda m, j=j: (m, cb0 + j)),
                     pl.BlockSpec((BF16_ROWS, tc), lambda m, j=j: (jnp.maximum(m * per - 1, 0), cb0 + j)),
                     pl.BlockSpec((BF16_ROWS, tc), lambda m, j=j: (jnp.minimum((m + 1) * per, nhalo - 1), cb0 + j))]
        args += [a, a, a]
    return pl.pallas_call(
        functools.partial(_conv_blocks_kernel, nblk=nblk, tc=tc, tm=tm, tp=tp, lp=lp, ls=ls),
        grid=(t // tm,),
        in_specs=in_specs + [pl.BlockSpec((CONV_W, width), lambda m: (0, 0)),
                             pl.BlockSpec((1, width), lambda m: (0, 0))],
        out_specs=pl.BlockSpec((tm, width), lambda m: (m, 0)),
        out_shape=jax.ShapeDtypeStruct((t, width), BF16),
        scratch_shapes=[pltpu.VMEM((width // LANES, tm + 2 * SUBLANES, LANES), F32)],
        compiler_params=_params(("arbitrary",)),
        name=name,
    )(*args, w, b.reshape(1, width))


def _qkv_kernel(xc_ref, xu_ref, wqk_ref, wv_ref, wg_ref, bg_ref, q_ref, k_ref, v_ref, g_ref):
    nblk = E_A // LANES
    g = jnp.zeros(g_ref.shape, F32) + bg_ref[...]
    for b in range(nblk):
        sl = slice(b * LANES, (b + 1) * LANES)
        xcb = xc_ref[:, sl]
        xub = xu_ref[:, sl]
        qk = jnp.dot(xcb, wqk_ref[b], preferred_element_type=F32)
        vv = jnp.dot(xub, wv_ref[b], preferred_element_type=F32)
        qb = qk[:, :LANES].astype(BF16)
        kb = qk[:, LANES:].astype(BF16)
        vb = vv.astype(BF16)
        q_ref[:, sl] = qb
        k_ref[:, sl] = kb
        v_ref[:, sl] = vb
        g = g + jnp.dot(qb, wg_ref[0, sl, :], preferred_element_type=F32)
        g = g + jnp.dot(kb, wg_ref[1, sl, :], preferred_element_type=F32)
        g = g + jnp.dot(vb, wg_ref[2, sl, :], preferred_element_type=F32)
    lane = lax.broadcasted_iota(jnp.int32, g.shape, 1)
    is_forget = (lane & (2 * H_A - 1)) >= H_A
    log_sig = jnp.minimum(g, 0.0) - jnp.log1p(jnp.exp(-jnp.abs(g)))
    g_ref[...] = jnp.where(is_forget, log_sig, g)


def _conv_qkv_kernel(cur_ref, prev_ref, next_ref, cw_ref, cb_ref, wqk_ref, wv_ref, wg_ref, bg_ref,
                     xc_ref, q_ref, k_ref, v_ref, g_ref, ext_ref, *, tm, tp, lp, ls):
    _conv_kernel(cur_ref, prev_ref, next_ref, cw_ref, cb_ref, xc_ref, ext_ref, tm=tm, tp=tp, lp=lp, ls=ls)
    _qkv_kernel(xc_ref, cur_ref, wqk_ref, wv_ref, wg_ref, bg_ref, q_ref, k_ref, v_ref, g_ref)


def _conv_qkv(proj, conv_w, conv_b, wqk, wv, wg, bg, tp, lp, ls):
    t = proj.shape[0]
    tm = TM_CONV
    nblk = E_A // LANES
    nhalo = t // BF16_ROWS
    per = tm // BF16_ROWS
    tile = pl.BlockSpec((tm, E_A), lambda m: (m, 0))
    return pl.pallas_call(
        functools.partial(_conv_qkv_kernel, tm=tm, tp=tp, lp=lp, ls=ls),
        grid=(t // tm,),
        in_specs=[tile,
                  pl.BlockSpec((BF16_ROWS, E_A), lambda m: (jnp.maximum(m * per - 1, 0), 0)),
                  pl.BlockSpec((BF16_ROWS, E_A), lambda m: (jnp.minimum((m + 1) * per, nhalo - 1), 0)),
                  pl.BlockSpec((CONV_W, E_A), lambda m: (0, 0)),
                  pl.BlockSpec((1, E_A), lambda m: (0, 0)),
                  pl.BlockSpec((nblk, LANES, 2 * LANES), lambda m: (0, 0, 0)),
                  pl.BlockSpec((nblk, LANES, LANES), lambda m: (0, 0, 0)),
                  pl.BlockSpec((3, E_A, LANES), lambda m: (0, 0, 0)),
                  pl.BlockSpec((1, LANES), lambda m: (0, 0))],
        out_specs=[tile, tile, tile, tile, pl.BlockSpec((tm, LANES), lambda m: (m, 0))],
        out_shape=[jax.ShapeDtypeStruct((t, E_A), BF16)] * 4 + [jax.ShapeDtypeStruct((t, LANES), F32)],
        scratch_shapes=[pltpu.VMEM((E_A // LANES, tm + 2 * SUBLANES, LANES), F32)],
        compiler_params=_params(("arbitrary",)),
        name="mlstm_conv_qkv_gates",
    )(proj, proj, proj, conv_w, conv_b.reshape(1, E_A), wqk, wv, wg, bg)


def _order_mask(dirn, n, rows_are_later):
    a_io = lax.broadcasted_iota(jnp.int32, (n, n), 0)
    b_io = lax.broadcasted_iota(jnp.int32, (n, n), 1)
    if (dirn == 0) == rows_are_later:
        return b_io <= a_io
    return b_io >= a_io


def _exact_cumsum_rows(rows, tri):
    nrow = rows.shape[0]
    p1 = rows.astype(BF16).astype(F32)
    r1 = rows - p1
    p2 = r1.astype(BF16).astype(F32)
    p3 = r1 - p2
    pieces = jnp.concatenate([p1, p2, p3, jnp.zeros_like(p1)], axis=0).astype(BF16)
    cs = jnp.dot(pieces, tri, preferred_element_type=F32)
    return cs[0:nrow] + cs[nrow:2 * nrow] + cs[2 * nrow:3 * nrow]


def _pick_row(blk, idx):
    sub = lax.broadcasted_iota(jnp.int32, (blk.shape[0], 1), 0)
    return jnp.sum(jnp.where(sub == idx, blk, 0.0), axis=0, keepdims=True)


def _rows_to_cols(rows):
    pad = jnp.zeros((LANES - rows.shape[0], rows.shape[1]), F32)
    return jnp.concatenate([rows, pad], axis=0).T


def _pick_col(blk, idx):
    lane = lax.broadcasted_iota(jnp.int32, (1, blk.shape[1]), 1)
    return jnp.sum(jnp.where(lane == idx, blk, 0.0), axis=1, keepdims=True)


def _mlstm_direction(dirn, h, q, k, v, s_qk, kt, gt_ref, c_ref, n_ref, m_ref, use_state, need_update):
    lc = SCAN_CHUNK
    scale = DH_A ** -0.5
    log_scale = -0.5 * math.log(DH_A)
    gates = gt_ref[dirn * 2 * H_A:(dirn + 1) * 2 * H_A, :]
    tri = jnp.where(_order_mask(dirn, lc, False), 1.0, 0.0).astype(BF16)
    csum = _exact_cumsum_rows(gates, tri)
    i_row = _pick_row(gates, h)
    f_row = _pick_row(gates, H_A + h)
    b_row = _pick_row(csum, H_A + h)
    v_row = i_row - b_row
    valid = _order_mask(dirn, lc, True)
    m_prev = m_ref[:, 0:1]
    run_max = jnp.max(jnp.where(valid, v_row, -jnp.inf), axis=1, keepdims=True)
    u_col = -jnp.maximum(m_prev, run_max)
    p = s_qk * jnp.exp(jnp.where(valid, u_col + (v_row + log_scale), -jnp.inf))
    num = jnp.dot(p.astype(BF16), v, preferred_element_type=F32)
    den = jnp.sum(p, axis=1, keepdims=True)
    if use_state:
        w_inter = jnp.exp(m_prev + u_col) * scale
        qc = jnp.dot(q, c_ref[...].astype(BF16), preferred_element_type=F32)
        qn = jnp.sum(q.astype(F32) * n_ref[...], axis=1, keepdims=True)
        num = num + w_inter * qc
        den = den + w_inter * qn
    b_col = _rows_to_cols(b_row)[:, 0:1]
    h_out = num / jnp.maximum(jnp.abs(den), jnp.exp(u_col - b_col))
    if not need_update:
        return h_out, None
    b_last = jnp.sum(f_row, axis=1, keepdims=True)
    logk = b_last + v_row
    m_new = jnp.maximum(b_last + m_prev, jnp.max(logk, axis=1, keepdims=True))
    wk = jnp.exp(logk - m_new)
    c_new = jnp.dot((kt * wk).astype(BF16), v, preferred_element_type=F32)
    wk8 = (jnp.zeros((SUBLANES, lc), F32) + wk).astype(BF16)
    n_new = jnp.dot(wk8, k, preferred_element_type=F32)[0:1, :]
    return h_out, (c_new, n_new, m_new, jnp.exp(b_last + m_prev - m_new))


def _mlstm_kernel(*refs, nc, has_init, emit_state):
    shared = nc == 1
    pos = 4 if shared else 8
    chunk_refs = [refs[0:4], refs[0:4] if shared else refs[4:8]]
    if has_init:
        c0_ref, n0_ref, m0_ref = refs[pos:pos + 3]
        pos += 3
    hs_ref = refs[pos]
    pos += 1
    if emit_state:
        cn_ref, nn_ref, mn_ref = refs[pos:pos + 3]
        pos += 3
    c_scr, n_scr, m_scr = refs[pos:pos + 3]

    lc = SCAN_CHUNK
    s_id, h, c = (pl.program_id(i) for i in range(3))
    use_state = has_init or nc > 1

    @pl.when(c == 0)
    def _():
        for dirn in range(2):
            if has_init:
                c_scr[dirn] = c0_ref[dirn]
                n_scr[dirn] = _pick_row(n0_ref[dirn], h)
                m_scr[dirn] = jnp.zeros(m_scr.shape[1:], F32) + m0_ref[(s_id * 2 + dirn) * H_A + h]
            else:
                c_scr[dirn] = jnp.zeros(c_scr.shape[1:], F32)
                n_scr[dirn] = jnp.zeros(n_scr.shape[1:], F32)
                m_scr[dirn] = jnp.zeros(m_scr.shape[1:], F32)
        if not shared:
            hs_ref[...] = jnp.zeros(hs_ref.shape, F32)

    def chunk_operands(q_ref, k_ref, v_ref, gt_ref):
        q = q_ref[...]
        k = k_ref[...]
        s_qk = lax.dot_general(q, k, (((1,), (1,)), ((), ())), preferred_element_type=F32)
        return q, k, v_ref[...], s_qk, k.astype(F32).T, gt_ref

    ops = [chunk_operands(*chunk_refs[0])]
    ops.append(ops[0] if shared else chunk_operands(*chunk_refs[1]))

    def run(need_update):
        return [_mlstm_direction(dirn, h, *ops[dirn], c_scr.at[dirn], n_scr.at[dirn], m_scr.at[dirn],
                                 use_state, need_update) for dirn in range(2)]

    def write_out(res):
        if shared:
            hs_ref[...] = res[0][0] + res[1][0]
        else:
            rows_f = pl.ds(pl.multiple_of(c * lc, lc), lc)
            rows_b = pl.ds(pl.multiple_of((nc - 1 - c) * lc, lc), lc)
            hs_ref[rows_f, :] = hs_ref[rows_f, :] + res[0][0]
            hs_ref[rows_b, :] = hs_ref[rows_b, :] + res[1][0]

    def with_update():
        res = run(True)
        write_out(res)
        for dirn in range(2):
            c_new, n_new, m_new, decay = res[dirn][1]
            if use_state:
                c_new = c_new + decay * c_scr[dirn]
                n_new = n_new + decay * n_scr[dirn]
            m_new_row = jnp.zeros(m_scr.shape[1:], F32) + m_new
            if nc > 1:
                c_scr[dirn] = c_new
                n_scr[dirn] = n_new
                m_scr[dirn] = m_new_row
            if emit_state:
                @pl.when(c == nc - 1)
                def _():
                    cn_ref[dirn] = c_new
                    nn_ref[dirn] = n_new
                    mn_ref[dirn] = m_new_row

    def without_update():
        write_out(run(False))

    if emit_state:
        with_update()
    else:
        pl.when(c < nc - 1)(with_update)
        pl.when(c == nc - 1)(without_update)


def _mlstm_scan(q, k, v, gt, row0, nseq, seqlen, init=None, emit_state=False):
    lc = SCAN_CHUNK
    nc = seqlen // lc
    blk0 = row0 // lc
    has_init = init is not None

    def blk_f(s, c):
        return blk0 + s * nc + c

    def blk_b(s, c):
        return blk0 + s * nc + nc - 1 - c

    def chunk_specs(blk):
        return [pl.BlockSpec((lc, DH_A), lambda s, h, c: (blk(s, c), h))] * 3 + [
            pl.BlockSpec((4 * H_A, lc), lambda s, h, c: (0, blk(s, c)))]

    in_specs = chunk_specs(blk_f)
    args = [q, k, v, gt]
    if nc > 1:
        in_specs += chunk_specs(blk_b)
        args += [q, k, v, gt]
    if has_init:
        c0, n0, m0 = init
        in_specs += [pl.BlockSpec((None, 2, None, DH_A, DH_A), lambda s, h, c: (s, 0, h, 0, 0)),
                     pl.BlockSpec((None, 2, H_A, DH_A), lambda s, h, c: (s, 0, 0, 0)),
                     pl.BlockSpec(memory_space=pltpu.SMEM)]
        args += [c0, n0, m0.reshape(-1)]
    out_specs = [pl.BlockSpec((seqlen, DH_A), lambda s, h, c: (s, h))]
    out_shape = [jax.ShapeDtypeStruct((nseq * seqlen, E_A), F32)]
    if emit_state:
        out_specs += [pl.BlockSpec((None, 2, None, DH_A, DH_A), lambda s, h, c: (s, 0, h, 0, 0)),
                      pl.BlockSpec((None, 2, None, 1, DH_A), lambda s, h, c: (s, 0, h, 0, 0)),
                      pl.BlockSpec((None, 2, None, 1, LANES), lambda s, h, c: (s, 0, h, 0, 0))]
        out_shape += [jax.ShapeDtypeStruct((nseq, 2, H_A, DH_A, DH_A), F32),
                      jax.ShapeDtypeStruct((nseq, 2, H_A, 1, DH_A), F32),
                      jax.ShapeDtypeStruct((nseq, 2, H_A, 1, LANES), F32)]
    return pl.pallas_call(
        functools.partial(_mlstm_kernel, nc=nc, has_init=has_init, emit_state=emit_state),
        grid=(nseq, H_A, nc),
        in_specs=in_specs,
        out_specs=out_specs,
        out_shape=out_shape,
        scratch_shapes=[pltpu.VMEM((2, DH_A, DH_A), F32), pltpu.VMEM((2, 1, DH_A), F32),
                        pltpu.VMEM((2, 1, LANES), F32)],
        compiler_params=_params(("arbitrary",) * 3),
        name="mlstm_scan_state" if emit_state else "mlstm_scan_init",
    )(*args)


def _ssd_direction(dirn, xt, bb, ct, cbt_raw, dtr_ref, a_ref, bias_ref, h_ref, yt_ref, xw_ref, use_state):
    lc = SCAN_CHUNK
    r_io = lax.broadcasted_iota(jnp.int32, (lc, lc), 0)
    s_io = lax.broadcasted_iota(jnp.int32, (lc, lc), 1)
    valid = (r_io <= s_io) if dirn == 0 else (r_io >= s_io)
    tri = jnp.where(valid, 1.0, 0.0).astype(BF16)

    dt_rows = _softplus(dtr_ref[...] + bias_ref[:, 0:1])
    a_rows = dt_rows * a_ref[:, 0:1]
    a1 = a_rows.astype(BF16).astype(F32)
    r1 = a_rows - a1
    a2 = r1.astype(BF16).astype(F32)
    a3 = r1 - a2
    pieces = jnp.concatenate([a1, a2, a3, jnp.zeros_like(a1)], axis=0).astype(BF16)
    cs = jnp.dot(pieces, tri, preferred_element_type=F32)
    acs_rows = cs[0:HG_B] + cs[HG_B:2 * HG_B] + cs[2 * HG_B:3 * HG_B]
    a_last = jnp.sum(a_rows, axis=1, keepdims=True)
    wr_rows = jnp.exp(a_last - acs_rows) * dt_rows
    acs_cols = jnp.concatenate([acs_rows, jnp.zeros((LANES - HG_B, lc), F32)], axis=0).T

    cbt = jnp.where(valid, cbt_raw, 0.0)
    if use_state:
        inter_t = jnp.dot(h_ref[...].astype(BF16), ct, preferred_element_type=F32)
        e_rows = jnp.exp(acs_rows)
    for hg in range(HG_B):
        sl = slice(hg * P_B, (hg + 1) * P_B)
        row = slice(hg, hg + 1)
        seg_t = acs_rows[row, :] - acs_cols[:, hg:hg + 1]
        wt = cbt * jnp.exp(jnp.minimum(seg_t, 0.0))
        xth = xt[sl, :]
        yth = jnp.dot((xth * dt_rows[row, :]).astype(BF16), wt.astype(BF16), preferred_element_type=F32)
        if use_state:
            yth = yth + inter_t[sl, :] * e_rows[row, :]
        yt_ref[sl, :] = yth
        xw_ref[sl, :] = xth * wr_rows[row, :]
    return a_last


def _ssd_kernel(*refs, nc, has_init, emit_state):
    shared = nc == 1
    pos = 3 if shared else 6
    chunk_refs = [refs[0:3], refs[0:3] if shared else refs[3:6]]
    dtr_refs = refs[pos:pos + 2]
    a_ref, bias_ref = refs[pos + 2:pos + 4]
    pos += 4
    if has_init:
        h0_ref = refs[pos]
        pos += 1
    y_ref = refs[pos]
    pos += 1
    if emit_state:
        sn_ref = refs[pos]
        pos += 1
    h_scr, yt_scr, xw_scr = refs[pos:pos + 3]

    lc = SCAN_CHUNK
    gps = SSD_GROUPS_PER_STEP
    c = pl.program_id(2)
    use_state = has_init or nc > 1
    hp = HG_B * P_B

    @pl.when(c == 0)
    def _():
        if has_init:
            h_scr[...] = h0_ref[...].reshape(2, gps, hp, D_STATE)
        else:
            h_scr[...] = jnp.zeros(h_scr.shape, F32)
        if not shared:
            y_ref[...] = jnp.zeros(y_ref.shape, F32)

    ops_all, a_last_all = [], []
    for gg in range(gps):
        xs = slice(gg * hp, (gg + 1) * hp)
        ns = slice(gg * D_STATE, (gg + 1) * D_STATE)

        def chunk_operands(x_ref, b_ref, c_ref):
            bb = b_ref[:, ns]
            ct = c_ref[:, ns].astype(F32).T.astype(BF16)
            return x_ref[:, xs].astype(F32).T, bb, ct, jnp.dot(bb, ct, preferred_element_type=F32)

        ops = [chunk_operands(*chunk_refs[0])]
        ops.append(ops[0] if shared else chunk_operands(*chunk_refs[1]))
        a_last = [_ssd_direction(dirn, *ops[dirn], dtr_refs[dirn].at[gg], a_ref.at[dirn, gg], bias_ref.at[dirn, gg],
                                 h_scr.at[dirn, gg], yt_scr.at[dirn, gg], xw_scr.at[dirn, gg], use_state)
                  for dirn in range(2)]
        ops_all.append(ops)
        a_last_all.append(a_last)
        if shared:
            y_ref[:, xs] = (yt_scr[0, gg] + yt_scr[1, gg]).T
        else:
            rows_f = pl.ds(pl.multiple_of(c * lc, lc), lc)
            rows_b = pl.ds(pl.multiple_of((nc - 1 - c) * lc, lc), lc)
            y_ref[rows_f, xs] = y_ref[rows_f, xs] + yt_scr[0, gg].T
            y_ref[rows_b, xs] = y_ref[rows_b, xs] + yt_scr[1, gg].T

    def update_state():
        for gg in range(gps):
            for dirn in range(2):
                upd = jnp.dot(xw_scr[dirn, gg].astype(BF16), ops_all[gg][dirn][1], preferred_element_type=F32)
                dec = jnp.exp(a_last_all[gg][dirn])
                for hg in range(HG_B):
                    sl = slice(hg * P_B, (hg + 1) * P_B)
                    new = upd[sl, :]
                    if use_state:
                        new = new + dec[hg:hg + 1, :] * h_scr[dirn, gg, sl, :]
                    h_scr[dirn, gg, sl, :] = new
        if emit_state:
            @pl.when(c == nc - 1)
            def _():
                sn_ref[...] = h_scr[...].reshape(2, gps * HG_B, P_B, D_STATE)

    if emit_state:
        update_state()
    elif nc > 1:
        pl.when(c < nc - 1)(update_state)


def _ssd_scan(xbc, dtr, a_neg, dt_bias, row0, nseq, seqlen, init=None, emit_state=False):
    lc = SCAN_CHUNK
    nc = seqlen // lc
    blk0 = row0 // lc
    has_init = init is not None
    gps = SSD_GROUPS_PER_STEP
    hp = gps * HG_B * P_B
    ns = gps * D_STATE
    b_off = E_B // ns
    c_off = (E_B + GN_B) // ns

    def blk_f(s, c):
        return blk0 + s * nc + c

    def blk_b(s, c):
        return blk0 + s * nc + nc - 1 - c

    def chunk_specs(blk):
        return [pl.BlockSpec((lc, hp), lambda s, g, c: (blk(s, c), g)),
                pl.BlockSpec((lc, ns), lambda s, g, c: (blk(s, c), b_off + g)),
                pl.BlockSpec((lc, ns), lambda s, g, c: (blk(s, c), c_off + g))]

    in_specs = chunk_specs(blk_f)
    args = [xbc, xbc, xbc]
    if nc > 1:
        in_specs += chunk_specs(blk_b)
        args += [xbc, xbc, xbc]
    table_spec = pl.BlockSpec((2, gps, HG_B, LANES), lambda s, g, c: (0, g, 0, 0))
    in_specs += [pl.BlockSpec((None, gps, HG_B, lc), lambda s, g, c: (0, g, 0, blk_f(s, c))),
                 pl.BlockSpec((None, gps, HG_B, lc), lambda s, g, c: (1, g, 0, blk_b(s, c))),
                 table_spec, table_spec]
    args += [dtr, dtr, a_neg, dt_bias]
    state_spec = pl.BlockSpec((None, 2, gps * HG_B, P_B, D_STATE), lambda s, g, c: (s, 0, g, 0, 0))
    if has_init:
        in_specs.append(state_spec)
        args.append(init)
    out_specs = [pl.BlockSpec((seqlen, hp), lambda s, g, c: (s, g))]
    out_shape = [jax.ShapeDtypeStruct((nseq * seqlen, E_B), F32)]
    if emit_state:
        out_specs.append(state_spec)
        out_shape.append(jax.ShapeDtypeStruct((nseq, 2, H_B, P_B, D_STATE), F32))
    return pl.pallas_call(
        functools.partial(_ssd_kernel, nc=nc, has_init=has_init, emit_state=emit_state),
        grid=(nseq, G_B // gps, nc),
        in_specs=in_specs,
        out_specs=out_specs,
        out_shape=out_shape,
        scratch_shapes=[pltpu.VMEM((2, gps, HG_B * P_B, D_STATE), F32), pltpu.VMEM((2, gps, HG_B * P_B, lc), F32),
                        pltpu.VMEM((2, gps, HG_B * P_B, lc), F32)],
        compiler_params=_params(("arbitrary",) * 3),
        name="ssd_scan_state" if emit_state else "ssd_scan_init",
    )(*args)


def _mixer_out_kernel(yp_ref, ys_ref, u_ref, z_ref, xp_ref, xs_ref, mod_ref, nw_ref, sk_ref, w_ref, lnw_ref, lnb_ref,
                      wr_ref, br_ref, o_ref, hm_ref, comb_ref, *, tp, grouped):
    y = _select_token_tile(yp_ref, ys_ref, TM, tp)
    u = u_ref[...].astype(F32)
    gate = _silu(z_ref[...].astype(F32))
    if grouped:
        parts = []
        for hd in range(H_A):
            seg = y[:, hd * DH_A:(hd + 1) * DH_A]
            mu = jnp.mean(seg, axis=-1, keepdims=True)
            sc = seg - mu
            var = jnp.mean(sc * sc, axis=-1, keepdims=True)
            parts.append(sc * lax.rsqrt(var + EPS))
        hn = jnp.concatenate(parts, axis=1) * nw_ref[...]
        t = (hn + sk_ref[...] * u) * gate
    else:
        t0 = (y + sk_ref[...] * u) * gate
        t = t0 * lax.rsqrt(jnp.mean(t0 * t0, axis=-1, keepdims=True) + EPS) * nw_ref[...]
    out = jnp.dot(t.astype(BF16), w_ref[...], preferred_element_type=F32)
    mod = mod_ref[...]
    r = ALPHA * _select_token_tile(xp_ref, xs_ref, TM, tp) + mod[2:3, :] * out
    x_new = _layer_norm_rows(r, lnw_ref[...], lnb_ref[...])
    o_ref[...] = x_new
    hm_ref[...], comb_ref[...] = _route(x_new, mod, wr_ref[...], br_ref[...])


def _mixer_out(yp, ys, u, proj, zblk, x_src, t, mod, nw, sk, w_bf, lnw, lnb, w_r, b_r, tp, ls, grouped, name):
    e = w_bf.shape[0]
    y_specs, y_args = _token_tile_specs((yp, 0, ys, 0), TM, tp, e)
    x_specs, x_args = _token_tile_specs(x_src, TM, tp)
    vec_e = pl.BlockSpec((1, e), lambda m: (0, 0))
    vec_d = pl.BlockSpec((1, D_MODEL), lambda m: (0, 0))
    return pl.pallas_call(
        functools.partial(_mixer_out_kernel, tp=tp, grouped=grouped),
        grid=(t // TM,),
        in_specs=y_specs + [
                  pl.BlockSpec((TM, e), lambda m: (m, 0)),
                  pl.BlockSpec((TM, e), lambda m: (m, zblk))] + x_specs + [
                  pl.BlockSpec((None, MOD_ROWS, D_MODEL), lambda m: (_seq_of_tile(m, TM, tp, ls), 0, 0)),
                  vec_e, vec_e,
                  pl.BlockSpec((e, D_MODEL), lambda m: (0, 0)),
                  vec_d, vec_d,
                  pl.BlockSpec((D_MODEL, LANES), lambda m: (0, 0)),
                  pl.BlockSpec((1, LANES), lambda m: (0, 0))],
        out_specs=[pl.BlockSpec((TM, D_MODEL), lambda m: (m, 0)),
                   pl.BlockSpec((TM, D_MODEL), lambda m: (m, 0)),
                   pl.BlockSpec((TM, LANES), lambda m: (m, 0))],
        out_shape=[jax.ShapeDtypeStruct((t, D_MODEL), F32), jax.ShapeDtypeStruct((t, D_MODEL), BF16),
                   jax.ShapeDtypeStruct((t, LANES), F32)],
        compiler_params=_params(("arbitrary",)),
        name=name,
    )(*y_args, u, proj, *x_args, mod, nw.reshape(1, e), sk.reshape(1, e), w_bf, lnw.reshape(1, D_MODEL),
      lnb.reshape(1, D_MODEL), w_r, b_r)


def _route(x, mod, w_r, b_r):
    hm = (x * (1.0 + mod[4:5, :]) + mod[3:4, :]).astype(BF16)
    logits = jnp.dot(hm, w_r, preferred_element_type=F32) + b_r
    n_used = ROUTER_OFF + N_EXPERTS
    n_rows = -(-n_used // SUBLANES) * SUBLANES
    lt = logits.T[:n_rows, :]
    row = lax.broadcasted_iota(jnp.int32, lt.shape, 0).astype(F32)
    big = float(2 * LANES)
    glog = jnp.where(row < MOE_GROUPS, lt, -jnp.inf)
    ge = jnp.exp(glog - jnp.max(glog, axis=0, keepdims=True))
    prob = ge / jnp.sum(ge, axis=0, keepdims=True)
    gp = jnp.max(prob, axis=0, keepdims=True)
    gi = jnp.min(jnp.where(prob == gp, row, big), axis=0, keepdims=True)
    lo = ROUTER_OFF + gi * MOE_EXPERTS
    ev = jnp.where((row >= lo) & (row < lo + MOE_EXPERTS), lt, -jnp.inf)
    e1 = jnp.max(ev, axis=0, keepdims=True)
    i1 = jnp.min(jnp.where(ev == e1, row, big), axis=0, keepdims=True)
    ev2 = jnp.where(row == i1, -jnp.inf, ev)
    e2 = jnp.max(ev2, axis=0, keepdims=True)
    i2 = jnp.min(jnp.where(ev2 == e2, row, big), axis=0, keepdims=True)
    t2 = jnp.exp(e2 - e1)
    w1 = 1.0 / (1.0 + t2)
    w2 = t2 / (1.0 + t2)
    comb_t = jnp.where(row == i1, w1 * gp, jnp.where(row == i2, w2 * gp, 0.0))
    return hm, _rows_to_cols(comb_t)


def _moe_kernel(hm_ref, comb_ref, wg_ref, wu_ref, wd_ref, x_ref, mod_ref, lnw_ref, lnb_ref, op_ref, os_ref, *, npt):
    hm = hm_ref[...]
    comb = comb_ref[...]
    acc = None
    for g in range(MOE_GROUPS):
        a = _silu(jnp.dot(hm, wg_ref[g], preferred_element_type=F32)) * jnp.dot(
            hm, wu_ref[g], preferred_element_type=F32)
        parts = []
        for e in range(MOE_EXPERTS):
            lane = ROUTER_OFF + g * MOE_EXPERTS + e
            parts.append(a[:, e * D_FF_E:(e + 1) * D_FF_E] * comb[:, lane:lane + 1])
        part = jnp.dot(jnp.concatenate(parts, axis=1).astype(BF16), wd_ref[g], preferred_element_type=F32)
        acc = part if acc is None else acc + part
    mod = mod_ref[...]
    r = ALPHA * x_ref[...] + mod[5:6, :] * acc
    y = _layer_norm_rows(r, lnw_ref[...], lnb_ref[...])
    m = pl.program_id(0)

    @pl.when(m < npt)
    def _():
        op_ref[...] = y

    @pl.when(m >= npt)
    def _():
        os_ref[...] = y


def _moe(hm, comb, wg, wu, wd, x, mod, lnw, lnb, tp, ls):
    t = x.shape[0]
    tm = TM_MOE
    npt = tp // tm
    nst = (t - tp) // tm
    gf = MOE_EXPERTS * D_FF_E
    vec_d = pl.BlockSpec((1, D_MODEL), lambda m: (0, 0))
    return pl.pallas_call(
        functools.partial(_moe_kernel, npt=npt),
        grid=(t // tm,),
        in_specs=[pl.BlockSpec((tm, D_MODEL), lambda m: (m, 0)),
                  pl.BlockSpec((tm, LANES), lambda m: (m, 0)),
                  _resident((MOE_GROUPS, D_MODEL, gf)),
                  _resident((MOE_GROUPS, D_MODEL, gf)),
                  _resident((MOE_GROUPS, gf, D_MODEL)),
                  pl.BlockSpec((tm, D_MODEL), lambda m: (m, 0)),
                  pl.BlockSpec((None, MOD_ROWS, D_MODEL), lambda m: (_seq_of_tile(m, tm, tp, ls), 0, 0)),
                  vec_d, vec_d],
        out_specs=[pl.BlockSpec((tm, D_MODEL), lambda m: (jnp.minimum(m, npt - 1), 0)),
                   pl.BlockSpec((tm, D_MODEL), lambda m: (jnp.clip(m - npt, 0, nst - 1), 0))],
        out_shape=[jax.ShapeDtypeStruct((tp, D_MODEL), F32), jax.ShapeDtypeStruct((t - tp, D_MODEL), F32)],
        compiler_params=_params(("arbitrary",)),
        name="moe_experts",
    )(hm, comb, wg, wu, wd, x, mod, lnw.reshape(1, D_MODEL), lnb.reshape(1, D_MODEL))


def _router_weights(w_rg, b_rg, w_re, b_re):
    w_r = jnp.concatenate([w_rg, jnp.moveaxis(w_re, 0, 1).reshape(D_MODEL, N_EXPERTS)], axis=1)
    w_r = jnp.pad(w_r, ((0, 0), (0, LANES - w_r.shape[1]))).astype(BF16)
    b_r = jnp.pad(jnp.concatenate([b_rg, b_re.reshape(-1)]), (0, LANES - MOE_GROUPS - N_EXPERTS)).reshape(1, LANES)
    return w_r, b_r


def _expert_weights(w_gate, w_up, w_down):
    gf = MOE_EXPERTS * D_FF_E
    wg = w_gate.transpose(0, 2, 1, 3).reshape(MOE_GROUPS, D_MODEL, gf).astype(BF16)
    wu = w_up.transpose(0, 2, 1, 3).reshape(MOE_GROUPS, D_MODEL, gf).astype(BF16)
    wd = w_down.reshape(MOE_GROUPS, gf, D_MODEL).astype(BF16)
    return wg, wu, wd


def _blockdiag_lanes(w):
    per = LANES // QKV_BLOCK
    wb = w.reshape(-1, per, QKV_BLOCK, QKV_BLOCK)
    eye = jnp.eye(per, dtype=w.dtype)
    return jnp.einsum('bncd,nm->bncmd', wb, eye).reshape(-1, LANES, LANES)


def _grid_to_cols(x):
    b, l, dd = x.shape
    return x.reshape(b, l // GRID_W, GRID_W, dd).transpose(0, 2, 1, 3).reshape(b, l, dd)


def _cols_to_grid(x):
    b, l, dd = x.shape
    return x.reshape(b, GRID_W, l // GRID_W, dd).transpose(0, 2, 1, 3).reshape(b, l, dd)


def kernel(x_prompt, x_sample, state_mlstm_C, state_mlstm_n, state_mlstm_m, state_ssd, c, c_ctx, ada_w, ada_b, ln_w, ln_b, a_w_in, a_conv_w, a_conv_b, a_w_q, a_w_k, a_w_v, a_w_gates, a_b_gates, a_norm_w, a_skip, a_w_down, b_w_in, b_conv_w, b_conv_b, b_dt_bias, b_a_log, b_d_skip, b_norm_w, b_w_out, moe_w_rg, moe_b_rg, moe_w_re, moe_b_re, moe_w_gate, moe_w_up, moe_w_down):
    bp, lp, _ = x_prompt.shape
    bs, ls, _ = x_sample.shape
    tp, ts = bp * lp, bs * ls
    assert lp % SCAN_CHUNK == 0 and ls % SCAN_CHUNK == 0 and tp % 512 == 0 and ls % 512 == 0
    assert bs + 1 <= N_MOD_SEQ

    cvec = jnp.concatenate([c_ctx[None, :], c, jnp.zeros((N_MOD_SEQ - 1 - bs, D_MODEL), F32)], axis=0)
    mods = _ada(cvec, ada_w, ada_b).reshape(DEPTH, N_MOD_SEQ, 6, D_MODEL)
    mods = jnp.pad(mods, ((0, 0), (0, 0), (0, MOD_ROWS - 6), (0, 0)))

    t = tp + ts
    x_src = (x_prompt.reshape(tp, D_MODEL), 0, x_sample.reshape(ts, D_MODEL), 0)

    mod = mods[0]
    (proj,) = _modmm(x_src, t, mod, a_w_in[0].astype(BF16), 0, tp, ls, 0, 1, "mlstm_in_proj")
    wqk = jnp.concatenate([_blockdiag_lanes(a_w_q[0]), _blockdiag_lanes(a_w_k[0])], axis=-1).astype(BF16)
    wv = _blockdiag_lanes(a_w_v[0]).astype(BF16)
    ngate = 4 * H_A
    wg = jnp.pad(a_w_gates[0].reshape(3, E_A, ngate), ((0, 0), (0, 0), (0, LANES - ngate))).astype(BF16)
    bg = jnp.pad(a_b_gates[0], (0, LANES - ngate)).reshape(1, LANES)
    xc, q, k, v, g = _conv_qkv(proj, a_conv_w[0], a_conv_b[0], wqk, wv, wg, bg, tp, lp, ls)
    gt = g[:, :ngate].T
    hs_p, c_new, n_new, m_new = _mlstm_scan(q, k, v, gt, 0, bp, lp, emit_state=True)
    (hs_s,) = _mlstm_scan(q, k, v, gt, tp, bs, ls,
                          init=(state_mlstm_C[:, 0], state_mlstm_n[:, 0], state_mlstm_m[:, 0]))
    x, hm, comb = _mixer_out(hs_p, hs_s, xc, proj, 1, x_src, t, mod, a_norm_w[0], a_skip[0],
                             a_w_down[0].astype(BF16), ln_w[0, 0], ln_b[0, 0],
                             *_router_weights(moe_w_rg[0], moe_b_rg[0], moe_w_re[0], moe_b_re[0]),
                             tp, ls, True, "mlstm_out")
    x_p, x_s = _moe(hm, comb, *_expert_weights(moe_w_gate[0], moe_w_up[0], moe_w_down[0]), x, mod,
                    ln_w[0, 1], ln_b[0, 1], tp, ls)

    mod = mods[1]
    x_src = (x_p, 0, _grid_to_cols(x_s.reshape(bs, ls, D_MODEL)).reshape(ts, D_MODEL), 0)
    w_in = jnp.pad(b_w_in[0].astype(BF16), ((0, 0), (0, -b_w_in.shape[2] % LANES)))
    proj, dt = _modmm(x_src, t, mod, w_in, LANES, tp, ls, 0, 1, "ssd_in_proj")
    dt = dt[:, :2 * H_B]
    xbc = _conv_silu(proj, E_B, E_B + 2 * GN_B, b_conv_w[0], b_conv_b[0], tp, lp, ls, "ssd_conv")
    lane_bcast = (2, G_B, HG_B, LANES)
    a_neg = jnp.broadcast_to(-jnp.exp(b_a_log[0]).reshape(2, G_B, HG_B, 1), lane_bcast)
    dt_bias = jnp.broadcast_to(b_dt_bias[0].reshape(2, G_B, HG_B, 1), lane_bcast)
    dtr = dt.T.reshape(2, G_B, HG_B, tp + ts)
    y_p, s_new = _ssd_scan(xbc, dtr, a_neg, dt_bias, 0, bp, lp, emit_state=True)
    (y_s,) = _ssd_scan(xbc, dtr, a_neg, dt_bias, tp, bs, ls, init=state_ssd[:, 0])
    x, hm, comb = _mixer_out(y_p, y_s, xbc, proj, 0, x_src, t, mod, b_norm_w[0], jnp.repeat(b_d_skip[0], P_B),
                             b_w_out[0].astype(BF16), ln_w[1, 0], ln_b[1, 0],
                             *_router_weights(moe_w_rg[1], moe_b_rg[1], moe_w_re[1], moe_b_re[1]),
                             tp, ls, False, "ssd_out")
    x_p, x_s = _moe(hm, comb, *_expert_weights(moe_w_gate[1], moe_w_up[1], moe_w_down[1]), x, mod,
                    ln_w[1, 1], ln_b[1, 1], tp, ls)

    y_prompt = x_p.reshape(bp, lp, D_MODEL)
    y_sample = _cols_to_grid(x_s.reshape(bs, ls, D_MODEL))
    return (y_prompt, y_sample,
            c_new[:, None],
            n_new.reshape(bp, 1, 2, H_A, DH_A),
            m_new[:, :, :, 0, 0][:, None],
            s_new[:, None])
```

```python
import functools
import math

import jax
import jax.numpy as jnp
from jax import lax
from jax.experimental import pallas as pl
from jax.experimental.pallas import tpu as pltpu

F32 = jnp.float32
BF16 = jnp.bfloat16

D_MODEL = 1024
DEPTH = 2
GRID_W = 64
CONV_W = 5
E_A = 2 * D_MODEL
H_A = 4
DH_A = E_A // H_A
QKV_BLOCK = 4
E_B = 2 * D_MODEL
P_B = 64
H_B = E_B // P_B
G_B = 4
HG_B = H_B // G_B
D_STATE = 128
GN_B = G_B * D_STATE
MOE_GROUPS = 4
MOE_EXPERTS = 4
N_EXPERTS = MOE_GROUPS * MOE_EXPERTS
D_FF_E = 256
ALPHA = (2.0 * DEPTH) ** 0.25
EPS = 1e-5

LANES = 128
SUBLANES = 8
BF16_ROWS = 16
TM = 512
TM_CONV = 256
TM_MOE = 512
SCAN_CHUNK = 256
SSD_GROUPS_PER_STEP = 2
MOD_ROWS = 8
N_MOD_SEQ = 16
VMEM_LIMIT = 56 * 1024 * 1024
ROUTER_OFF = MOE_GROUPS


def _params(sem):
    return pltpu.CompilerParams(dimension_semantics=sem, vmem_limit_bytes=VMEM_LIMIT)


def _seq_of_tile(m, tm, tp, ls):
    npt = tp // tm
    return jnp.where(m < npt, 0, 1 + (m - npt) // (ls // tm))


def _silu(x):
    return x * jax.nn.sigmoid(x)


def _softplus(x):
    return jnp.maximum(x, 0.0) + jnp.log1p(jnp.exp(-jnp.abs(x)))


def _layer_norm_rows(r, w, b):
    mu = jnp.mean(r, axis=-1, keepdims=True)
    rc = r - mu
    var = jnp.mean(rc * rc, axis=-1, keepdims=True)
    return rc * lax.rsqrt(var + EPS) * w + b


def _ada_kernel(c_ref, w_ref, b_ref, o_ref):
    s = _silu(c_ref[...])
    o_ref[...] = jnp.dot(s.astype(BF16), w_ref[...].astype(BF16), preferred_element_type=F32) + b_ref[...]


def _ada(cvec, ada_w, ada_b):
    nl = ada_w.shape[0]
    n_out = ada_w.shape[2]
    tn = 1536
    return pl.pallas_call(
        _ada_kernel,
        grid=(nl, n_out // tn),
        in_specs=[pl.BlockSpec((N_MOD_SEQ, D_MODEL), lambda l, n: (0, 0)),
                  pl.BlockSpec((None, D_MODEL, tn), lambda l, n: (l, 0, n)),
                  pl.BlockSpec((None, 1, tn), lambda l, n: (l, 0, n))],
        out_specs=pl.BlockSpec((None, N_MOD_SEQ, tn), lambda l, n: (l, 0, n)),
        out_shape=jax.ShapeDtypeStruct((nl, N_MOD_SEQ, n_out), F32),
        compiler_params=_params(("arbitrary", "arbitrary")),
        name="ada_mod",
    )(cvec, ada_w, ada_b.reshape(nl, 1, n_out))


def _token_tile_specs(src, tm, tp, width=D_MODEL):
    pa, p0, sa, s0 = src
    npt = tp // tm
    nst = (sa.shape[0] - s0) // tm
    return ([pl.BlockSpec((tm, width), lambda m, *_: (p0 // tm + jnp.minimum(m, npt - 1), 0)),
             pl.BlockSpec((tm, width), lambda m, *_: (s0 // tm + jnp.clip(m - npt, 0, nst - 1), 0))], [pa, sa])


def _select_token_tile(p_ref, s_ref, tm, tp):
    return jnp.where(pl.program_id(0) < tp // tm, p_ref[...], s_ref[...])


def _modmm_kernel(xp_ref, xs_ref, mod_ref, w_ref, o_ref, *tail_ref, tm, tp, shift_row, scale_row):
    mod = mod_ref[...]
    x = _select_token_tile(xp_ref, xs_ref, tm, tp)
    h = x * (1.0 + mod[scale_row:scale_row + 1, :]) + mod[shift_row:shift_row + 1, :]
    res = jnp.dot(h.astype(BF16), w_ref[...], preferred_element_type=F32)
    n_main = o_ref.shape[1]
    o_ref[...] = res[:, :n_main].astype(o_ref.dtype)
    if tail_ref:
        tail_ref[0][...] = res[:, n_main:]


def _resident(shape):
    return pl.BlockSpec(shape, lambda *_: (0,) * len(shape), pipeline_mode=pl.Buffered(1))


def _modmm(src, t, mod, w_bf, n_tail, tp, ls, shift_row, scale_row, name):
    n_main = w_bf.shape[1] - n_tail
    tm = TM
    x_specs, x_args = _token_tile_specs(src, tm, tp)
    out_specs = [pl.BlockSpec((tm, n_main), lambda m: (m, 0))]
    out_shape = [jax.ShapeDtypeStruct((t, n_main), BF16)]
    if n_tail:
        out_specs.append(pl.BlockSpec((tm, n_tail), lambda m: (m, 0)))
        out_shape.append(jax.ShapeDtypeStruct((t, n_tail), F32))
    return pl.pallas_call(
        functools.partial(_modmm_kernel, tm=tm, tp=tp, shift_row=shift_row, scale_row=scale_row),
        grid=(t // tm,),
        in_specs=x_specs + [
            pl.BlockSpec((None, MOD_ROWS, D_MODEL), lambda m: (_seq_of_tile(m, tm, tp, ls), 0, 0)),
            _resident((D_MODEL, w_bf.shape[1]))],
        out_specs=out_specs,
        out_shape=out_shape,
        compiler_params=_params(("arbitrary",)),
        name=name,
    )(*x_args, mod, w_bf)


def _conv_kernel(cur_ref, prev_ref, next_ref, w_ref, b_ref, o_ref, ext_ref, *, tm, tp, lp, ls, col0=0):
    m = pl.program_id(0)
    npt = tp // tm
    pos = jnp.where(m < npt, m % (lp // tm), (m - npt) % (ls // tm))
    last_pos = jnp.where(m < npt, lp // tm - 1, ls // tm - 1)
    halo = SUBLANES
    prev = jnp.where(pos == 0, 0.0, prev_ref[...].astype(F32)[BF16_ROWS - halo:, :])
    cur = cur_ref[...].astype(F32)
    nxt = jnp.where(pos == last_pos, 0.0, next_ref[...].astype(F32)[:halo, :])
    first_tap = jnp.minimum(m, 0) + (halo - CONV_W // 2)
    for c in range(ext_ref.shape[0]):
        lanes = slice(c * LANES, (c + 1) * LANES)
        ext_ref[c, 0:halo, :] = prev[:, lanes]
        ext_ref[c, halo:halo + tm, :] = cur[:, lanes]
        ext_ref[c, halo + tm:2 * halo + tm, :] = nxt[:, lanes]
    for c in range(ext_ref.shape[0]):
        lanes = slice(col0 + c * LANES, col0 + (c + 1) * LANES)
        acc = jnp.zeros((tm, LANES), F32) + b_ref[:, lanes]
        for j in range(CONV_W):
            acc = acc + ext_ref[c, pl.ds(first_tap + j, tm), :] * w_ref[j:j + 1, lanes]
        o_ref[:, lanes] = _silu(acc).astype(o_ref.dtype)


def _conv_blocks_kernel(*refs, nblk, tc, tm, tp, lp, ls):
    w_ref, b_ref, o_ref, ext_ref = refs[3 * nblk:]
    slabs = tc // LANES
    for j in range(nblk):
        _conv_kernel(*refs[3 * j:3 * j + 3], w_ref, b_ref, o_ref, ext_ref.at[pl.ds(j * slabs, slabs)],
                     tm=tm, tp=tp, lp=lp, ls=ls, col0=j * tc)


def _conv_silu(a, col0, width, w, b, tp, lp, ls, name):
    t = a.shape[0]
    tm, tc = TM_CONV, 1024
    nblk = width // tc
    cb0 = col0 // tc
    nhalo = t // BF16_ROWS
    per = tm // BF16_ROWS
    in_specs, args = [], []
    for j in range(nblk):
        in_specs += [pl.BlockSpec((tm, tc), lambda m, j=j: (m, cb0 + j)),
                     pl.BlockSpec((BF16_ROWS, tc), lambda m, j=j: (jnp.maximum(m * per - 1, 0), cb0 + j)),
                     pl.BlockSpec((BF16_ROWS, tc), lambda m, j=j: (jnp.minimum((m + 1) * per, nhalo - 1), cb0 + j))]
        args += [a, a, a]
    return pl.pallas_call(
        functools.partial(_conv_blocks_kernel, nblk=nblk, tc=tc, tm=tm, tp=tp, lp=lp, ls=ls),
        grid=(t // tm,),
        in_specs=in_specs + [pl.BlockSpec((CONV_W, width), lambda m: (0, 0)),
                             pl.BlockSpec((1, width), lambda m: (0, 0))],
        out_specs=pl.BlockSpec((tm, width), lambda m: (m, 0)),
        out_shape=jax.ShapeDtypeStruct((t, width), BF16),
        scratch_shapes=[pltpu.VMEM((width // LANES, tm + 2 * SUBLANES, LANES), F32)],
        compiler_params=_params(("arbitrary",)),
        name=name,
    )(*args, w, b.reshape(1, width))


def _qkv_kernel(xc_ref, xu_ref, wqk_ref, wv_ref, wg_ref, bg_ref, q_ref, k_ref, v_ref, g_ref):
    nblk = E_A // LANES
    g = jnp.zeros(g_ref.shape, F32) + bg_ref[...]
    for b in range(nblk):
        sl = slice(b * LANES, (b + 1) * LANES)
        xcb = xc_ref[:, sl]
        xub = xu_ref[:, sl]
        qk = jnp.dot(xcb, wqk_ref[b], preferred_element_type=F32)
        vv = jnp.dot(xub, wv_ref[b], preferred_element_type=F32)
        qb = qk[:, :LANES].astype(BF16)
        kb = qk[:, LANES:].astype(BF16)
        vb = vv.astype(BF16)
        q_ref[:, sl] = qb
        k_ref[:, sl] = kb
        v_ref[:, sl] = vb
        g = g + jnp.dot(qb, wg_ref[0, sl, :], preferred_element_type=F32)
        g = g + jnp.dot(kb, wg_ref[1, sl, :], preferred_element_type=F32)
        g = g + jnp.dot(vb, wg_ref[2, sl, :], preferred_element_type=F32)
    lane = lax.broadcasted_iota(jnp.int32, g.shape, 1)
    is_forget = (lane & (2 * H_A - 1)) >= H_A
    log_sig = jnp.minimum(g, 0.0) - jnp.log1p(jnp.exp(-jnp.abs(g)))
    g_ref[...] = jnp.where(is_forget, log_sig, g)


def _conv_qkv_kernel(cur_ref, prev_ref, next_ref, cw_ref, cb_ref, wqk_ref, wv_ref, wg_ref, bg_ref,
                     xc_ref, q_ref, k_ref, v_ref, g_ref, ext_ref, *, tm, tp, lp, ls):
    _conv_kernel(cur_ref, prev_ref, next_ref, cw_ref, cb_ref, xc_ref, ext_ref, tm=tm, tp=tp, lp=lp, ls=ls)
    _qkv_kernel(xc_ref, cur_ref, wqk_ref, wv_ref, wg_ref, bg_ref, q_ref, k_ref, v_ref, g_ref)


def _conv_qkv(proj, conv_w, conv_b, wqk, wv, wg, bg, tp, lp, ls):
    t = proj.shape[0]
    tm = TM_CONV
    nblk = E_A // LANES
    nhalo = t // BF16_ROWS
    per = tm // BF16_ROWS
    tile = pl.BlockSpec((tm, E_A), lambda m: (m, 0))
    return pl.pallas_call(
        functools.partial(_conv_qkv_kernel, tm=tm, tp=tp, lp=lp, ls=ls),
        grid=(t // tm,),
        in_specs=[tile,
                  pl.BlockSpec((BF16_ROWS, E_A), lambda m: (jnp.maximum(m * per - 1, 0), 0)),
                  pl.BlockSpec((BF16_ROWS, E_A), lambda m: (jnp.minimum((m + 1) * per, nhalo - 1), 0)),
                  pl.BlockSpec((CONV_W, E_A), lambda m: (0, 0)),
                  pl.BlockSpec((1, E_A), lambda m: (0, 0)),
                  pl.BlockSpec((nblk, LANES, 2 * LANES), lambda m: (0, 0, 0)),
                  pl.BlockSpec((nblk, LANES, LANES), lambda m: (0, 0, 0)),
                  pl.BlockSpec((3, E_A, LANES), lambda m: (0, 0, 0)),
                  pl.BlockSpec((1, LANES), lambda m: (0, 0))],
        out_specs=[tile, tile, tile, tile, pl.BlockSpec((tm, LANES), lambda m: (m, 0))],
        out_shape=[jax.ShapeDtypeStruct((t, E_A), BF16)] * 4 + [jax.ShapeDtypeStruct((t, LANES), F32)],
        scratch_shapes=[pltpu.VMEM((E_A // LANES, tm + 2 * SUBLANES, LANES), F32)],
        compiler_params=_params(("arbitrary",)),
        name="mlstm_conv_qkv_gates",
    )(proj, proj, proj, conv_w, conv_b.reshape(1, E_A), wqk, wv, wg, bg)


def _order_mask(dirn, n, rows_are_later):
    a_io = lax.broadcasted_iota(jnp.int32, (n, n), 0)
    b_io = lax.broadcasted_iota(jnp.int32, (n, n), 1)
    if (dirn == 0) == rows_are_later:
        return b_io <= a_io
    return b_io >= a_io


def _exact_cumsum_rows(rows, tri):
    nrow = rows.shape[0]
    p1 = rows.astype(BF16).astype(F32)
    r1 = rows - p1
    p2 = r1.astype(BF16).astype(F32)
    p3 = r1 - p2
    pieces = jnp.concatenate([p1, p2, p3, jnp.zeros_like(p1)], axis=0).astype(BF16)
    cs = jnp.dot(pieces, tri, preferred_element_type=F32)
    return cs[0:nrow] + cs[nrow:2 * nrow] + cs[2 * nrow:3 * nrow]


def _pick_row(blk, idx):
    sub = lax.broadcasted_iota(jnp.int32, (blk.shape[0], 1), 0)
    return jnp.sum(jnp.where(sub == idx, blk, 0.0), axis=0, keepdims=True)


def _rows_to_cols(rows):
    pad = jnp.zeros((LANES - rows.shape[0], rows.shape[1]), F32)
    return jnp.concatenate([rows, pad], axis=0).T


def _pick_col(blk, idx):
    lane = lax.broadcasted_iota(jnp.int32, (1, blk.shape[1]), 1)
    return jnp.sum(jnp.where(lane == idx, blk, 0.0), axis=1, keepdims=True)


def _mlstm_direction(dirn, h, q, k, v, s_qk, kt, gt_ref, c_ref, n_ref, m_ref, use_state, need_update):
    lc = SCAN_CHUNK
    scale = DH_A ** -0.5
    log_scale = -0.5 * math.log(DH_A)
    gates = gt_ref[dirn * 2 * H_A:(dirn + 1) * 2 * H_A, :]
    tri = jnp.where(_order_mask(dirn, lc, False), 1.0, 0.0).astype(BF16)
    csum = _exact_cumsum_rows(gates, tri)
    i_row = _pick_row(gates, h)
    f_row = _pick_row(gates, H_A + h)
    b_row = _pick_row(csum, H_A + h)
    v_row = i_row - b_row
    valid = _order_mask(dirn, lc, True)
    m_prev = m_ref[:, 0:1]
    run_max = jnp.max(jnp.where(valid, v_row, -jnp.inf), axis=1, keepdims=True)
    u_col = -jnp.maximum(m_prev, run_max)
    p = s_qk * jnp.exp(jnp.where(valid, u_col + (v_row + log_scale), -jnp.inf))
    num = jnp.dot(p.astype(BF16), v, preferred_element_type=F32)
    den = jnp.sum(p, axis=1, keepdims=True)
    if use_state:
        w_inter = jnp.exp(m_prev + u_col) * scale
        qc = jnp.dot(q, c_ref[...].astype(BF16), preferred_element_type=F32)
        qn = jnp.sum(q.astype(F32) * n_ref[...], axis=1, keepdims=True)
        num = num + w_inter * qc
        den = den + w_inter * qn
    b_col = _rows_to_cols(b_row)[:, 0:1]
    h_out = num / jnp.maximum(jnp.abs(den), jnp.exp(u_col - b_col))
    if not need_update:
        return h_out, None
    b_last = jnp.sum(f_row, axis=1, keepdims=True)
    logk = b_last + v_row
    m_new = jnp.maximum(b_last + m_prev, jnp.max(logk, axis=1, keepdims=True))
    wk = jnp.exp(logk - m_new)
    c_new = jnp.dot((kt * wk).astype(BF16), v, preferred_element_type=F32)
    wk8 = (jnp.zeros((SUBLANES, lc), F32) + wk).astype(BF16)
    n_new = jnp.dot(wk8, k, preferred_element_type=F32)[0:1, :]
    return h_out, (c_new, n_new, m_new, jnp.exp(b_last + m_prev - m_new))


def _mlstm_kernel(*refs, nc, has_init, emit_state):
    shared = nc == 1
    pos = 4 if shared else 8
    chunk_refs = [refs[0:4], refs[0:4] if shared else refs[4:8]]
    if has_init:
        c0_ref, n0_ref, m0_ref = refs[pos:pos + 3]
        pos += 3
    hs_ref = refs[pos]
    pos += 1
    if emit_state:
        cn_ref, nn_ref, mn_ref = refs[pos:pos + 3]
        pos += 3
    c_scr, n_scr, m_scr = refs[pos:pos + 3]

    lc = SCAN_CHUNK
    s_id, h, c = (pl.program_id(i) for i in range(3))
    use_state = has_init or nc > 1

    @pl.when(c == 0)
    def _():
        for dirn in range(2):
            if has_init:
                c_scr[dirn] = c0_ref[dirn]
                n_scr[dirn] = _pick_row(n0_ref[dirn], h)
                m_scr[dirn] = jnp.zeros(m_scr.shape[1:], F32) + m0_ref[(s_id * 2 + dirn) * H_A + h]
            else:
                c_scr[dirn] = jnp.zeros(c_scr.shape[1:], F32)
                n_scr[dirn] = jnp.zeros(n_scr.shape[1:], F32)
                m_scr[dirn] = jnp.zeros(m_scr.shape[1:], F32)

    def chunk_operands(q_ref, k_ref, v_ref, gt_ref):
        q = q_ref[...]
        k = k_ref[...]
        s_qk = lax.dot_general(q, k, (((1,), (1,)), ((), ())), preferred_element_type=F32)
        return q, k, v_ref[...], s_qk, k.astype(F32).T, gt_ref

    ops = [chunk_operands(*chunk_refs[0])]
    ops.append(ops[0] if shared else chunk_operands(*chunk_refs[1]))

    def run(need_update):
        return [_mlstm_direction(dirn, h, *ops[dirn], c_scr.at[dirn], n_scr.at[dirn], m_scr.at[dirn],
                                 use_state, need_update) for dirn in range(2)]

    def write_out(res):
        if shared:
            hs_ref[...] = res[0][0] + res[1][0]
        else:
            rows_f = pl.ds(pl.multiple_of(c * lc, lc), lc)
            rows_b = pl.ds(pl.multiple_of((nc - 1 - c) * lc, lc), lc)

            @pl.when(c < nc // 2)
            def _():
                hs_ref[rows_f, :] = res[0][0]
                hs_ref[rows_b, :] = res[1][0]

            @pl.when(c >= nc // 2)
            def _():
                hs_ref[rows_f, :] = hs_ref[rows_f, :] + res[0][0]
                hs_ref[rows_b, :] = hs_ref[rows_b, :] + res[1][0]

    def with_update():
        res = run(True)
        write_out(res)
        for dirn in range(2):
            c_new, n_new, m_new, decay = res[dirn][1]
            if use_state:
                c_new = c_new + decay * c_scr[dirn]
                n_new = n_new + decay * n_scr[dirn]
            m_new_row = jnp.zeros(m_scr.shape[1:], F32) + m_new
            if nc > 1:
                c_scr[dirn] = c_new
                n_scr[dirn] = n_new
                m_scr[dirn] = m_new_row
            if emit_state:
                @pl.when(c == nc - 1)
                def _():
                    cn_ref[dirn] = c_new
                    nn_ref[dirn] = n_new
                    mn_ref[dirn] = m_new_row

    def without_update():
        write_out(run(False))

    if emit_state:
        with_update()
    else:
        pl.when(c < nc - 1)(with_update)
        pl.when(c == nc - 1)(without_update)


def _mlstm_scan(q, k, v, gt, row0, nseq, seqlen, init=None, emit_state=False):
    lc = SCAN_CHUNK
    nc = seqlen // lc
    blk0 = row0 // lc
    has_init = init is not None

    def blk_f(s, c):
        return blk0 + s * nc + c

    def blk_b(s, c):
        return blk0 + s * nc + nc - 1 - c

    def chunk_specs(blk):
        return [pl.BlockSpec((lc, DH_A), lambda s, h, c: (blk(s, c), h))] * 3 + [
            pl.BlockSpec((4 * H_A, lc), lambda s, h, c: (0, blk(s, c)))]

    in_specs = chunk_specs(blk_f)
    args = [q, k, v, gt]
    if nc > 1:
        in_specs += chunk_specs(blk_b)
        args += [q, k, v, gt]
    if has_init:
        c0, n0, m0 = init
        in_specs += [pl.BlockSpec((None, 2, None, DH_A, DH_A), lambda s, h, c: (s, 0, h, 0, 0)),
                     pl.BlockSpec((None, 2, H_A, DH_A), lambda s, h, c: (s, 0, 0, 0)),
                     pl.BlockSpec(memory_space=pltpu.SMEM)]
        args += [c0, n0, m0.reshape(-1)]
    out_specs = [pl.BlockSpec((seqlen, DH_A), lambda s, h, c: (s, h))]
    out_shape = [jax.ShapeDtypeStruct((nseq * seqlen, E_A), F32)]
    if emit_state:
        out_specs += [pl.BlockSpec((None, 2, None, DH_A, DH_A), lambda s, h, c: (s, 0, h, 0, 0)),
                      pl.BlockSpec((None, 2, None, 1, DH_A), lambda s, h, c: (s, 0, h, 0, 0)),
                      pl.BlockSpec((None, 2, None, 1, LANES), lambda s, h, c: (s, 0, h, 0, 0))]
        out_shape += [jax.ShapeDtypeStruct((nseq, 2, H_A, DH_A, DH_A), F32),
                      jax.ShapeDtypeStruct((nseq, 2, H_A, 1, DH_A), F32),
                      jax.ShapeDtypeStruct((nseq, 2, H_A, 1, LANES), F32)]
    return pl.pallas_call(
        functools.partial(_mlstm_kernel, nc=nc, has_init=has_init, emit_state=emit_state),
        grid=(nseq, H_A, nc),
        in_specs=in_specs,
        out_specs=out_specs,
        out_shape=out_shape,
        scratch_shapes=[pltpu.VMEM((2, DH_A, DH_A), F32), pltpu.VMEM((2, 1, DH_A), F32),
                        pltpu.VMEM((2, 1, LANES), F32)],
        compiler_params=_params(("arbitrary",) * 3),
        name="mlstm_scan_state" if emit_state else "mlstm_scan_init",
    )(*args)


def _ssd_direction(dirn, xt, bb, ct, cbt_raw, dtr_ref, a_ref, bias_ref, h_ref, yt_ref, xw_ref, use_state):
    lc = SCAN_CHUNK
    r_io = lax.broadcasted_iota(jnp.int32, (lc, lc), 0)
    s_io = lax.broadcasted_iota(jnp.int32, (lc, lc), 1)
    valid = (r_io <= s_io) if dirn == 0 else (r_io >= s_io)
    tri = jnp.where(valid, 1.0, 0.0).astype(BF16)

    dt_rows = _softplus(dtr_ref[...] + bias_ref[:, 0:1])
    a_rows = dt_rows * a_ref[:, 0:1]
    a1 = a_rows.astype(BF16).astype(F32)
    r1 = a_rows - a1
    a2 = r1.astype(BF16).astype(F32)
    a3 = r1 - a2
    pieces = jnp.concatenate([a1, a2, a3, jnp.zeros_like(a1)], axis=0).astype(BF16)
    cs = jnp.dot(pieces, tri, preferred_element_type=F32)
    acs_rows = cs[0:HG_B] + cs[HG_B:2 * HG_B] + cs[2 * HG_B:3 * HG_B]
    a_last = jnp.sum(a_rows, axis=1, keepdims=True)
    wr_rows = jnp.exp(a_last - acs_rows) * dt_rows
    acs_cols = jnp.concatenate([acs_rows, jnp.zeros((LANES - HG_B, lc), F32)], axis=0).T

    cbt = jnp.where(valid, cbt_raw, 0.0)
    if use_state:
        inter_t = jnp.dot(h_ref[...].astype(BF16), ct, preferred_element_type=F32)
        e_rows = jnp.exp(acs_rows)
    for hg in range(HG_B):
        sl = slice(hg * P_B, (hg + 1) * P_B)
        row = slice(hg, hg + 1)
        seg_t = acs_rows[row, :] - acs_cols[:, hg:hg + 1]
        wt = cbt * jnp.exp(jnp.minimum(seg_t, 0.0))
        xth = xt[sl, :]
        yth = jnp.dot((xth * dt_rows[row, :]).astype(BF16), wt.astype(BF16), preferred_element_type=F32)
        if use_state:
            yth = yth + inter_t[sl, :] * e_rows[row, :]
        yt_ref[sl, :] = yth
        xw_ref[sl, :] = xth * wr_rows[row, :]
    return a_last


def _ssd_kernel(*refs, nc, has_init, emit_state):
    shared = nc == 1
    pos = 3 if shared else 6
    chunk_refs = [refs[0:3], refs[0:3] if shared else refs[3:6]]
    dtr_refs = refs[pos:pos + 2]
    a_ref, bias_ref = refs[pos + 2:pos + 4]
    pos += 4
    if has_init:
        h0_ref = refs[pos]
        pos += 1
    y_ref = refs[pos]
    pos += 1
    if emit_state:
        sn_ref = refs[pos]
        pos += 1
    h_scr, yt_scr, xw_scr = refs[pos:pos + 3]

    lc = SCAN_CHUNK
    gps = SSD_GROUPS_PER_STEP
    c = pl.program_id(2)
    use_state = has_init or nc > 1
    hp = HG_B * P_B

    @pl.when(c == 0)
    def _():
        if has_init:
            h_scr[...] = h0_ref[...].reshape(2, gps, hp, D_STATE)
        else:
            h_scr[...] = jnp.zeros(h_scr.shape, F32)

    ops_all, a_last_all = [], []
    for gg in range(gps):
        xs = slice(gg * hp, (gg + 1) * hp)
        ns = slice(gg * D_STATE, (gg + 1) * D_STATE)

        def chunk_operands(x_ref, b_ref, c_ref):
            bb = b_ref[:, ns]
            ct = c_ref[:, ns].astype(F32).T.astype(BF16)
            return x_ref[:, xs].astype(F32).T, bb, ct, jnp.dot(bb, ct, preferred_element_type=F32)

        ops = [chunk_operands(*chunk_refs[0])]
        ops.append(ops[0] if shared else chunk_operands(*chunk_refs[1]))
        a_last = [_ssd_direction(dirn, *ops[dirn], dtr_refs[dirn].at[gg], a_ref.at[dirn, gg], bias_ref.at[dirn, gg],
                                 h_scr.at[dirn, gg], yt_scr.at[dirn, gg], xw_scr.at[dirn, gg], use_state)
                  for dirn in range(2)]
        ops_all.append(ops)
        a_last_all.append(a_last)
        if shared:
            y_ref[:, xs] = (yt_scr[0, gg] + yt_scr[1, gg]).T
        else:
            rows_f = pl.ds(pl.multiple_of(c * lc, lc), lc)
            rows_b = pl.ds(pl.multiple_of((nc - 1 - c) * lc, lc), lc)

            @pl.when(c < nc // 2)
            def _(gg=gg, xs=xs):
                y_ref[rows_f, xs] = yt_scr[0, gg].T
                y_ref[rows_b, xs] = yt_scr[1, gg].T

            @pl.when(c >= nc // 2)
            def _(gg=gg, xs=xs):
                y_ref[rows_f, xs] = y_ref[rows_f, xs] + yt_scr[0, gg].T
                y_ref[rows_b, xs] = y_ref[rows_b, xs] + yt_scr[1, gg].T

    def update_state():
        for gg in range(gps):
            for dirn in range(2):
                upd = jnp.dot(xw_scr[dirn, gg].astype(BF16), ops_all[gg][dirn][1], preferred_element_type=F32)
                dec = jnp.exp(a_last_all[gg][dirn])
                for hg in range(HG_B):
                    sl = slice(hg * P_B, (hg + 1) * P_B)
                    new = upd[sl, :]
                    if use_state:
                        new = new + dec[hg:hg + 1, :] * h_scr[dirn, gg, sl, :]
                    h_scr[dirn, gg, sl, :] = new
        if emit_state:
            @pl.when(c == nc - 1)
            def _():
                sn_ref[...] = h_scr[...].reshape(2, gps * HG_B, P_B, D_STATE)

    if emit_state:
        update_state()
    elif nc > 1:
        pl.when(c < nc - 1)(update_state)


def _ssd_scan(xbc, dtr, a_neg, dt_bias, row0, nseq, seqlen, init=None, emit_state=False):
    lc = SCAN_CHUNK
    nc = seqlen // lc
    blk0 = row0 // lc
    has_init = init is not None
    gps = SSD_GROUPS_PER_STEP
    hp = gps * HG_B * P_B
    ns = gps * D_STATE
    b_off = E_B // ns
    c_off = (E_B + GN_B) // ns

    def blk_f(s, c):
        return blk0 + s * nc + c

    def blk_b(s, c):
        return blk0 + s * nc + nc - 1 - c

    def chunk_specs(blk):
        return [pl.BlockSpec((lc, hp), lambda s, g, c: (blk(s, c), g)),
                pl.BlockSpec((lc, ns), lambda s, g, c: (blk(s, c), b_off + g)),
                pl.BlockSpec((lc, ns), lambda s, g, c: (blk(s, c), c_off + g))]

    in_specs = chunk_specs(blk_f)
    args = [xbc, xbc, xbc]
    if nc > 1:
        in_specs += chunk_specs(blk_b)
        args += [xbc, xbc, xbc]
    table_spec = pl.BlockSpec((2, gps, HG_B, LANES), lambda s, g, c: (0, g, 0, 0))
    in_specs += [pl.BlockSpec((None, gps, HG_B, lc), lambda s, g, c: (0, g, 0, blk_f(s, c))),
                 pl.BlockSpec((None, gps, HG_B, lc), lambda s, g, c: (1, g, 0, blk_b(s, c))),
                 table_spec, table_spec]
    args += [dtr, dtr, a_neg, dt_bias]
    state_spec = pl.BlockSpec((None, 2, gps * HG_B, P_B, D_STATE), lambda s, g, c: (s, 0, g, 0, 0))
    if has_init:
        in_specs.append(state_spec)
        args.append(init)
    out_specs = [pl.BlockSpec((seqlen, hp), lambda s, g, c: (s, g))]
    out_shape = [jax.ShapeDtypeStruct((nseq * seqlen, E_B), F32)]
    if emit_state:
        out_specs.append(state_spec)
        out_shape.append(jax.ShapeDtypeStruct((nseq, 2, H_B, P_B, D_STATE), F32))
    return pl.pallas_call(
        functools.partial(_ssd_kernel, nc=nc, has_init=has_init, emit_state=emit_state),
        grid=(nseq, G_B // gps, nc),
        in_specs=in_specs,
        out_specs=out_specs,
        out_shape=out_shape,
        scratch_shapes=[pltpu.VMEM((2, gps, HG_B * P_B, D_STATE), F32), pltpu.VMEM((2, gps, HG_B * P_B, lc), F32),
                        pltpu.VMEM((2, gps, HG_B * P_B, lc), F32)],
        compiler_params=_params(("arbitrary",) * 3),
        name="ssd_scan_state" if emit_state else "ssd_scan_init",
    )(*args)


def _mixer_out_kernel(yp_ref, ys_ref, u_ref, z_ref, xp_ref, xs_ref, mod_ref, nw_ref, sk_ref, w_ref, lnw_ref, lnb_ref,
                      wr_ref, br_ref, o_ref, hm_ref, comb_ref, *, tp, grouped):
    y = _select_token_tile(yp_ref, ys_ref, TM, tp)
    u = u_ref[...].astype(F32)
    gate = _silu(z_ref[...].astype(F32))
    if grouped:
        parts = []
        for hd in range(H_A):
            seg = y[:, hd * DH_A:(hd + 1) * DH_A]
            mu = jnp.mean(seg, axis=-1, keepdims=True)
            sc = seg - mu
            var = jnp.mean(sc * sc, axis=-1, keepdims=True)
            parts.append(sc * lax.rsqrt(var + EPS))
        hn = jnp.concatenate(parts, axis=1) * nw_ref[...]
        t = (hn + sk_ref[...] * u) * gate
    else:
        t0 = (y + sk_ref[...] * u) * gate
        t = t0 * lax.rsqrt(jnp.mean(t0 * t0, axis=-1, keepdims=True) + EPS) * nw_ref[...]
    out = jnp.dot(t.astype(BF16), w_ref[...], preferred_element_type=F32)
    mod = mod_ref[...]
    r = ALPHA * _select_token_tile(xp_ref, xs_ref, TM, tp) + mod[2:3, :] * out
    x_new = _layer_norm_rows(r, lnw_ref[...], lnb_ref[...])
    o_ref[...] = x_new
    hm_ref[...], comb_ref[...] = _route(x_new, mod, wr_ref[...], br_ref[...])


def _mixer_out(yp, ys, u, proj, zblk, x_src, t, mod, nw, sk, w_bf, lnw, lnb, w_r, b_r, tp, ls, grouped, name):
    e = w_bf.shape[0]
    y_specs, y_args = _token_tile_specs((yp, 0, ys, 0), TM, tp, e)
    x_specs, x_args = _token_tile_specs(x_src, TM, tp)
    vec_e = pl.BlockSpec((1, e), lambda m: (0, 0))
    vec_d = pl.BlockSpec((1, D_MODEL), lambda m: (0, 0))
    return pl.pallas_call(
        functools.partial(_mixer_out_kernel, tp=tp, grouped=grouped),
        grid=(t // TM,),
        in_specs=y_specs + [
                  pl.BlockSpec((TM, e), lambda m: (m, 0)),
                  pl.BlockSpec((TM, e), lambda m: (m, zblk))] + x_specs + [
                  pl.BlockSpec((None, MOD_ROWS, D_MODEL), lambda m: (_seq_of_tile(m, TM, tp, ls), 0, 0)),
                  vec_e, vec_e,
                  pl.BlockSpec((e, D_MODEL), lambda m: (0, 0)),
                  vec_d, vec_d,
                  pl.BlockSpec((D_MODEL, LANES), lambda m: (0, 0)),
                  pl.BlockSpec((1, LANES), lambda m: (0, 0))],
        out_specs=[pl.BlockSpec((TM, D_MODEL), lambda m: (m, 0)),
                   pl.BlockSpec((TM, D_MODEL), lambda m: (m, 0)),
                   pl.BlockSpec((TM, LANES), lambda m: (m, 0))],
        out_shape=[jax.ShapeDtypeStruct((t, D_MODEL), F32), jax.ShapeDtypeStruct((t, D_MODEL), BF16),
                   jax.ShapeDtypeStruct((t, LANES), F32)],
        compiler_params=_params(("arbitrary",)),
        name=name,
    )(*y_args, u, proj, *x_args, mod, nw.reshape(1, e), sk.reshape(1, e), w_bf, lnw.reshape(1, D_MODEL),
      lnb.reshape(1, D_MODEL), w_r, b_r)


def _route(x, mod, w_r, b_r):
    hm = (x * (1.0 + mod[4:5, :]) + mod[3:4, :]).astype(BF16)
    logits = jnp.dot(hm, w_r, preferred_element_type=F32) + b_r
    n_used = ROUTER_OFF + N_EXPERTS
    n_rows = -(-n_used // SUBLANES) * SUBLANES
    lt = logits.T[:n_rows, :]
    row = lax.broadcasted_iota(jnp.int32, lt.shape, 0).astype(F32)
    big = float(2 * LANES)
    glog = jnp.where(row < MOE_GROUPS, lt, -jnp.inf)
    ge = jnp.exp(glog - jnp.max(glog, axis=0, keepdims=True))
    prob = ge / jnp.sum(ge, axis=0, keepdims=True)
    gp = jnp.max(prob, axis=0, keepdims=True)
    gi = jnp.min(jnp.where(prob == gp, row, big), axis=0, keepdims=True)
    lo = ROUTER_OFF + gi * MOE_EXPERTS
    ev = jnp.where((row >= lo) & (row < lo + MOE_EXPERTS), lt, -jnp.inf)
    e1 = jnp.max(ev, axis=0, keepdims=True)
    i1 = jnp.min(jnp.where(ev == e1, row, big), axis=0, keepdims=True)
    ev2 = jnp.where(row == i1, -jnp.inf, ev)
    e2 = jnp.max(ev2, axis=0, keepdims=True)
    i2 = jnp.min(jnp.where(ev2 == e2, row, big), axis=0, keepdims=True)
    t2 = jnp.exp(e2 - e1)
    w1 = 1.0 / (1.0 + t2)
    w2 = t2 / (1.0 + t2)
    comb_t = jnp.where(row == i1, w1 * gp, jnp.where(row == i2, w2 * gp, 0.0))
    return hm, _rows_to_cols(comb_t)


def _moe_kernel(hm_ref, comb_ref, wg_ref, wu_ref, wd_ref, x_ref, mod_ref, lnw_ref, lnb_ref, op_ref, os_ref, *, npt):
    hm = hm_ref[...]
    comb = comb_ref[...]
    acc = None
    for g in range(MOE_GROUPS):
        a = _silu(jnp.dot(hm, wg_ref[g], preferred_element_type=F32)) * jnp.dot(
            hm, wu_ref[g], preferred_element_type=F32)
        parts = []
        for e in range(MOE_EXPERTS):
            lane = ROUTER_OFF + g * MOE_EXPERTS + e
            parts.append(a[:, e * D_FF_E:(e + 1) * D_FF_E] * comb[:, lane:lane + 1])
        part = jnp.dot(jnp.concatenate(parts, axis=1).astype(BF16), wd_ref[g], preferred_element_type=F32)
        acc = part if acc is None else acc + part
    mod = mod_ref[...]
    r = ALPHA * x_ref[...] + mod[5:6, :] * acc
    y = _layer_norm_rows(r, lnw_ref[...], lnb_ref[...])
    m = pl.program_id(0)

    @pl.when(m < npt)
    def _():
        op_ref[...] = y

    @pl.when(m >= npt)
    def _():
        os_ref[...] = y


def _moe(hm, comb, wg, wu, wd, x, mod, lnw, lnb, tp, ls):
    t = x.shape[0]
    tm = TM_MOE
    npt = tp // tm
    nst = (t - tp) // tm
    gf = MOE_EXPERTS * D_FF_E
    vec_d = pl.BlockSpec((1, D_MODEL), lambda m: (0, 0))
    return pl.pallas_call(
        functools.partial(_moe_kernel, npt=npt),
        grid=(t // tm,),
        in_specs=[pl.BlockSpec((tm, D_MODEL), lambda m: (m, 0)),
                  pl.BlockSpec((tm, LANES), lambda m: (m, 0)),
                  _resident((MOE_GROUPS, D_MODEL, gf)),
                  _resident((MOE_GROUPS, D_MODEL, gf)),
                  _resident((MOE_GROUPS, gf, D_MODEL)),
                  pl.BlockSpec((tm, D_MODEL), lambda m: (m, 0)),
                  pl.BlockSpec((None, MOD_ROWS, D_MODEL), lambda m: (_seq_of_tile(m, tm, tp, ls), 0, 0)),
                  vec_d, vec_d],
        out_specs=[pl.BlockSpec((tm, D_MODEL), lambda m: (jnp.minimum(m, npt - 1), 0)),
                   pl.BlockSpec((tm, D_MODEL), lambda m: (jnp.clip(m - npt, 0, nst - 1), 0))],
        out_shape=[jax.ShapeDtypeStruct((tp, D_MODEL), F32), jax.ShapeDtypeStruct((t - tp, D_MODEL), F32)],
        compiler_params=_params(("arbitrary",)),
        name="moe_experts",
    )(hm, comb, wg, wu, wd, x, mod, lnw.reshape(1, D_MODEL), lnb.reshape(1, D_MODEL))


def _router_weights(w_rg, b_rg, w_re, b_re):
    w_r = jnp.concatenate([w_rg, jnp.moveaxis(w_re, 0, 1).reshape(D_MODEL, N_EXPERTS)], axis=1)
    w_r = jnp.pad(w_r, ((0, 0), (0, LANES - w_r.shape[1]))).astype(BF16)
    b_r = jnp.pad(jnp.concatenate([b_rg, b_re.reshape(-1)]), (0, LANES - MOE_GROUPS - N_EXPERTS)).reshape(1, LANES)
    return w_r, b_r


def _expert_weights(w_gate, w_up, w_down):
    gf = MOE_EXPERTS * D_FF_E
    wg = w_gate.transpose(0, 2, 1, 3).reshape(MOE_GROUPS, D_MODEL, gf).astype(BF16)
    wu = w_up.transpose(0, 2, 1, 3).reshape(MOE_GROUPS, D_MODEL, gf).astype(BF16)
    wd = w_down.reshape(MOE_GROUPS, gf, D_MODEL).astype(BF16)
    return wg, wu, wd


def _blockdiag_lanes(w):
    per = LANES // QKV_BLOCK
    wb = w.reshape(-1, per, QKV_BLOCK, QKV_BLOCK)
    eye = jnp.eye(per, dtype=w.dtype)
    return jnp.einsum('bncd,nm->bncmd', wb, eye).reshape(-1, LANES, LANES)


def _grid_to_cols(x):
    b, l, dd = x.shape
    return x.reshape(b, l // GRID_W, GRID_W, dd).transpose(0, 2, 1, 3).reshape(b, l, dd)


def _cols_to_grid(x):
    b, l, dd = x.shape
    return x.reshape(b, GRID_W, l // GRID_W, dd).transpose(0, 2, 1, 3).reshape(b, l, dd)


def kernel(x_prompt, x_sample, state_mlstm_C, state_mlstm_n, state_mlstm_m, state_ssd, c, c_ctx, ada_w, ada_b, ln_w, ln_b, a_w_in, a_conv_w, a_conv_b, a_w_q, a_w_k, a_w_v, a_w_gates, a_b_gates, a_norm_w, a_skip, a_w_down, b_w_in, b_conv_w, b_conv_b, b_dt_bias, b_a_log, b_d_skip, b_norm_w, b_w_out, moe_w_rg, moe_b_rg, moe_w_re, moe_b_re, moe_w_gate, moe_w_up, moe_w_down):
    bp, lp, _ = x_prompt.shape
    bs, ls, _ = x_sample.shape
    tp, ts = bp * lp, bs * ls
    assert lp % SCAN_CHUNK == 0 and ls % SCAN_CHUNK == 0 and tp % 512 == 0 and ls % 512 == 0
    assert bs + 1 <= N_MOD_SEQ
    assert all(n == SCAN_CHUNK or (n // SCAN_CHUNK) % 2 == 0 for n in (lp, ls))

    cvec = jnp.concatenate([c_ctx[None, :], c, jnp.zeros((N_MOD_SEQ - 1 - bs, D_MODEL), F32)], axis=0)
    mods = _ada(cvec, ada_w, ada_b).reshape(DEPTH, N_MOD_SEQ, 6, D_MODEL)
    mods = jnp.pad(mods, ((0, 0), (0, 0), (0, MOD_ROWS - 6), (0, 0)))

    t = tp + ts
    x_src = (x_prompt.reshape(tp, D_MODEL), 0, x_sample.reshape(ts, D_MODEL), 0)

    mod = mods[0]
    (proj,) = _modmm(x_src, t, mod, a_w_in[0].astype(BF16), 0, tp, ls, 0, 1, "mlstm_in_proj")
    wqk = jnp.concatenate([_blockdiag_lanes(a_w_q[0]), _blockdiag_lanes(a_w_k[0])], axis=-1).astype(BF16)
    wv = _blockdiag_lanes(a_w_v[0]).astype(BF16)
    ngate = 4 * H_A
    wg = jnp.pad(a_w_gates[0].reshape(3, E_A, ngate), ((0, 0), (0, 0), (0, LANES - ngate))).astype(BF16)
    bg = jnp.pad(a_b_gates[0], (0, LANES - ngate)).reshape(1, LANES)
    xc, q, k, v, g = _conv_qkv(proj, a_conv_w[0], a_conv_b[0], wqk, wv, wg, bg, tp, lp, ls)
    gt = g[:, :ngate].T
    hs_p, c_new, n_new, m_new = _mlstm_scan(q, k, v, gt, 0, bp, lp, emit_state=True)
    (hs_s,) = _mlstm_scan(q, k, v, gt, tp, bs, ls,
                          init=(state_mlstm_C[:, 0], state_mlstm_n[:, 0], state_mlstm_m[:, 0]))
    x, hm, comb = _mixer_out(hs_p, hs_s, xc, proj, 1, x_src, t, mod, a_norm_w[0], a_skip[0],
                             a_w_down[0].astype(BF16), ln_w[0, 0], ln_b[0, 0],
                             *_router_weights(moe_w_rg[0], moe_b_rg[0], moe_w_re[0], moe_b_re[0]),
                             tp, ls, True, "mlstm_out")
    x_p, x_s = _moe(hm, comb, *_expert_weights(moe_w_gate[0], moe_w_up[0], moe_w_down[0]), x, mod,
                    ln_w[0, 1], ln_b[0, 1], tp, ls)

    mod = mods[1]
    x_src = (x_p, 0, _grid_to_cols(x_s.reshape(bs, ls, D_MODEL)).reshape(ts, D_MODEL), 0)
    w_in = jnp.pad(b_w_in[0].astype(BF16), ((0, 0), (0, -b_w_in.shape[2] % LANES)))
    proj, dt = _modmm(x_src, t, mod, w_in, LANES, tp, ls, 0, 1, "ssd_in_proj")
    dt = dt[:, :2 * H_B]
    xbc = _conv_silu(proj, E_B, E_B + 2 * GN_B, b_conv_w[0], b_conv_b[0], tp, lp, ls, "ssd_conv")
    lane_bcast = (2, G_B, HG_B, LANES)
    a_neg = jnp.broadcast_to(-jnp.exp(b_a_log[0]).reshape(2, G_B, HG_B, 1), lane_bcast)
    dt_bias = jnp.broadcast_to(b_dt_bias[0].reshape(2, G_B, HG_B, 1), lane_bcast)
    dtr = dt.T.reshape(2, G_B, HG_B, tp + ts)
    y_p, s_new = _ssd_scan(xbc, dtr, a_neg, dt_bias, 0, bp, lp, emit_state=True)
    (y_s,) = _ssd_scan(xbc, dtr, a_neg, dt_bias, tp, bs, ls, init=state_ssd[:, 0])
    x, hm, comb = _mixer_out(y_p, y_s, xbc, proj, 0, x_src, t, mod, b_norm_w[0], jnp.repeat(b_d_skip[0], P_B),
                             b_w_out[0].astype(BF16), ln_w[1, 0], ln_b[1, 0],
                             *_router_weights(moe_w_rg[1], moe_b_rg[1], moe_w_re[1], moe_b_re[1]),
                             tp, ls, False, "ssd_out")
    x_p, x_s = _moe(hm, comb, *_expert_weights(moe_w_gate[1], moe_w_up[1], moe_w_down[1]), x, mod,
                    ln_w[1, 1], ln_b[1, 1], tp, ls)

    y_prompt = x_p.reshape(bp, lp, D_MODEL)
    y_sample = _cols_to_grid(x_s.reshape(bs, ls, D_MODEL))
    return (y_prompt, y_sample,
            c_new[:, None],
            n_new.reshape(bp, 1, 2, H_A, DH_A),
            m_new[:, :, :, 0, 0][:, None],
            s_new[:, None])
```
